```python
import math
import jax, jax.numpy as jnp
from jax import lax
import numpy as np

D_MODEL = 1024
BATCH = 1
SEQ = 16384
DEPTH = 1

HEAD_DIM = 64
FOX_HEADS = 8
FOX_WIDTH = FOX_HEADS * HEAD_DIM
DIFF_HEADS = 4
DIFF_QK_DIM = 64
DIFF_V_DIM = 2 * DIFF_QK_DIM
DIFF_WIDTH = DIFF_HEADS * DIFF_V_DIM
MIX_WIDTH = FOX_WIDTH + DIFF_WIDTH
D_FF = ((8 * D_MODEL + 767) // 768) * 256
BLOCK_Q = 128
EPS = 1e-6
NEG_INF = -1e30

SPLIT_SIZES = (
    FOX_WIDTH,
    FOX_WIDTH,
    FOX_WIDTH,
    FOX_HEADS,
    2 * DIFF_HEADS * DIFF_QK_DIM,
    2 * DIFF_HEADS * DIFF_QK_DIM,
    DIFF_WIDTH,
)
IN_WIDTH = sum(SPLIT_SIZES)

kernel_name = "hybrid_fox_diffattn_parallel_heads"


def rms_norm(x, g):
    xf = x.astype(jnp.float32)
    y = xf * lax.rsqrt(jnp.mean(xf * xf, axis=-1, keepdims=True) + EPS)
    return (y * g.astype(jnp.float32)).astype(x.dtype)


def alibi_slopes(n):
    return 2.0 ** (-8.0 * jnp.arange(1, n + 1, dtype=jnp.float32) / n)


def fox_attention(q, k, v, log_f):
    b, h, s_len, d = q.shape
    scale = 1.0 / math.sqrt(d)
    c = jnp.cumsum(log_f, axis=-1)
    pos = jnp.arange(s_len)

    def one_block(i):
        t0 = i * BLOCK_Q
        qb = lax.dynamic_slice_in_dim(q, t0, BLOCK_Q, axis=2)
        cb = lax.dynamic_slice_in_dim(c, t0, BLOCK_Q, axis=2)
        tq = t0 + jnp.arange(BLOCK_Q)
        sc = jnp.einsum('bhqd,bhkd->bhqk', qb, k).astype(jnp.float32) * scale
        sc = sc + (cb[..., :, None] - c[..., None, :])
        sc = jnp.where(pos[None, :] <= tq[:, None], sc, NEG_INF)
        p = jax.nn.softmax(sc, axis=-1)
        return jnp.einsum('bhqk,bhkd->bhqd', p.astype(v.dtype), v)

    out = lax.map(one_block, jnp.arange(s_len // BLOCK_Q))
    return jnp.transpose(out, (1, 2, 0, 3, 4)).reshape(b, h, s_len, d)


def diff_attention(q, k, v, lam, slopes):
    b, h, _, s_len, d = q.shape
    dv = v.shape[-1]
    scale = 1.0 / math.sqrt(d)
    pos = jnp.arange(s_len)

    def one_block(i):
        t0 = i * BLOCK_Q
        qb = lax.dynamic_slice_in_dim(q, t0, BLOCK_Q, axis=3)
        tq = t0 + jnp.arange(BLOCK_Q)
        sc = jnp.einsum('bhcqd,bhckd->bhcqk', qb, k).astype(jnp.float32) * scale
        dist = (tq[:, None] - pos[None, :]).astype(jnp.float32)
        sc = sc - slopes[:, None, None, None] * dist
        sc = jnp.where(pos[None, :] <= tq[:, None], sc, NEG_INF)
        p = jax.nn.softmax(sc, axis=-1)
        a = p[:, :, 0] - lam * p[:, :, 1]
        return jnp.einsum('bhqk,bhkd->bhqd', a.astype(v.dtype), v)

    out = lax.map(one_block, jnp.arange(s_len // BLOCK_Q))
    return jnp.transpose(out, (1, 2, 0, 3, 4)).reshape(b, h, s_len, dv)


def setup_inputs(seed: int = 0) -> dict:
    key = jax.random.key(seed)
    ks = jax.random.split(key, 16)
    f32 = jnp.float32
    nrm = lambda k, shape, s: jax.random.normal(k, shape, f32) * s
    x = jax.random.normal(ks[0], (BATCH, SEQ, D_MODEL), f32)
    mix_norm_g = 1.0 + nrm(ks[1], (DEPTH, D_MODEL), 0.02)
    w_in = nrm(ks[2], (DEPTH, D_MODEL, IN_WIDTH), D_MODEL ** -0.5)
    b_forget = 3.0 + nrm(ks[3], (DEPTH, FOX_HEADS), 0.5)
    lambda_q1 = nrm(ks[4], (DEPTH, DIFF_QK_DIM), 0.1)
    lambda_k1 = nrm(ks[5], (DEPTH, DIFF_QK_DIM), 0.1)
    lambda_q2 = nrm(ks[6], (DEPTH, DIFF_QK_DIM), 0.1)
    lambda_k2 = nrm(ks[7], (DEPTH, DIFF_QK_DIM), 0.1)
    diff_norm_g = 1.0 + nrm(ks[8], (DEPTH, DIFF_V_DIM), 0.02)
    w_out = nrm(ks[9], (DEPTH, MIX_WIDTH, D_MODEL), MIX_WIDTH ** -0.5)
    ffn_norm_g = 1.0 + nrm(ks[10], (DEPTH, D_MODEL), 0.02)
    w_gate = nrm(ks[11], (DEPTH, D_MODEL, D_FF), D_MODEL ** -0.5)
    w_up = nrm(ks[12], (DEPTH, D_MODEL, D_FF), D_MODEL ** -0.5)
    w_down = nrm(ks[13], (DEPTH, D_FF, D_MODEL), D_FF ** -0.5)
    final_norm_g = 1.0 + nrm(ks[14], (D_MODEL,), 0.02)
    return {"x": x, "mix_norm_g": mix_norm_g, "w_in": w_in, "b_forget": b_forget,
            "lambda_q1": lambda_q1, "lambda_k1": lambda_k1,
            "lambda_q2": lambda_q2, "lambda_k2": lambda_k2,
            "diff_norm_g": diff_norm_g, "w_out": w_out, "ffn_norm_g": ffn_norm_g,
            "w_gate": w_gate, "w_up": w_up, "w_down": w_down,
            "final_norm_g": final_norm_g}


def reference(x, mix_norm_g, w_in, b_forget, lambda_q1, lambda_k1, lambda_q2,
              lambda_k2, diff_norm_g, w_out, ffn_norm_g, w_gate, w_up, w_down,
              final_norm_g):
    b, s_len, _ = x.shape
    slopes = alibi_slopes(DIFF_HEADS)
    offsets = list(np.cumsum(SPLIT_SIZES)[:-1])
    for l in range(DEPTH):
        h = rms_norm(x, mix_norm_g[l])
        proj = jnp.einsum('bsd,de->bse', h, w_in[l])
        fq, fk, fv, flog, dq, dk, dv = jnp.split(proj, offsets, axis=-1)

        to_heads = lambda t: jnp.transpose(t.reshape(b, s_len, FOX_HEADS, HEAD_DIM), (0, 2, 1, 3))
        log_f = jax.nn.log_sigmoid(flog.astype(jnp.float32) + b_forget[l].astype(jnp.float32))
        log_f = jnp.transpose(log_f, (0, 2, 1))
        fox_out = fox_attention(to_heads(fq), to_heads(fk), to_heads(fv), log_f)
        fox_out = jnp.transpose(fox_out, (0, 2, 1, 3)).reshape(b, s_len, FOX_WIDTH)

        qk_heads = lambda t: jnp.transpose(t.reshape(b, s_len, DIFF_HEADS, 2, DIFF_QK_DIM), (0, 2, 3, 1, 4))
        dv_h = jnp.transpose(dv.reshape(b, s_len, DIFF_HEADS, DIFF_V_DIM), (0, 2, 1, 3))
        lam_init = 0.8 - 0.6 * math.exp(-0.3 * l)
        lam = (jnp.exp(jnp.sum(lambda_q1[l].astype(jnp.float32) * lambda_k1[l].astype(jnp.float32)))
               - jnp.exp(jnp.sum(lambda_q2[l].astype(jnp.float32) * lambda_k2[l].astype(jnp.float32)))
               + lam_init)
        diff_out = diff_attention(qk_heads(dq), qk_heads(dk), dv_h, lam, slopes)
        diff_out = rms_norm(diff_out, diff_norm_g[l]) * (1.0 - lam_init)
        diff_out = jnp.transpose(diff_out, (0, 2, 1, 3)).reshape(b, s_len, DIFF_WIDTH)

        mixed = jnp.concatenate([fox_out, diff_out], axis=-1)
        x = x + jnp.einsum('bse,ed->bsd', mixed, w_out[l])

        h = rms_norm(x, ffn_norm_g[l])
        g = jnp.einsum('bsd,df->bsf', h, w_gate[l])
        u = jnp.einsum('bsd,df->bsf', h, w_up[l])
        x = x + jnp.einsum('bsf,fd->bsd', jax.nn.silu(g) * u, w_down[l])
    return rms_norm(x, final_norm_g)
```

```python
import functools
import math

import jax
import jax.numpy as jnp
from jax import lax
from jax.experimental import pallas as pl
from jax.experimental.pallas import tpu as pltpu

D_MODEL = 1024
HEAD_DIM = 64
FOX_HEADS = 8
FOX_WIDTH = FOX_HEADS * HEAD_DIM
DIFF_HEADS = 4
DIFF_V_DIM = 2 * HEAD_DIM
DIFF_WIDTH = DIFF_HEADS * DIFF_V_DIM
D_FF = ((8 * D_MODEL + 767) // 768) * 256
EPS = 1e-6
NEG_INF = -1e30
LAMBDA_INIT = 0.8 - 0.6 * math.exp(-0.3 * 0)

LANES = 128
GATE_PAD = LANES
QKV_WIDTH = 3 * FOX_WIDTH + 3 * DIFF_WIDTH
VMEM_LIMIT = 56 * 1024 * 1024

TM_IN = 512
TQ = 512
TK = 512
TM_FFN = 512

_NT = (((1,), (1,)), ((), ()))


def _rms(x, g):
    return x * lax.rsqrt(jnp.mean(x * x, axis=-1, keepdims=True) + EPS) * g


def _in_proj_kernel(x_ref, g_ref, w_ref, b_ref, proj_ref, c_ref, tri_ref, carry_ref):
    step = pl.program_id(0)
    tm = x_ref.shape[0]

    @pl.when(step == 0)
    def _():
        r = lax.broadcasted_iota(jnp.int32, (tm, tm), 0)
        c = lax.broadcasted_iota(jnp.int32, (tm, tm), 1)
        tri_ref[...] = jnp.where(r <= c, 1.0, 0.0).astype(jnp.bfloat16)
        carry_ref[...] = jnp.zeros_like(carry_ref)

    h = _rms(x_ref[...], g_ref[...]).astype(jnp.bfloat16)
    y = jnp.dot(h, w_ref[...], preferred_element_type=jnp.float32)
    proj_ref[...] = y[:, :QKV_WIDTH].astype(jnp.bfloat16)

    z = y[:, QKV_WIDTH:] + b_ref[...]
    logf = jnp.minimum(z, 0.0) - jnp.log1p(jnp.exp(-jnp.abs(z)))
    lt = logf.T[:FOX_HEADS, :]
    hi = lt.astype(jnp.bfloat16)
    r1 = lt - hi.astype(jnp.float32)
    mid = r1.astype(jnp.bfloat16)
    lo = (r1 - mid.astype(jnp.float32)).astype(jnp.bfloat16)
    parts = jnp.concatenate([hi, mid, lo], axis=0)
    cs = jnp.dot(parts, tri_ref[...], preferred_element_type=jnp.float32)
    c = (cs[0:FOX_HEADS] + cs[FOX_HEADS:2 * FOX_HEADS] + cs[2 * FOX_HEADS:]
         + carry_ref[:, 0:1])
    c_ref[...] = c
    carry_ref[...] = jnp.broadcast_to(c[:, tm - 1:tm], carry_ref.shape)


def _in_proj(x, g, w, b):
    s = x.shape[0]
    n = w.shape[1]
    return pl.pallas_call(
        _in_proj_kernel,
        grid=(s // TM_IN,),
        in_specs=[
            pl.BlockSpec((TM_IN, D_MODEL), lambda i: (i, 0)),
            pl.BlockSpec((1, D_MODEL), lambda i: (0, 0)),
            pl.BlockSpec((D_MODEL, n), lambda i: (0, 0)),
            pl.BlockSpec((1, GATE_PAD), lambda i: (0, 0)),
        ],
        out_specs=[
            pl.BlockSpec((TM_IN, QKV_WIDTH), lambda i: (i, 0)),
            pl.BlockSpec((FOX_HEADS, TM_IN), lambda i: (0, i)),
        ],
        out_shape=[
            jax.ShapeDtypeStruct((s, QKV_WIDTH), jnp.bfloat16),
            jax.ShapeDtypeStruct((FOX_HEADS, s), jnp.float32),
        ],
        scratch_shapes=[
            pltpu.VMEM((TM_IN, TM_IN), jnp.bfloat16),
            pltpu.VMEM((FOX_HEADS, LANES), jnp.float32),
        ],
        compiler_params=pltpu.CompilerParams(
            dimension_semantics=("arbitrary",), vmem_limit_bytes=VMEM_LIMIT),
        name="in_proj",
    )(x, g, w, b)


def _attn_kernel(*refs, fox):
    if fox:
        q_ref, k_ref, v_ref, c_ref, o_ref, m_ref, l_ref, acc_ref = refs
    else:
        (q_ref, k_ref, v_ref, lq1_ref, lk1_ref, lq2_ref, lk2_ref, gn_ref,
         o_ref, m_ref, l_ref, acc_ref) = refs
    grp = pl.program_id(0)
    i = pl.program_id(1)
    tq, tk = TQ, TK
    rep = tk // LANES

    lane = lax.broadcasted_iota(jnp.int32, (1, LANES), 1)
    lo_half = lane < HEAD_DIM
    q = q_ref[...] * jnp.asarray(1.0 / math.sqrt(HEAD_DIM), jnp.bfloat16)

    m_ref[...] = jnp.full_like(m_ref, NEG_INF)
    l_ref[...] = jnp.zeros_like(l_ref)
    acc_ref[...] = jnp.zeros_like(acc_ref)

    t0 = pl.multiple_of(i * tq, tq)
    if fox:
        c_t0 = [c_ref[h:h + 1, pl.ds(t0, LANES)][:, 0:1] for h in range(2)]
    else:
        slope = jnp.exp2(-2.0 * jnp.full((1, 1), grp + 1, jnp.int32).astype(jnp.float32))
        kpos = lax.broadcasted_iota(jnp.int32, (1, tk), 1)

    def tile(j, masked):
        k0 = pl.multiple_of(j * tk, tk)
        k = k_ref[pl.ds(k0, tk), :]
        v = v_ref[pl.ds(k0, tk), :]
        zero = jnp.zeros_like(k)
        k_half = (jnp.where(lo_half, k, zero), jnp.where(lo_half, zero, k))
        if fox:
            v_half = (jnp.where(lo_half, v, zero), jnp.where(lo_half, zero, v))
            bias = [c_t0[h] - c_ref[h:h + 1, pl.ds(k0, tk)] for h in range(2)]
        else:
            v_half = (v, v)
            b = slope * (kpos + (k0 - t0)).astype(jnp.float32)
            bias = [b, b]
        if masked:
            r = lax.broadcasted_iota(jnp.int32, (tq, tk), 0)
            c = lax.broadcasted_iota(jnp.int32, (tq, tk), 1)
            causal = c <= r
        alphas, pvs = [], []
        for h in range(2):
            s = lax.dot_general(q, k_half[h], _NT, preferred_element_type=jnp.float32)
            s = s + bias[h]
            if masked:
                s = jnp.where(causal, s, NEG_INF)
            m_prev = m_ref[h]
            m_next = jnp.maximum(m_prev, jnp.max(s, axis=1, keepdims=True))
            alpha = jnp.exp(m_prev - m_next)
            p = jnp.exp(s - jnp.tile(m_next, (1, rep)))
            l_ref[h] = alpha * l_ref[h] + jnp.sum(p, axis=1, keepdims=True)
            m_ref[h] = m_next
            alphas.append(alpha)
            pvs.append(jnp.dot(p.astype(jnp.bfloat16), v_half[h],
                               preferred_element_type=jnp.float32))
        if fox:
            acc_ref[0] = acc_ref[0] * jnp.where(lo_half, alphas[0], alphas[1]) + (pvs[0] + pvs[1])
        else:
            acc_ref[0] = acc_ref[0] * alphas[0] + pvs[0]
            acc_ref[1] = acc_ref[1] * alphas[1] + pvs[1]

    def body(j, carry):
        tile(j, masked=False)
        return carry

    lax.fori_loop(0, i, body, 0)
    tile(i, masked=True)

    if fox:
        out = acc_ref[0] / jnp.where(lo_half, l_ref[0], l_ref[1])
    else:
        lam = (jnp.exp(jnp.sum(lq1_ref[...] * lk1_ref[...], axis=1, keepdims=True))
               - jnp.exp(jnp.sum(lq2_ref[...] * lk2_ref[...], axis=1, keepdims=True))
               + LAMBDA_INIT)
        a = acc_ref[0] / l_ref[0] - lam * (acc_ref[1] / l_ref[1])
        out = _rms(a, gn_ref[...]) * (1.0 - LAMBDA_INIT)
    o_ref[...] = out.astype(o_ref.dtype)


def _attention(proj, extra, *, fox):
    s = proj.shape[0]
    groups = FOX_HEADS // 2 if fox else DIFF_HEADS
    base = 0 if fox else 3 * (FOX_WIDTH // LANES)
    per = FOX_WIDTH // LANES if fox else DIFF_WIDTH // LANES
    in_specs = [
        pl.BlockSpec((TQ, LANES), lambda g, i: (i, base + g)),
        pl.BlockSpec((s, LANES), lambda g, i: (0, base + per + g)),
        pl.BlockSpec((s, LANES), lambda g, i: (0, base + 2 * per + g)),
    ]
    args = [proj, proj, proj]
    if fox:
        (c,) = extra
        in_specs.append(pl.BlockSpec((None, 2, s), lambda g, i: (g, 0, 0)))
        args.append(c)
        n_acc = 1
    else:
        for a in extra:
            in_specs.append(pl.BlockSpec(a.shape, lambda g, i: (0, 0)))
            args.append(a)
        n_acc = 2
    return pl.pallas_call(
        functools.partial(_attn_kernel, fox=fox),
        grid=(groups, s // TQ),
        in_specs=in_specs,
        out_specs=pl.BlockSpec((TQ, LANES), lambda g, i: (i, g)),
        out_shape=jax.ShapeDtypeStruct((s, groups * LANES), jnp.bfloat16),
        scratch_shapes=[
            pltpu.VMEM((2, TQ, LANES), jnp.float32),
            pltpu.VMEM((2, TQ, LANES), jnp.float32),
            pltpu.VMEM((n_acc, TQ, LANES), jnp.float32),
        ],
        compiler_params=pltpu.CompilerParams(
            dimension_semantics=("arbitrary", "arbitrary"), vmem_limit_bytes=VMEM_LIMIT),
        name="fox_attn" if fox else "diff_attn",
    )(*args)


def _out_ffn_kernel(x_ref, fo_ref, do_ref, wo_ref, gf_ref, wg_ref, wu_ref, wd_ref,
                    gl_ref, o_ref):
    x1 = (x_ref[...]
          + jnp.dot(fo_ref[...], wo_ref[:FOX_WIDTH, :], preferred_element_type=jnp.float32)
          + jnp.dot(do_ref[...], wo_ref[FOX_WIDTH:, :], preferred_element_type=jnp.float32))
    h = _rms(x1, gf_ref[...]).astype(jnp.bfloat16)
    g = jnp.dot(h, wg_ref[...], preferred_element_type=jnp.float32)
    u = jnp.dot(h, wu_ref[...], preferred_element_type=jnp.float32)
    a = (g * jax.nn.sigmoid(g) * u).astype(jnp.bfloat16)
    x2 = x1 + jnp.dot(a, wd_ref[...], preferred_element_type=jnp.float32)
    o_ref[...] = _rms(x2, gl_ref[...])


def _out_ffn(x, fo, do, wo, gf, wg, wu, wd, gl):
    s = x.shape[0]
    const = lambda shape: pl.BlockSpec(shape, lambda i: (0, 0), pipeline_mode=pl.Buffered(1))
    return pl.pallas_call(
        _out_ffn_kernel,
        grid=(s // TM_FFN,),
        in_specs=[
            pl.BlockSpec((TM_FFN, D_MODEL), lambda i: (i, 0)),
            pl.BlockSpec((TM_FFN, FOX_WIDTH), lambda i: (i, 0)),
            pl.BlockSpec((TM_FFN, DIFF_WIDTH), lambda i: (i, 0)),
            const(wo.shape), const(gf.shape), const(wg.shape), const(wu.shape),
            const(wd.shape), const(gl.shape),
        ],
        out_specs=pl.BlockSpec((TM_FFN, D_MODEL), lambda i: (i, 0)),
        out_shape=jax.ShapeDtypeStruct((s, D_MODEL), jnp.float32),
        compiler_params=pltpu.CompilerParams(
            dimension_semantics=("arbitrary",), vmem_limit_bytes=VMEM_LIMIT),
        name="out_ffn",
    )(x, fo, do, wo, gf, wg, wu, wd, gl)


def kernel(x, mix_norm_g, w_in, b_forget, lambda_q1, lambda_k1, lambda_q2, lambda_k2,
           diff_norm_g, w_out, ffn_norm_g, w_gate, w_up, w_down, final_norm_g):
    b, s, d = x.shape
    assert b == 1 and d == D_MODEL and w_in.shape[0] == 1
    assert s % TQ == 0 and s % TM_IN == 0 and s % TM_FFN == 0
    bf16 = jnp.bfloat16
    x2d = x.reshape(s, d)

    w = w_in[0]
    o_gate = 3 * FOX_WIDTH
    w_r = jnp.concatenate(
        [w[:, :o_gate], w[:, o_gate + FOX_HEADS:], w[:, o_gate:o_gate + FOX_HEADS],
         jnp.zeros((d, GATE_PAD - FOX_HEADS), w.dtype)], axis=1).astype(bf16)
    b_pad = jnp.pad(b_forget[0].astype(jnp.float32), (0, GATE_PAD - FOX_HEADS)).reshape(1, GATE_PAD)

    proj, c = _in_proj(x2d, mix_norm_g[0].reshape(1, d), w_r, b_pad)
    c_pairs = c.reshape(FOX_HEADS // 2, 2, s)

    fox_out = _attention(proj, (c_pairs,), fox=True)
    lam_args = tuple(a[0].astype(jnp.float32).reshape(1, HEAD_DIM)
                     for a in (lambda_q1, lambda_k1, lambda_q2, lambda_k2))
    diff_out = _attention(
        proj, lam_args + (diff_norm_g[0].astype(jnp.float32).reshape(1, DIFF_V_DIM),), fox=False)

    out = _out_ffn(
        x2d, fox_out, diff_out, w_out[0].astype(bf16), ffn_norm_g[0].reshape(1, d),
        w_gate[0].astype(bf16), w_up[0].astype(bf16), w_down[0].astype(bf16),
        final_norm_g.reshape(1, d))
    return out.reshape(b, s, d)
```

```python
import functools
import math

import jax
import jax.numpy as jnp
from jax import lax
from jax.experimental import pallas as pl
from jax.experimental.pallas import tpu as pltpu

D_MODEL = 1024
HEAD_DIM = 64
FOX_HEADS = 8
FOX_WIDTH = FOX_HEADS * HEAD_DIM
DIFF_HEADS = 4
DIFF_V_DIM = 2 * HEAD_DIM
DIFF_WIDTH = DIFF_HEADS * DIFF_V_DIM
D_FF = ((8 * D_MODEL + 767) // 768) * 256
EPS = 1e-6
NEG_INF = -1e30
UNDERFLOW = 105.0
LAMBDA_INIT = 0.8 - 0.6 * math.exp(-0.3 * 0)

LANES = 128
GATE_PAD = LANES
QKV_WIDTH = 3 * FOX_WIDTH + 3 * DIFF_WIDTH
VMEM_LIMIT = 56 * 1024 * 1024

TM_IN = 512
TQ = 512
TK = 512
TM_FFN = 512

_NT = (((1,), (1,)), ((), ()))


def _rms(x, g):
    return x * lax.rsqrt(jnp.mean(x * x, axis=-1, keepdims=True) + EPS) * g


def _in_proj_kernel(x_ref, g_ref, w_ref, b_ref, proj_ref, c_ref, cend_ref, tri_ref, carry_ref):
    step = pl.program_id(0)
    tm = x_ref.shape[0]

    @pl.when(step == 0)
    def _():
        r = lax.broadcasted_iota(jnp.int32, (tm, tm), 0)
        c = lax.broadcasted_iota(jnp.int32, (tm, tm), 1)
        tri_ref[...] = jnp.where(r <= c, 1.0, 0.0).astype(jnp.bfloat16)
        carry_ref[...] = jnp.zeros_like(carry_ref)
        cend_ref[...] = jnp.zeros_like(cend_ref)

    h = _rms(x_ref[...], g_ref[...]).astype(jnp.bfloat16)
    y = jnp.dot(h, w_ref[...], preferred_element_type=jnp.float32)
    proj_ref[...] = y[:, :QKV_WIDTH].astype(jnp.bfloat16)

    z = y[:, QKV_WIDTH:] + b_ref[...]
    logf = jnp.minimum(z, 0.0) - jnp.log1p(jnp.exp(-jnp.abs(z)))
    lt = logf.T[:FOX_HEADS, :]
    hi = lt.astype(jnp.bfloat16)
    r1 = lt - hi.astype(jnp.float32)
    mid = r1.astype(jnp.bfloat16)
    lo = (r1 - mid.astype(jnp.float32)).astype(jnp.bfloat16)
    parts = jnp.concatenate([hi, mid, lo], axis=0)
    cs = jnp.dot(parts, tri_ref[...], preferred_element_type=jnp.float32)
    c = (cs[0:FOX_HEADS] + cs[FOX_HEADS:2 * FOX_HEADS] + cs[2 * FOX_HEADS:]
         + carry_ref[:, 0:1])
    c_ref[...] = c
    carry_ref[...] = jnp.broadcast_to(c[:, tm - 1:tm], carry_ref.shape)
    tile_lane = lax.broadcasted_iota(jnp.int32, cend_ref.shape, 1)
    cend_ref[...] = jnp.where(tile_lane == step, carry_ref[...], cend_ref[...])


def _in_proj(x, g, w, b):
    s = x.shape[0]
    n = w.shape[1]
    return pl.pallas_call(
        _in_proj_kernel,
        grid=(s // TM_IN,),
        in_specs=[
            pl.BlockSpec((TM_IN, D_MODEL), lambda i: (i, 0)),
            pl.BlockSpec((1, D_MODEL), lambda i: (0, 0)),
            pl.BlockSpec((D_MODEL, n), lambda i: (0, 0)),
            pl.BlockSpec((1, GATE_PAD), lambda i: (0, 0)),
        ],
        out_specs=[
            pl.BlockSpec((TM_IN, QKV_WIDTH), lambda i: (i, 0)),
            pl.BlockSpec((FOX_HEADS, TM_IN), lambda i: (0, i)),
            pl.BlockSpec((FOX_HEADS, LANES), lambda i: (0, 0)),
        ],
        out_shape=[
            jax.ShapeDtypeStruct((s, QKV_WIDTH), jnp.bfloat16),
            jax.ShapeDtypeStruct((FOX_HEADS, s), jnp.float32),
            jax.ShapeDtypeStruct((FOX_HEADS, LANES), jnp.float32),
        ],
        scratch_shapes=[
            pltpu.VMEM((TM_IN, TM_IN), jnp.bfloat16),
            pltpu.VMEM((FOX_HEADS, LANES), jnp.float32),
        ],
        compiler_params=pltpu.CompilerParams(
            dimension_semantics=("arbitrary",), vmem_limit_bytes=VMEM_LIMIT),
        name="in_proj",
    )(x, g, w, b)


def _attn_kernel(*refs, fox):
    if fox:
        q_ref, k_ref, v_ref, c_ref, cend_ref, o_ref, m_ref, l_ref, acc_ref, kmax_ref = refs
    else:
        (q_ref, k_ref, v_ref, lq1_ref, lk1_ref, lq2_ref, lk2_ref, gn_ref,
         o_ref, m_ref, l_ref, acc_ref, kmax_ref) = refs
    grp = pl.program_id(0)
    i = pl.program_id(1)
    tq, tk = TQ, TK
    rep = tk // LANES
    s_len = k_ref.shape[0]

    lane = lax.broadcasted_iota(jnp.int32, (1, LANES), 1)
    lo_half = lane < HEAD_DIM
    q = q_ref[...] * jnp.asarray(1.0 / math.sqrt(HEAD_DIM), jnp.bfloat16)

    def half_norms_sq(x):
        sq = x.astype(jnp.float32)
        sq = sq * sq
        return (jnp.sum(jnp.where(lo_half, sq, 0.0), axis=1, keepdims=True),
                jnp.sum(jnp.where(lo_half, 0.0, sq), axis=1, keepdims=True))

    @pl.when(i == 0)
    def _():
        def kbody(cidx, mx):
            n = half_norms_sq(k_ref[pl.ds(pl.multiple_of(cidx * tk, tk), tk), :])
            return (jnp.maximum(mx[0], n[0]), jnp.maximum(mx[1], n[1]))
        zero = jnp.zeros((tk, 1), jnp.float32)
        mx = lax.fori_loop(0, s_len // tk, kbody, (zero, zero))
        for h in range(2):
            kmax_ref[h] = jnp.broadcast_to(
                jnp.sqrt(jnp.max(mx[h], axis=0, keepdims=True)), kmax_ref.shape[1:])

    m_ref[...] = jnp.full_like(m_ref, NEG_INF)
    l_ref[...] = jnp.zeros_like(l_ref)
    acc_ref[...] = jnp.zeros_like(acc_ref)

    t0 = pl.multiple_of(i * tq, tq)
    if fox:
        c_t0 = [c_ref[h:h + 1, pl.ds(t0, LANES)][:, 0:1] for h in range(2)]
    else:
        slope = jnp.exp2(-2.0 * jnp.full((1, 1), grp + 1, jnp.int32).astype(jnp.float32))
        kpos = lax.broadcasted_iota(jnp.int32, (1, tk), 1)

    def tile(j, masked):
        k0 = pl.multiple_of(j * tk, tk)
        k = k_ref[pl.ds(k0, tk), :]
        v = v_ref[pl.ds(k0, tk), :]
        zero = jnp.zeros_like(k)
        k_half = (jnp.where(lo_half, k, zero), jnp.where(lo_half, zero, k))
        if fox:
            v_half = (jnp.where(lo_half, v, zero), jnp.where(lo_half, zero, v))
            bias = [c_t0[h] - c_ref[h:h + 1, pl.ds(k0, tk)] for h in range(2)]
        else:
            v_half = (v, v)
            b = slope * (kpos + (k0 - t0)).astype(jnp.float32)
            bias = [b, b]
        if masked:
            r = lax.broadcasted_iota(jnp.int32, (tq, tk), 0)
            c = lax.broadcasted_iota(jnp.int32, (tq, tk), 1)
            causal = c <= r
        alphas, pvs = [], []
        for h in range(2):
            s = lax.dot_general(q, k_half[h], _NT, preferred_element_type=jnp.float32)
            s = s + bias[h]
            if masked:
                s = jnp.where(causal, s, NEG_INF)
            m_prev = m_ref[h]
            m_next = jnp.maximum(m_prev, jnp.max(s, axis=1, keepdims=True))
            alpha = jnp.exp(m_prev - m_next)
            p = jnp.exp(s - jnp.tile(m_next, (1, rep)))
            l_ref[h] = alpha * l_ref[h] + jnp.sum(p, axis=1, keepdims=True)
            m_ref[h] = m_next
            alphas.append(alpha)
            pvs.append(jnp.dot(p.astype(jnp.bfloat16), v_half[h],
                               preferred_element_type=jnp.float32))
        if fox:
            acc_ref[0] = acc_ref[0] * jnp.where(lo_half, alphas[0], alphas[1]) + (pvs[0] + pvs[1])
        else:
            acc_ref[0] = acc_ref[0] * alphas[0] + pvs[0]
            acc_ref[1] = acc_ref[1] * alphas[1] + pvs[1]

    tile(i, masked=True)

    qn = half_norms_sq(q)
    gap = [jnp.max(jnp.sqrt(qn[h]) * kmax_ref[h][0:1, 0:1] - m_ref[h][:, 0:1],
                   axis=0, keepdims=True) for h in range(2)]
    if fox:
        cnt = []
        for h in range(2):
            dead = (c_t0[h] - cend_ref[h:h + 1, :] < -UNDERFLOW - gap[h]) & (lane < i)
            cnt.append(jnp.sum(dead.astype(jnp.int32)))
        j_lo = jnp.minimum(cnt[0], cnt[1])
    else:
        thr = -UNDERFLOW - jnp.maximum(gap[0], gap[1])
        x = jnp.full((1, 1), t0, jnp.int32).astype(jnp.float32) + thr / slope
        j_lo = jnp.clip(jnp.floor(x * (1.0 / tk)), 0.0, 1.0 * (s_len // tk))
        j_lo = jnp.minimum(j_lo.astype(jnp.int32)[0, 0], i)

    def body(step, carry):
        tile(i - 1 - step, masked=False)
        return carry

    lax.fori_loop(0, i - j_lo, body, 0)

    if fox:
        out = acc_ref[0] / jnp.where(lo_half, l_ref[0], l_ref[1])
    else:
        lam = (jnp.exp(jnp.sum(lq1_ref[...] * lk1_ref[...], axis=1, keepdims=True))
               - jnp.exp(jnp.sum(lq2_ref[...] * lk2_ref[...], axis=1, keepdims=True))
               + LAMBDA_INIT)
        a = acc_ref[0] / l_ref[0] - lam * (acc_ref[1] / l_ref[1])
        out = _rms(a, gn_ref[...]) * (1.0 - LAMBDA_INIT)
    o_ref[...] = out.astype(o_ref.dtype)


def _attention(proj, extra, *, fox):
    s = proj.shape[0]
    groups = FOX_HEADS // 2 if fox else DIFF_HEADS
    base = 0 if fox else 3 * (FOX_WIDTH // LANES)
    per = FOX_WIDTH // LANES if fox else DIFF_WIDTH // LANES
    in_specs = [
        pl.BlockSpec((TQ, LANES), lambda g, i: (i, base + g)),
        pl.BlockSpec((s, LANES), lambda g, i: (0, base + per + g)),
        pl.BlockSpec((s, LANES), lambda g, i: (0, base + 2 * per + g)),
    ]
    args = [proj, proj, proj]
    if fox:
        c, cend = extra
        in_specs.append(pl.BlockSpec((None, 2, s), lambda g, i: (g, 0, 0)))
        in_specs.append(pl.BlockSpec((None, 2, LANES), lambda g, i: (g, 0, 0)))
        args += [c, cend]
        n_acc = 1
    else:
        for a in extra:
            in_specs.append(pl.BlockSpec(a.shape, lambda g, i: (0, 0)))
            args.append(a)
        n_acc = 2
    return pl.pallas_call(
        functools.partial(_attn_kernel, fox=fox),
        grid=(groups, s // TQ),
        in_specs=in_specs,
        out_specs=pl.BlockSpec((TQ, LANES), lambda g, i: (i, g)),
        out_shape=jax.ShapeDtypeStruct((s, groups * LANES), jnp.bfloat16),
        scratch_shapes=[
            pltpu.VMEM((2, TQ, LANES), jnp.float32),
            pltpu.VMEM((2, TQ, LANES), jnp.float32),
            pltpu.VMEM((n_acc, TQ, LANES), jnp.float32),
            pltpu.VMEM((2, 8, LANES), jnp.float32),
        ],
        compiler_params=pltpu.CompilerParams(
            dimension_semantics=("arbitrary", "arbitrary"), vmem_limit_bytes=VMEM_LIMIT),
        name="fox_attn" if fox else "diff_attn",
    )(*args)


def _out_ffn_kernel(x_ref, fo_ref, do_ref, wo_ref, gf_ref, wg_ref, wu_ref, wd_ref,
                    gl_ref, o_ref):
    x1 = (x_ref[...]
          + jnp.dot(fo_ref[...], wo_ref[:FOX_WIDTH, :], preferred_element_type=jnp.float32)
          + jnp.dot(do_ref[...], wo_ref[FOX_WIDTH:, :], preferred_element_type=jnp.float32))
    h = _rms(x1, gf_ref[...]).astype(jnp.bfloat16)
    g = jnp.dot(h, wg_ref[...], preferred_element_type=jnp.float32)
    u = jnp.dot(h, wu_ref[...], preferred_element_type=jnp.float32)
    a = (g * jax.nn.sigmoid(g) * u).astype(jnp.bfloat16)
    x2 = x1 + jnp.dot(a, wd_ref[...], preferred_element_type=jnp.float32)
    o_ref[...] = _rms(x2, gl_ref[...])


def _out_ffn(x, fo, do, wo, gf, wg, wu, wd, gl):
    s = x.shape[0]
    const = lambda shape: pl.BlockSpec(shape, lambda i: (0, 0), pipeline_mode=pl.Buffered(1))
    return pl.pallas_call(
        _out_ffn_kernel,
        grid=(s // TM_FFN,),
        in_specs=[
            pl.BlockSpec((TM_FFN, D_MODEL), lambda i: (i, 0)),
            pl.BlockSpec((TM_FFN, FOX_WIDTH), lambda i: (i, 0)),
            pl.BlockSpec((TM_FFN, DIFF_WIDTH), lambda i: (i, 0)),
            const(wo.shape), const(gf.shape), const(wg.shape), const(wu.shape),
            const(wd.shape), const(gl.shape),
        ],
        out_specs=pl.BlockSpec((TM_FFN, D_MODEL), lambda i: (i, 0)),
        out_shape=jax.ShapeDtypeStruct((s, D_MODEL), jnp.float32),
        compiler_params=pltpu.CompilerParams(
            dimension_semantics=("arbitrary",), vmem_limit_bytes=VMEM_LIMIT),
        name="out_ffn",
    )(x, fo, do, wo, gf, wg, wu, wd, gl)


def kernel(x, mix_norm_g, w_in, b_forget, lambda_q1, lambda_k1, lambda_q2, lambda_k2,
           diff_norm_g, w_out, ffn_norm_g, w_gate, w_up, w_down, final_norm_g):
    b, s, d = x.shape
    assert b == 1 and d == D_MODEL and w_in.shape[0] == 1
    assert s % TQ == 0 and s % TM_IN == 0 and s % TM_FFN == 0
    assert TM_IN == TK and s // TK <= LANES
    bf16 = jnp.bfloat16
    x2d = x.reshape(s, d)

    w = w_in[0]
    o_gate = 3 * FOX_WIDTH
    w_r = jnp.concatenate(
        [w[:, :o_gate], w[:, o_gate + FOX_HEADS:], w[:, o_gate:o_gate + FOX_HEADS],
         jnp.zeros((d, GATE_PAD - FOX_HEADS), w.dtype)], axis=1).astype(bf16)
    b_pad = jnp.pad(b_forget[0].astype(jnp.float32), (0, GATE_PAD - FOX_HEADS)).reshape(1, GATE_PAD)

    proj, c, cend = _in_proj(x2d, mix_norm_g[0].reshape(1, d), w_r, b_pad)
    c_pairs = c.reshape(FOX_HEADS // 2, 2, s)
    cend_pairs = cend.reshape(FOX_HEADS // 2, 2, LANES)

    fox_out = _attention(proj, (c_pairs, cend_pairs), fox=True)
    lam_args = tuple(a[0].astype(jnp.float32).reshape(1, HEAD_DIM)
                     for a in (lambda_q1, lambda_k1, lambda_q2, lambda_k2))
    diff_out = _attention(
        proj, lam_args + (diff_norm_g[0].astype(jnp.float32).reshape(1, DIFF_V_DIM),), fox=False)

    out = _out_ffn(
        x2d, fox_out, diff_out, w_out[0].astype(bf16), ffn_norm_g[0].reshape(1, d),
        w_gate[0].astype(bf16), w_up[0].astype(bf16), w_down[0].astype(bf16),
        final_norm_g.reshape(1, d))
    return out.reshape(b, s, d)
```

```python
import functools
import math

import numpy as np
import jax
import jax.numpy as jnp
from jax import lax
from jax.experimental import pallas as pl
from jax.experimental.pallas import tpu as pltpu

D_MODEL = 1024
HEAD_DIM = 64
FOX_HEADS = 8
FOX_WIDTH = FOX_HEADS * HEAD_DIM
DIFF_HEADS = 4
DIFF_V_DIM = 2 * HEAD_DIM
DIFF_WIDTH = DIFF_HEADS * DIFF_V_DIM
D_FF = ((8 * D_MODEL + 767) // 768) * 256
EPS = 1e-6
NEG_INF = -1e30
LOG2E = math.log2(math.e)
Q_SCALE = LOG2E / math.sqrt(HEAD_DIM)
UNDERFLOW_LOG2 = 150.0
LAMBDA_INIT = 0.8 - 0.6 * math.exp(-0.3 * 0)

LANES = 128
GATE_PAD = LANES
QKV_WIDTH = 3 * FOX_WIDTH + 3 * DIFF_WIDTH
VMEM_LIMIT = 56 * 1024 * 1024

TM_IN = 512
TQ = 512
TK = 512
TM_FFN = 512

_NT = (((1,), (1,)), ((), ()))


def _rms(x, g):
    return x * lax.rsqrt(jnp.mean(x * x, axis=-1, keepdims=True) + EPS) * g


def _split3(x):
    hi = x.astype(jnp.bfloat16)
    r1 = x - hi.astype(jnp.float32)
    mid = r1.astype(jnp.bfloat16)
    lo = (r1 - mid.astype(jnp.float32)).astype(jnp.bfloat16)
    return hi, mid, lo


def _fox_aug_placement():
    p = np.zeros((3 * LANES, FOX_WIDTH), np.float32)
    for part in range(3):
        for head in range(FOX_HEADS):
            lane = HEAD_DIM + part if head % 2 == 0 else part
            p[part * LANES + head, (head // 2) * LANES + lane] = 1.0
    return p


def _in_proj_kernel(x_ref, g_ref, w_ref, b_ref, place_ref, proj_ref, caug_ref, cend_ref,
                    tri_ref, carry_ref):
    step = pl.program_id(0)
    tm = x_ref.shape[0]

    @pl.when(step == 0)
    def _():
        r = lax.broadcasted_iota(jnp.int32, (tm, tm), 0)
        c = lax.broadcasted_iota(jnp.int32, (tm, tm), 1)
        tri_ref[...] = jnp.where(r >= c, 1.0, 0.0).astype(jnp.bfloat16)
        carry_ref[...] = jnp.zeros_like(carry_ref)

    h = _rms(x_ref[...], g_ref[...]).astype(jnp.bfloat16)
    y = jnp.dot(h, w_ref[...], preferred_element_type=jnp.float32)
    bf16 = jnp.bfloat16
    dq0 = 3 * FOX_WIDTH
    proj_ref[:, :FOX_WIDTH] = (y[:, :FOX_WIDTH] * Q_SCALE).astype(bf16)
    proj_ref[:, FOX_WIDTH:dq0] = y[:, FOX_WIDTH:dq0].astype(bf16)
    proj_ref[:, dq0:dq0 + DIFF_WIDTH] = (y[:, dq0:dq0 + DIFF_WIDTH] * Q_SCALE).astype(bf16)
    proj_ref[:, dq0 + DIFF_WIDTH:] = y[:, dq0 + DIFF_WIDTH:QKV_WIDTH].astype(bf16)

    z = y[:, QKV_WIDTH:] + b_ref[...]
    logf = LOG2E * (jnp.minimum(z, 0.0) - jnp.log1p(jnp.exp(-jnp.abs(z))))
    cs = jnp.dot(tri_ref[...], jnp.concatenate(_split3(logf), axis=1),
                 preferred_element_type=jnp.float32)
    c = cs[:, :LANES] + cs[:, LANES:2 * LANES] + cs[:, 2 * LANES:] + carry_ref[0:1, :]
    last = c[tm - 1:tm, :]
    carry_ref[...] = jnp.broadcast_to(last, carry_ref.shape)
    cend_ref[pl.ds(step, 1), :] = last
    caug_ref[...] = jnp.dot(jnp.concatenate(_split3(-c), axis=1), place_ref[...],
                            preferred_element_type=jnp.float32).astype(bf16)


def _in_proj(x, g, w, b, place):
    s = x.shape[0]
    n = w.shape[1]
    n_tiles = s // TM_IN
    return pl.pallas_call(
        _in_proj_kernel,
        grid=(n_tiles,),
        in_specs=[
            pl.BlockSpec((TM_IN, D_MODEL), lambda i: (i, 0)),
            pl.BlockSpec((1, D_MODEL), lambda i: (0, 0)),
            pl.BlockSpec((D_MODEL, n), lambda i: (0, 0)),
            pl.BlockSpec((1, GATE_PAD), lambda i: (0, 0)),
            pl.BlockSpec(place.shape, lambda i: (0, 0)),
        ],
        out_specs=[
            pl.BlockSpec((TM_IN, QKV_WIDTH), lambda i: (i, 0)),
            pl.BlockSpec((TM_IN, FOX_WIDTH), lambda i: (i, 0)),
            pl.BlockSpec((n_tiles, LANES), lambda i: (0, 0)),
        ],
        out_shape=[
            jax.ShapeDtypeStruct((s, QKV_WIDTH), jnp.bfloat16),
            jax.ShapeDtypeStruct((s, FOX_WIDTH), jnp.bfloat16),
            jax.ShapeDtypeStruct((n_tiles, LANES), jnp.float32),
        ],
        scratch_shapes=[
            pltpu.VMEM((TM_IN, TM_IN), jnp.bfloat16),
            pltpu.VMEM((8, LANES), jnp.float32),
        ],
        compiler_params=pltpu.CompilerParams(
            dimension_semantics=("arbitrary",), vmem_limit_bytes=VMEM_LIMIT),
        name="in_proj",
    )(x, g, w, b, place)


def _attn_kernel(*refs, fox):
    if fox:
        (q_ref, k_ref, v_ref, caug_ref, cend_ref,
         o_ref, m_ref, acc_ref, kmax_ref, kaug_ref, vaug_ref) = refs
    else:
        (q_ref, k_ref, v_ref, lq1_ref, lk1_ref, lq2_ref, lk2_ref, gn_ref,
         o_ref, m_ref, acc_ref, kmax_ref, kaug_ref, vaug_ref) = refs
    grp = pl.program_id(0)
    i = pl.program_id(1)
    tq, tk = TQ, TK
    s_len = k_ref.shape[0]
    n_tiles = s_len // tk
    vw = acc_ref.shape[-1]
    f32, bf16 = jnp.float32, jnp.bfloat16

    lane = lax.broadcasted_iota(jnp.int32, (1, LANES), 1)
    lo_half = lane < HEAD_DIM
    sub = lane & (HEAD_DIM - 1)

    if not fox:
        slope = LOG2E * jnp.exp2(-2.0 * jnp.full((1, 1), grp + 1, jnp.int32).astype(f32))
        sl = [p.astype(f32) for p in _split3(slope)]

    def half_norms_sq(x):
        sq = x.astype(f32)
        sq = sq * sq
        return (jnp.sum(jnp.where(lo_half, sq, 0.0), axis=1, keepdims=True),
                jnp.sum(jnp.where(lo_half, 0.0, sq), axis=1, keepdims=True))

    @pl.when(i == 0)
    def _():
        def chunk(cidx, mx):
            rows = pl.ds(pl.multiple_of(cidx * tk, tk), tk)
            k = k_ref[rows, :]
            v = v_ref[rows, :]
            if fox:
                aug = caug_ref[rows, :]
                vaug_ref[0, rows, :] = jnp.where(lo_half, v, (lane == HEAD_DIM).astype(bf16))
                vaug_ref[1, rows, :] = jnp.where(lo_half, (lane == 0).astype(bf16), v)
            else:
                pos = cidx * tk + lax.broadcasted_iota(jnp.int32, (tk, LANES), 0)
                aug = jnp.where(sub < 3, pos & ~(LANES - 1),
                                jnp.where(sub < 6, pos & (LANES - 1), 0)).astype(f32).astype(bf16)
                vaug_ref[rows, :LANES] = v
                vaug_ref[rows, LANES:] = jnp.broadcast_to((lane == 0).astype(bf16), (tk, LANES))
            kaug_ref[0, rows, :] = jnp.where(lo_half, k, aug)
            kaug_ref[1, rows, :] = jnp.where(lo_half, aug, k)
            n = half_norms_sq(k)
            return (jnp.maximum(mx[0], n[0]), jnp.maximum(mx[1], n[1]))
        zero = jnp.zeros((tk, 1), f32)
        mx = lax.fori_loop(0, n_tiles, chunk, (zero, zero))
        for h in range(2):
            kmax_ref[h] = jnp.broadcast_to(
                jnp.sqrt(jnp.max(mx[h], axis=0, keepdims=True)), kmax_ref.shape[1:])

    q = q_ref[...]
    if fox:
        q_aug_lanes = (sub < 3).astype(bf16)
    else:
        q_aug_lanes = jnp.where((sub == 0) | (sub == 3), sl[0],
                                jnp.where((sub == 1) | (sub == 4), sl[1],
                                          jnp.where((sub == 2) | (sub == 5), sl[2], 0.0))
                                ).astype(bf16)
    q_aug = (jnp.where(lo_half, q, q_aug_lanes), jnp.where(lo_half, q_aug_lanes, q))

    m_ref[...] = jnp.full_like(m_ref, NEG_INF)
    acc_ref[...] = jnp.zeros_like(acc_ref)

    def tile(j, masked):
        rows = pl.ds(pl.multiple_of(j * tk, tk), tk)
        if masked:
            r = lax.broadcasted_iota(jnp.int32, (tq, tk), 0)
            c = lax.broadcasted_iota(jnp.int32, (tq, tk), 1)
            causal = c <= r
        for h in range(2):
            s = lax.dot_general(q_aug[h], kaug_ref[h, rows, :], _NT, preferred_element_type=f32)
            if masked:
                s = jnp.where(causal, s, NEG_INF)
            m_prev = m_ref[h]
            m_next = jnp.maximum(m_prev, jnp.max(s, axis=1, keepdims=True))
            alpha = jnp.exp2(m_prev - m_next)
            p = jnp.exp2(s - jnp.tile(m_next, (1, tk // LANES))).astype(bf16)
            v = vaug_ref[h, rows, :] if fox else vaug_ref[rows, :]
            acc_ref[h] = (acc_ref[h] * jnp.tile(alpha, (1, vw // LANES))
                          + jnp.dot(p, v, preferred_element_type=f32))
            m_ref[h] = m_next

    tile(i, masked=True)

    qn = half_norms_sq(q)
    gap = [jnp.max(jnp.sqrt(qn[h]) * kmax_ref[h][0:1, 0:1] - m_ref[h][:, 0:1],
                   axis=0, keepdims=True) for h in range(2)]
    if fox:
        tile_idx = lax.broadcasted_iota(jnp.int32, cend_ref.shape, 0)
        head_lane = lax.broadcasted_iota(jnp.int32, cend_ref.shape, 1)
        cnt = []
        for h in range(2):
            dead = ((-cend_ref[...] < -UNDERFLOW_LOG2 - gap[h]) & (tile_idx < i)
                    & (head_lane == 2 * grp + h))
            cnt.append(jnp.sum(dead.astype(jnp.int32)))
        j_lo = jnp.minimum(cnt[0], cnt[1])
    else:
        x = (-UNDERFLOW_LOG2 - jnp.maximum(gap[0], gap[1])) / slope
        j_lo = jnp.clip(jnp.floor(x * (1.0 / tk)), 0.0, 1.0 * n_tiles)
        j_lo = jnp.minimum(j_lo.astype(jnp.int32)[0, 0], i)

    def body(step, carry):
        tile(i - 1 - step, masked=False)
        return carry

    lax.fori_loop(0, i - j_lo, body, 0)

    if fox:
        out = jnp.where(lo_half, acc_ref[0] / acc_ref[0][:, HEAD_DIM:HEAD_DIM + 1],
                        acc_ref[1] / acc_ref[1][:, 0:1])
    else:
        lam = (jnp.exp(jnp.sum(lq1_ref[...] * lk1_ref[...], axis=1, keepdims=True))
               - jnp.exp(jnp.sum(lq2_ref[...] * lk2_ref[...], axis=1, keepdims=True))
               + LAMBDA_INIT)
        a = (acc_ref[0][:, :LANES] / acc_ref[0][:, LANES:LANES + 1]
             - lam * (acc_ref[1][:, :LANES] / acc_ref[1][:, LANES:LANES + 1]))
        out = _rms(a, gn_ref[...]) * (1.0 - LAMBDA_INIT)
    o_ref[...] = out.astype(o_ref.dtype)


def _attention(proj, extra, *, fox):
    s = proj.shape[0]
    groups = FOX_HEADS // 2 if fox else DIFF_HEADS
    base = 0 if fox else 3 * (FOX_WIDTH // LANES)
    per = FOX_WIDTH // LANES if fox else DIFF_WIDTH // LANES
    resident = lambda col0: pl.BlockSpec((s, LANES), lambda g, i: (0, col0 + g),
                                         pipeline_mode=pl.Buffered(1))
    in_specs = [
        pl.BlockSpec((TQ, LANES), lambda g, i: (i, base + g)),
        resident(base + per),
        resident(base + 2 * per),
    ]
    args = [proj, proj, proj]
    if fox:
        caug, cend = extra
        in_specs.append(resident(0))
        in_specs.append(pl.BlockSpec(cend.shape, lambda g, i: (0, 0)))
        args += [caug, cend]
        vw = LANES
        vaug = pltpu.VMEM((2, s, LANES), jnp.bfloat16)
    else:
        for a in extra:
            in_specs.append(pl.BlockSpec(a.shape, lambda g, i: (0, 0)))
            args.append(a)
        vw = 2 * LANES
        vaug = pltpu.VMEM((s, 2 * LANES), jnp.bfloat16)
    return pl.pallas_call(
        functools.partial(_attn_kernel, fox=fox),
        grid=(groups, s // TQ),
        in_specs=in_specs,
        out_specs=pl.BlockSpec((TQ, LANES), lambda g, i: (i, g)),
        out_shape=jax.ShapeDtypeStruct((s, groups * LANES), jnp.bfloat16),
        scratch_shapes=[
            pltpu.VMEM((2, TQ, LANES), jnp.float32),
            pltpu.VMEM((2, TQ, vw), jnp.float32),
            pltpu.VMEM((2, 8, LANES), jnp.float32),
            pltpu.VMEM((2, s, LANES), jnp.bfloat16),
            vaug,
        ],
        compiler_params=pltpu.CompilerParams(
            dimension_semantics=("arbitrary", "arbitrary"), vmem_limit_bytes=VMEM_LIMIT),
        name="fox_attn" if fox else "diff_attn",
    )(*args)


def _out_ffn_kernel(x_ref, fo_ref, do_ref, wo_ref, gf_ref, wg_ref, wu_ref, wd_ref,
                    gl_ref, o_ref):
    x1 = (x_ref[...]
          + jnp.dot(fo_ref[...], wo_ref[:FOX_WIDTH, :], preferred_element_type=jnp.float32)
          + jnp.dot(do_ref[...], wo_ref[FOX_WIDTH:, :], preferred_element_type=jnp.float32))
    h = _rms(x1, gf_ref[...]).astype(jnp.bfloat16)
    g = jnp.dot(h, wg_ref[...], preferred_element_type=jnp.float32)
    u = jnp.dot(h, wu_ref[...], preferred_element_type=jnp.float32)
    a = (g * jax.nn.sigmoid(g) * u).astype(jnp.bfloat16)
    x2 = x1 + jnp.dot(a, wd_ref[...], preferred_element_type=jnp.float32)
    o_ref[...] = _rms(x2, gl_ref[...])


def _out_ffn(x, fo, do, wo, gf, wg, wu, wd, gl):
    s = x.shape[0]
    const = lambda shape: pl.BlockSpec(shape, lambda i: (0, 0), pipeline_mode=pl.Buffered(1))
    return pl.pallas_call(
        _out_ffn_kernel,
        grid=(s // TM_FFN,),
        in_specs=[
            pl.BlockSpec((TM_FFN, D_MODEL), lambda i: (i, 0)),
            pl.BlockSpec((TM_FFN, FOX_WIDTH), lambda i: (i, 0)),
            pl.BlockSpec((TM_FFN, DIFF_WIDTH), lambda i: (i, 0)),
            const(wo.shape), const(gf.shape), const(wg.shape), const(wu.shape),
            const(wd.shape), const(gl.shape),
        ],
        out_specs=pl.BlockSpec((TM_FFN, D_MODEL), lambda i: (i, 0)),
        out_shape=jax.ShapeDtypeStruct((s, D_MODEL), jnp.float32),
        compiler_params=pltpu.CompilerParams(
            dimension_semantics=("arbitrary",), vmem_limit_bytes=VMEM_LIMIT),
        name="out_ffn",
    )(x, fo, do, wo, gf, wg, wu, wd, gl)


def kernel(x, mix_norm_g, w_in, b_forget, lambda_q1, lambda_k1, lambda_q2, lambda_k2,
           diff_norm_g, w_out, ffn_norm_g, w_gate, w_up, w_down, final_norm_g):
    b, s, d = x.shape
    assert b == 1 and d == D_MODEL and w_in.shape[0] == 1
    assert s % TQ == 0 and s % TM_IN == 0 and s % TM_FFN == 0
    assert TM_IN == TK and TQ == TK and s < 2 ** 14 + 1
    bf16 = jnp.bfloat16
    x2d = x.reshape(s, d)

    w = w_in[0]
    o_gate = 3 * FOX_WIDTH
    w_r = jnp.concatenate(
        [w[:, :o_gate], w[:, o_gate + FOX_HEADS:], w[:, o_gate:o_gate + FOX_HEADS],
         jnp.zeros((d, GATE_PAD - FOX_HEADS), w.dtype)], axis=1).astype(bf16)
    b_pad = jnp.pad(b_forget[0].astype(jnp.float32), (0, GATE_PAD - FOX_HEADS)).reshape(1, GATE_PAD)
    place = jnp.asarray(_fox_aug_placement(), bf16)

    proj, caug, cend = _in_proj(x2d, mix_norm_g[0].reshape(1, d), w_r, b_pad, place)

    fox_out = _attention(proj, (caug, cend), fox=True)
    lam_args = tuple(a[0].astype(jnp.float32).reshape(1, HEAD_DIM)
                     for a in (lambda_q1, lambda_k1, lambda_q2, lambda_k2))
    diff_out = _attention(
        proj, lam_args + (diff_norm_g[0].astype(jnp.float32).reshape(1, DIFF_V_DIM),), fox=False)

    out = _out_ffn(
        x2d, fox_out, diff_out, w_out[0].astype(bf16), ffn_norm_g[0].reshape(1, d),
        w_gate[0].astype(bf16), w_up[0].astype(bf16), w_down[0].astype(bf16),
        final_norm_g.reshape(1, d))
    return out.reshape(b, s, d)
```

```python
import functools
import math

import numpy as np
import jax
import jax.numpy as jnp
from jax import lax
from jax.experimental import pallas as pl
from jax.experimental.pallas import tpu as pltpu

D_MODEL = 1024
HEAD_DIM = 64
FOX_HEADS = 8
FOX_WIDTH = FOX_HEADS * HEAD_DIM
DIFF_HEADS = 4
DIFF_V_DIM = 2 * HEAD_DIM
DIFF_WIDTH = DIFF_HEADS * DIFF_V_DIM
D_FF = ((8 * D_MODEL + 767) // 768) * 256
EPS = 1e-6
NEG_INF = -1e30
LOG2E = math.log2(math.e)
Q_SCALE = LOG2E / math.sqrt(HEAD_DIM)
UNDERFLOW_LOG2 = 150.0
LAMBDA_INIT = 0.8 - 0.6 * math.exp(-0.3 * 0)

LANES = 128
GATE_PAD = LANES
QKV_WIDTH = 3 * FOX_WIDTH + 3 * DIFF_WIDTH
VMEM_LIMIT = 56 * 1024 * 1024

TM_IN = 512
TQ = 512
TK = 512
TILES_PER_STEP = 2
TM_FFN = 512

_NT = (((1,), (1,)), ((), ()))


def _rms(x, g):
    return x * lax.rsqrt(jnp.mean(x * x, axis=-1, keepdims=True) + EPS) * g


def _split3(x):
    hi = x.astype(jnp.bfloat16)
    r1 = x - hi.astype(jnp.float32)
    mid = r1.astype(jnp.bfloat16)
    lo = (r1 - mid.astype(jnp.float32)).astype(jnp.bfloat16)
    return hi, mid, lo


def _fox_aug_placement():
    p = np.zeros((3 * LANES, FOX_WIDTH), np.float32)
    for part in range(3):
        for head in range(FOX_HEADS):
            lane = HEAD_DIM + part if head % 2 == 0 else part
            p[part * LANES + head, (head // 2) * LANES + lane] = 1.0
    return p


def _in_proj_kernel(x_ref, g_ref, w_ref, b_ref, place_ref, proj_ref, caug_ref, cend_ref,
                    tri_ref, carry_ref):
    step = pl.program_id(0)
    tm = x_ref.shape[0]

    @pl.when(step == 0)
    def _():
        r = lax.broadcasted_iota(jnp.int32, (tm, tm), 0)
        c = lax.broadcasted_iota(jnp.int32, (tm, tm), 1)
        tri_ref[...] = jnp.where(r >= c, 1.0, 0.0).astype(jnp.bfloat16)
        carry_ref[...] = jnp.zeros_like(carry_ref)

    h = _rms(x_ref[...], g_ref[...]).astype(jnp.bfloat16)
    y = jnp.dot(h, w_ref[...], preferred_element_type=jnp.float32)
    bf16 = jnp.bfloat16
    dq0 = 3 * FOX_WIDTH
    proj_ref[:, :FOX_WIDTH] = (y[:, :FOX_WIDTH] * Q_SCALE).astype(bf16)
    proj_ref[:, FOX_WIDTH:dq0] = y[:, FOX_WIDTH:dq0].astype(bf16)
    proj_ref[:, dq0:dq0 + DIFF_WIDTH] = (y[:, dq0:dq0 + DIFF_WIDTH] * Q_SCALE).astype(bf16)
    proj_ref[:, dq0 + DIFF_WIDTH:] = y[:, dq0 + DIFF_WIDTH:QKV_WIDTH].astype(bf16)

    z = y[:, QKV_WIDTH:] + b_ref[...]
    logf = LOG2E * (jnp.minimum(z, 0.0) - jnp.log1p(jnp.exp(-jnp.abs(z))))
    cs = jnp.dot(tri_ref[...], jnp.concatenate(_split3(logf), axis=1),
                 preferred_element_type=jnp.float32)
    c = cs[:, :LANES] + cs[:, LANES:2 * LANES] + cs[:, 2 * LANES:] + carry_ref[0:1, :]
    last = c[tm - 1:tm, :]
    carry_ref[...] = jnp.broadcast_to(last, carry_ref.shape)
    cend_ref[pl.ds(step, 1), :] = last
    caug_ref[...] = jnp.dot(jnp.concatenate(_split3(-c), axis=1), place_ref[...],
                            preferred_element_type=jnp.float32).astype(bf16)


def _in_proj(x, g, w, b, place):
    s = x.shape[0]
    n = w.shape[1]
    n_tiles = s // TM_IN
    return pl.pallas_call(
        _in_proj_kernel,
        grid=(n_tiles,),
        in_specs=[
            pl.BlockSpec((TM_IN, D_MODEL), lambda i: (i, 0)),
            pl.BlockSpec((1, D_MODEL), lambda i: (0, 0)),
            pl.BlockSpec((D_MODEL, n), lambda i: (0, 0)),
            pl.BlockSpec((1, GATE_PAD), lambda i: (0, 0)),
            pl.BlockSpec(place.shape, lambda i: (0, 0)),
        ],
        out_specs=[
            pl.BlockSpec((TM_IN, QKV_WIDTH), lambda i: (i, 0)),
            pl.BlockSpec((TM_IN, FOX_WIDTH), lambda i: (i, 0)),
            pl.BlockSpec((n_tiles, LANES), lambda i: (0, 0)),
        ],
        out_shape=[
            jax.ShapeDtypeStruct((s, QKV_WIDTH), jnp.bfloat16),
            jax.ShapeDtypeStruct((s, FOX_WIDTH), jnp.bfloat16),
            jax.ShapeDtypeStruct((n_tiles, LANES), jnp.float32),
        ],
        scratch_shapes=[
            pltpu.VMEM((TM_IN, TM_IN), jnp.bfloat16),
            pltpu.VMEM((8, LANES), jnp.float32),
        ],
        compiler_params=pltpu.CompilerParams(
            dimension_semantics=("arbitrary",), vmem_limit_bytes=VMEM_LIMIT),
        name="in_proj",
    )(x, g, w, b, place)


def _attn_kernel(*refs, fox):
    if fox:
        (q_ref, k_ref, v_ref, caug_ref, cend_ref,
         o_ref, m_ref, acc_ref, kmax_ref, kaug_ref, vaug_ref) = refs
    else:
        (q_ref, k_ref, v_ref, lq1_ref, lk1_ref, lq2_ref, lk2_ref, gn_ref,
         o_ref, m_ref, acc_ref, kmax_ref, kaug_ref, l_ref) = refs
    grp = pl.program_id(0)
    i = pl.program_id(1)
    tq, tk = TQ, TK
    s_len = k_ref.shape[0]
    n_tiles = s_len // tk
    f32, bf16 = jnp.float32, jnp.bfloat16

    lane = lax.broadcasted_iota(jnp.int32, (1, LANES), 1)
    lo_half = lane < HEAD_DIM
    sub = lane & (HEAD_DIM - 1)

    if not fox:
        slope = LOG2E * jnp.exp2(-2.0 * jnp.full((1, 1), grp + 1, jnp.int32).astype(f32))
        sl = [p.astype(f32) for p in _split3(slope)]

    def half_norms_sq(x):
        sq = x.astype(f32)
        sq = sq * sq
        return (jnp.sum(jnp.where(lo_half, sq, 0.0), axis=1, keepdims=True),
                jnp.sum(jnp.where(lo_half, 0.0, sq), axis=1, keepdims=True))

    @pl.when(i == 0)
    def _():
        def chunk(cidx, mx):
            rows = pl.ds(pl.multiple_of(cidx * tk, tk), tk)
            k = k_ref[rows, :]
            if fox:
                aug = caug_ref[rows, :]
                v = v_ref[rows, :]
                vaug_ref[0, rows, :] = jnp.where(lo_half, v, (lane == HEAD_DIM).astype(bf16))
                vaug_ref[1, rows, :] = jnp.where(lo_half, (lane == 0).astype(bf16), v)
            else:
                pos = cidx * tk + lax.broadcasted_iota(jnp.int32, (tk, LANES), 0)
                aug = jnp.where(sub < 3, pos & ~(LANES - 1),
                                jnp.where(sub < 6, pos & (LANES - 1), 0)).astype(f32).astype(bf16)
            kaug_ref[0, rows, :] = jnp.where(lo_half, k, aug)
            kaug_ref[1, rows, :] = jnp.where(lo_half, aug, k)
            n = half_norms_sq(k)
            return (jnp.maximum(mx[0], n[0]), jnp.maximum(mx[1], n[1]))
        zero = jnp.zeros((tk, 1), f32)
        mx = lax.fori_loop(0, n_tiles, chunk, (zero, zero))
        for h in range(2):
            kmax_ref[h] = jnp.broadcast_to(
                jnp.sqrt(jnp.max(mx[h], axis=0, keepdims=True)), kmax_ref.shape[1:])

    q = q_ref[...]
    if fox:
        q_aug_lanes = (sub < 3).astype(bf16)
    else:
        q_aug_lanes = jnp.where((sub == 0) | (sub == 3), sl[0],
                                jnp.where((sub == 1) | (sub == 4), sl[1],
                                          jnp.where((sub == 2) | (sub == 5), sl[2], 0.0))
                                ).astype(bf16)
    q_aug = (jnp.where(lo_half, q, q_aug_lanes), jnp.where(lo_half, q_aug_lanes, q))

    m_ref[...] = jnp.full_like(m_ref, NEG_INF)
    acc_ref[...] = jnp.zeros_like(acc_ref)
    if not fox:
        l_ref[...] = jnp.zeros_like(l_ref)

    def tile(j, n_sub, masked):
        width = n_sub * tk
        rows = pl.ds(pl.multiple_of(j * tk, tk), width)
        if masked:
            r = lax.broadcasted_iota(jnp.int32, (tq, width), 0)
            c = lax.broadcasted_iota(jnp.int32, (tq, width), 1)
            causal = c <= r
        for h in range(2):
            s = lax.dot_general(q_aug[h], kaug_ref[h, rows, :], _NT, preferred_element_type=f32)
            if masked:
                s = jnp.where(causal, s, NEG_INF)
            m_prev = m_ref[h]
            m_next = jnp.maximum(m_prev, jnp.max(s, axis=1, keepdims=True))
            alpha = jnp.exp2(m_prev - m_next)
            p = jnp.exp2(s - jnp.tile(m_next, (1, width // LANES)))
            if fox:
                v = vaug_ref[h, rows, :]
            else:
                v = v_ref[rows, :]
                part = p[:, :LANES]
                for blk in range(1, width // LANES):
                    part = part + p[:, blk * LANES:(blk + 1) * LANES]
                l_ref[h] = alpha * l_ref[h] + part
            acc_ref[h] = (acc_ref[h] * alpha
                          + jnp.dot(p.astype(bf16), v, preferred_element_type=f32))
            m_ref[h] = m_next

    tile(i, 1, masked=True)

    qn = half_norms_sq(q)
    gap = [jnp.max(jnp.sqrt(qn[h]) * kmax_ref[h][0:1, 0:1] - m_ref[h][:, 0:1],
                   axis=0, keepdims=True) for h in range(2)]
    if fox:
        tile_idx = lax.broadcasted_iota(jnp.int32, cend_ref.shape, 0)
        head_lane = lax.broadcasted_iota(jnp.int32, cend_ref.shape, 1)
        cnt = []
        for h in range(2):
            dead = ((-cend_ref[...] < -UNDERFLOW_LOG2 - gap[h]) & (tile_idx < i)
                    & (head_lane == 2 * grp + h))
            cnt.append(jnp.sum(dead.astype(jnp.int32)))
        j_lo = jnp.minimum(cnt[0], cnt[1])
    else:
        x = (-UNDERFLOW_LOG2 - jnp.maximum(gap[0], gap[1])) / slope
        j_lo = jnp.clip(jnp.floor(x * (1.0 / tk)), 0.0, 1.0 * n_tiles)
        j_lo = jnp.minimum(j_lo.astype(jnp.int32)[0, 0], i)

    n_live = i - j_lo
    odd = n_live & 1

    @pl.when(odd == 1)
    def _():
        tile(i - 1, 1, masked=False)

    def body(step, carry):
        tile(i - odd - TILES_PER_STEP * (step + 1), TILES_PER_STEP, masked=False)
        return carry

    lax.fori_loop(0, n_live // TILES_PER_STEP, body, 0)

    if fox:
        out = jnp.where(lo_half, acc_ref[0] / acc_ref[0][:, HEAD_DIM:HEAD_DIM + 1],
                        acc_ref[1] / acc_ref[1][:, 0:1])
    else:
        lam = (jnp.exp(jnp.sum(lq1_ref[...] * lk1_ref[...], axis=1, keepdims=True))
               - jnp.exp(jnp.sum(lq2_ref[...] * lk2_ref[...], axis=1, keepdims=True))
               + LAMBDA_INIT)
        a = (acc_ref[0] / jnp.sum(l_ref[0], axis=1, keepdims=True)
             - lam * (acc_ref[1] / jnp.sum(l_ref[1], axis=1, keepdims=True)))
        out = _rms(a, gn_ref[...]) * (1.0 - LAMBDA_INIT)
    o_ref[...] = out.astype(o_ref.dtype)


def _attention(proj, extra, *, fox):
    s = proj.shape[0]
    groups = FOX_HEADS // 2 if fox else DIFF_HEADS
    base = 0 if fox else 3 * (FOX_WIDTH // LANES)
    per = FOX_WIDTH // LANES if fox else DIFF_WIDTH // LANES
    resident = lambda col0: pl.BlockSpec((s, LANES), lambda g, i: (0, col0 + g),
                                         pipeline_mode=pl.Buffered(1))
    in_specs = [
        pl.BlockSpec((TQ, LANES), lambda g, i: (i, base + g)),
        resident(base + per),
        resident(base + 2 * per),
    ]
    args = [proj, proj, proj]
    if fox:
        caug, cend = extra
        in_specs.append(resident(0))
        in_specs.append(pl.BlockSpec(cend.shape, lambda g, i: (0, 0)))
        args += [caug, cend]
        last_scratch = pltpu.VMEM((2, s, LANES), jnp.bfloat16)
    else:
        for a in extra:
            in_specs.append(pl.BlockSpec(a.shape, lambda g, i: (0, 0)))
            args.append(a)
        last_scratch = pltpu.VMEM((2, TQ, LANES), jnp.float32)
    return pl.pallas_call(
        functools.partial(_attn_kernel, fox=fox),
        grid=(groups, s // TQ),
        in_specs=in_specs,
        out_specs=pl.BlockSpec((TQ, LANES), lambda g, i: (i, g)),
        out_shape=jax.ShapeDtypeStruct((s, groups * LANES), jnp.bfloat16),
        scratch_shapes=[
            pltpu.VMEM((2, TQ, LANES), jnp.float32),
            pltpu.VMEM((2, TQ, LANES), jnp.float32),
            pltpu.VMEM((2, 8, LANES), jnp.float32),
            pltpu.VMEM((2, s, LANES), jnp.bfloat16),
            last_scratch,
        ],
        compiler_params=pltpu.CompilerParams(
            dimension_semantics=("arbitrary", "arbitrary"), vmem_limit_bytes=VMEM_LIMIT),
        name="fox_attn" if fox else "diff_attn",
    )(*args)


def _out_ffn_kernel(x_ref, fo_ref, do_ref, wo_ref, gf_ref, wg_ref, wu_ref, wd_ref,
                    gl_ref, o_ref):
    x1 = (x_ref[...]
          + jnp.dot(fo_ref[...], wo_ref[:FOX_WIDTH, :], preferred_element_type=jnp.float32)
          + jnp.dot(do_ref[...], wo_ref[FOX_WIDTH:, :], preferred_element_type=jnp.float32))
    h = _rms(x1, gf_ref[...]).astype(jnp.bfloat16)
    g = jnp.dot(h, wg_ref[...], preferred_element_type=jnp.float32)
    u = jnp.dot(h, wu_ref[...], preferred_element_type=jnp.float32)
    a = (g * jax.nn.sigmoid(g) * u).astype(jnp.bfloat16)
    x2 = x1 + jnp.dot(a, wd_ref[...], preferred_element_type=jnp.float32)
    o_ref[...] = _rms(x2, gl_ref[...])


def _out_ffn(x, fo, do, wo, gf, wg, wu, wd, gl):
    s = x.shape[0]
    const = lambda shape: pl.BlockSpec(shape, lambda i: (0, 0), pipeline_mode=pl.Buffered(1))
    return pl.pallas_call(
        _out_ffn_kernel,
        grid=(s // TM_FFN,),
        in_specs=[
            pl.BlockSpec((TM_FFN, D_MODEL), lambda i: (i, 0)),
            pl.BlockSpec((TM_FFN, FOX_WIDTH), lambda i: (i, 0)),
            pl.BlockSpec((TM_FFN, DIFF_WIDTH), lambda i: (i, 0)),
            const(wo.shape), const(gf.shape), const(wg.shape), const(wu.shape),
            const(wd.shape), const(gl.shape),
        ],
        out_specs=pl.BlockSpec((TM_FFN, D_MODEL), lambda i: (i, 0)),
        out_shape=jax.ShapeDtypeStruct((s, D_MODEL), jnp.float32),
        compiler_params=pltpu.CompilerParams(
            dimension_semantics=("arbitrary",), vmem_limit_bytes=VMEM_LIMIT),
        name="out_ffn",
    )(x, fo, do, wo, gf, wg, wu, wd, gl)


def kernel(x, mix_norm_g, w_in, b_forget, lambda_q1, lambda_k1, lambda_q2, lambda_k2,
           diff_norm_g, w_out, ffn_norm_g, w_gate, w_up, w_down, final_norm_g):
    b, s, d = x.shape
    assert b == 1 and d == D_MODEL and w_in.shape[0] == 1
    assert s % TQ == 0 and s % TM_IN == 0 and s % TM_FFN == 0
    assert TM_IN == TK and TQ == TK and TILES_PER_STEP == 2
    assert s < 2 ** 14 + 1
    bf16 = jnp.bfloat16
    x2d = x.reshape(s, d)

    w = w_in[0]
    o_gate = 3 * FOX_WIDTH
    w_r = jnp.concatenate(
        [w[:, :o_gate], w[:, o_gate + FOX_HEADS:], w[:, o_gate:o_gate + FOX_HEADS],
         jnp.zeros((d, GATE_PAD - FOX_HEADS), w.dtype)], axis=1).astype(bf16)
    b_pad = jnp.pad(b_forget[0].astype(jnp.float32), (0, GATE_PAD - FOX_HEADS)).reshape(1, GATE_PAD)
    place = jnp.asarray(_fox_aug_placement(), bf16)

    proj, caug, cend = _in_proj(x2d, mix_norm_g[0].reshape(1, d), w_r, b_pad, place)

    fox_out = _attention(proj, (caug, cend), fox=True)
    lam_args = tuple(a[0].astype(jnp.float32).reshape(1, HEAD_DIM)
                     for a in (lambda_q1, lambda_k1, lambda_q2, lambda_k2))
    diff_out = _attention(
        proj, lam_args + (diff_norm_g[0].astype(jnp.float32).reshape(1, DIFF_V_DIM),), fox=False)

    out = _out_ffn(
        x2d, fox_out, diff_out, w_out[0].astype(bf16), ffn_norm_g[0].reshape(1, d),
        w_gate[0].astype(bf16), w_up[0].astype(bf16), w_down[0].astype(bf16),
        final_norm_g.reshape(1, d))
    return out.reshape(b, s, d)
```

```python
import functools
import math

import numpy as np
import jax
import jax.numpy as jnp
from jax import lax
from jax.experimental import pallas as pl
from jax.experimental.pallas import tpu as pltpu

D_MODEL = 1024
HEAD_DIM = 64
FOX_HEADS = 8
FOX_WIDTH = FOX_HEADS * HEAD_DIM
DIFF_HEADS = 4
DIFF_V_DIM = 2 * HEAD_DIM
DIFF_WIDTH = DIFF_HEADS * DIFF_V_DIM
D_FF = ((8 * D_MODEL + 767) // 768) * 256
EPS = 1e-6
NEG_INF = -1e30
LOG2E = math.log2(math.e)
Q_SCALE = LOG2E / math.sqrt(HEAD_DIM)
UNDERFLOW_LOG2 = 150.0
LAMBDA_INIT = 0.8 - 0.6 * math.exp(-0.3 * 0)

LANES = 128
GATE_PAD = LANES
QKV_WIDTH = 3 * FOX_WIDTH + 3 * DIFF_WIDTH
VMEM_LIMIT = 56 * 1024 * 1024

TM_IN = 512
TQ = 512
TK = 512
TILES_PER_STEP = 4
TM_FFN = 512

_NT = (((1,), (1,)), ((), ()))


def _rms(x, g):
    return x * lax.rsqrt(jnp.mean(x * x, axis=-1, keepdims=True) + EPS) * g


def _split3(x):
    hi = x.astype(jnp.bfloat16)
    r1 = x - hi.astype(jnp.float32)
    mid = r1.astype(jnp.bfloat16)
    lo = (r1 - mid.astype(jnp.float32)).astype(jnp.bfloat16)
    return hi, mid, lo


def _fox_aug_placement():
    p = np.zeros((3 * LANES, FOX_WIDTH), np.float32)
    for part in range(3):
        for head in range(FOX_HEADS):
            lane = HEAD_DIM + part if head % 2 == 0 else part
            p[part * LANES + head, (head // 2) * LANES + lane] = 1.0
    return p


def _in_proj_kernel(x_ref, g_ref, w_ref, b_ref, place_ref, proj_ref, caug_ref, cend_ref,
                    tri_ref, carry_ref):
    step = pl.program_id(0)
    tm = x_ref.shape[0]

    @pl.when(step == 0)
    def _():
        r = lax.broadcasted_iota(jnp.int32, (tm, tm), 0)
        c = lax.broadcasted_iota(jnp.int32, (tm, tm), 1)
        tri_ref[...] = jnp.where(r >= c, 1.0, 0.0).astype(jnp.bfloat16)
        carry_ref[...] = jnp.zeros_like(carry_ref)

    h = _rms(x_ref[...], g_ref[...]).astype(jnp.bfloat16)
    y = jnp.dot(h, w_ref[...], preferred_element_type=jnp.float32)
    bf16 = jnp.bfloat16
    dq0 = 3 * FOX_WIDTH
    proj_ref[:, :FOX_WIDTH] = (y[:, :FOX_WIDTH] * Q_SCALE).astype(bf16)
    proj_ref[:, FOX_WIDTH:dq0] = y[:, FOX_WIDTH:dq0].astype(bf16)
    proj_ref[:, dq0:dq0 + DIFF_WIDTH] = (y[:, dq0:dq0 + DIFF_WIDTH] * Q_SCALE).astype(bf16)
    proj_ref[:, dq0 + DIFF_WIDTH:] = y[:, dq0 + DIFF_WIDTH:QKV_WIDTH].astype(bf16)

    z = y[:, QKV_WIDTH:] + b_ref[...]
    logf = LOG2E * (jnp.minimum(z, 0.0) - jnp.log1p(jnp.exp(-jnp.abs(z))))
    cs = jnp.dot(tri_ref[...], jnp.concatenate(_split3(logf), axis=1),
                 preferred_element_type=jnp.float32)
    c = cs[:, :LANES] + cs[:, LANES:2 * LANES] + cs[:, 2 * LANES:] + carry_ref[0:1, :]
    last = c[tm - 1:tm, :]
    carry_ref[...] = jnp.broadcast_to(last, carry_ref.shape)
    cend_ref[pl.ds(step, 1), :] = last
    caug_ref[...] = jnp.dot(jnp.concatenate(_split3(-c), axis=1), place_ref[...],
                            preferred_element_type=jnp.float32).astype(bf16)


def _in_proj(x, g, w, b, place):
    s = x.shape[0]
    n = w.shape[1]
    n_tiles = s // TM_IN
    return pl.pallas_call(
        _in_proj_kernel,
        grid=(n_tiles,),
        in_specs=[
            pl.BlockSpec((TM_IN, D_MODEL), lambda i: (i, 0)),
            pl.BlockSpec((1, D_MODEL), lambda i: (0, 0)),
            pl.BlockSpec((D_MODEL, n), lambda i: (0, 0)),
            pl.BlockSpec((1, GATE_PAD), lambda i: (0, 0)),
            pl.BlockSpec(place.shape, lambda i: (0, 0)),
        ],
        out_specs=[
            pl.BlockSpec((TM_IN, QKV_WIDTH), lambda i: (i, 0)),
            pl.BlockSpec((TM_IN, FOX_WIDTH), lambda i: (i, 0)),
            pl.BlockSpec((n_tiles, LANES), lambda i: (0, 0)),
        ],
        out_shape=[
            jax.ShapeDtypeStruct((s, QKV_WIDTH), jnp.bfloat16),
            jax.ShapeDtypeStruct((s, FOX_WIDTH), jnp.bfloat16),
            jax.ShapeDtypeStruct((n_tiles, LANES), jnp.float32),
        ],
        scratch_shapes=[
            pltpu.VMEM((TM_IN, TM_IN), jnp.bfloat16),
            pltpu.VMEM((8, LANES), jnp.float32),
        ],
        compiler_params=pltpu.CompilerParams(
            dimension_semantics=("arbitrary",), vmem_limit_bytes=VMEM_LIMIT),
        name="in_proj",
    )(x, g, w, b, place)


def _attn_kernel(*refs, fox):
    if fox:
        (q_ref, k_ref, v_ref, caug_ref, cend_ref,
         o_ref, m_ref, acc_ref, kmax_ref, kaug_ref, vaug_ref) = refs
    else:
        (q_ref, k_ref, v_ref, lq1_ref, lk1_ref, lq2_ref, lk2_ref, gn_ref,
         o_ref, m_ref, acc_ref, kmax_ref, kaug_ref, l_ref) = refs
    grp = pl.program_id(0)
    i = pl.program_id(1)
    tq, tk = TQ, TK
    s_len = k_ref.shape[0]
    n_tiles = s_len // tk
    f32, bf16 = jnp.float32, jnp.bfloat16

    lane = lax.broadcasted_iota(jnp.int32, (1, LANES), 1)
    lo_half = lane < HEAD_DIM
    sub = lane & (HEAD_DIM - 1)

    if not fox:
        slope = LOG2E * jnp.exp2(-2.0 * jnp.full((1, 1), grp + 1, jnp.int32).astype(f32))
        sl = [p.astype(f32) for p in _split3(slope)]

    def half_norms_sq(x):
        sq = x.astype(f32)
        sq = sq * sq
        return (jnp.sum(jnp.where(lo_half, sq, 0.0), axis=1, keepdims=True),
                jnp.sum(jnp.where(lo_half, 0.0, sq), axis=1, keepdims=True))

    @pl.when(i == 0)
    def _():
        def chunk(cidx, mx):
            rows = pl.ds(pl.multiple_of(cidx * tk, tk), tk)
            k = k_ref[rows, :]
            if fox:
                aug = caug_ref[rows, :]
                v = v_ref[rows, :]
                vaug_ref[0, rows, :] = jnp.where(lo_half, v, (lane == HEAD_DIM).astype(bf16))
                vaug_ref[1, rows, :] = jnp.where(lo_half, (lane == 0).astype(bf16), v)
            else:
                pos = cidx * tk + lax.broadcasted_iota(jnp.int32, (tk, LANES), 0)
                aug = jnp.where(sub < 3, pos & ~(LANES - 1),
                                jnp.where(sub < 6, pos & (LANES - 1), 0)).astype(f32).astype(bf16)
            kaug_ref[0, rows, :] = jnp.where(lo_half, k, aug)
            kaug_ref[1, rows, :] = jnp.where(lo_half, aug, k)
            n = half_norms_sq(k)
            return (jnp.maximum(mx[0], n[0]), jnp.maximum(mx[1], n[1]))
        zero = jnp.zeros((tk, 1), f32)
        mx = lax.fori_loop(0, n_tiles, chunk, (zero, zero))
        for h in range(2):
            kmax_ref[h] = jnp.broadcast_to(
                jnp.sqrt(jnp.max(mx[h], axis=0, keepdims=True)), kmax_ref.shape[1:])

    q = q_ref[...]
    if fox:
        q_aug_lanes = (sub < 3).astype(bf16)
    else:
        q_aug_lanes = jnp.where((sub == 0) | (sub == 3), sl[0],
                                jnp.where((sub == 1) | (sub == 4), sl[1],
                                          jnp.where((sub == 2) | (sub == 5), sl[2], 0.0))
                                ).astype(bf16)
    q_aug = (jnp.where(lo_half, q, q_aug_lanes), jnp.where(lo_half, q_aug_lanes, q))

    m_ref[...] = jnp.full_like(m_ref, NEG_INF)
    acc_ref[...] = jnp.zeros_like(acc_ref)
    if not fox:
        l_ref[...] = jnp.zeros_like(l_ref)

    def tile(j, n_sub, masked):
        width = n_sub * tk
        rows = pl.ds(pl.multiple_of(j * tk, tk), width)
        if masked:
            r = lax.broadcasted_iota(jnp.int32, (tq, width), 0)
            c = lax.broadcasted_iota(jnp.int32, (tq, width), 1)
            causal = c <= r
        for h in range(2):
            s = lax.dot_general(q_aug[h], kaug_ref[h, rows, :], _NT, preferred_element_type=f32)
            if masked:
                s = jnp.where(causal, s, NEG_INF)
            m_prev = m_ref[h]
            m_next = jnp.maximum(m_prev, jnp.max(s, axis=1, keepdims=True))
            alpha = jnp.exp2(m_prev - m_next)
            p = jnp.exp2(s - jnp.tile(m_next, (1, width // LANES)))
            if fox:
                v = vaug_ref[h, rows, :]
            else:
                v = v_ref[rows, :]
                part = p[:, :LANES]
                for blk in range(1, width // LANES):
                    part = part + p[:, blk * LANES:(blk + 1) * LANES]
                l_ref[h] = alpha * l_ref[h] + part
            acc_ref[h] = (acc_ref[h] * alpha
                          + jnp.dot(p.astype(bf16), v, preferred_element_type=f32))
            m_ref[h] = m_next

    tile(i, 1, masked=True)

    qn = half_norms_sq(q)
    gap = [jnp.max(jnp.sqrt(qn[h]) * kmax_ref[h][0:1, 0:1] - m_ref[h][:, 0:1],
                   axis=0, keepdims=True) for h in range(2)]
    if fox:
        tile_idx = lax.broadcasted_iota(jnp.int32, cend_ref.shape, 0)
        head_lane = lax.broadcasted_iota(jnp.int32, cend_ref.shape, 1)
        cnt = []
        for h in range(2):
            dead = ((-cend_ref[...] < -UNDERFLOW_LOG2 - gap[h]) & (tile_idx < i)
                    & (head_lane == 2 * grp + h))
            cnt.append(jnp.sum(dead.astype(jnp.int32)))
        j_lo = jnp.minimum(cnt[0], cnt[1])
    else:
        x = (-UNDERFLOW_LOG2 - jnp.maximum(gap[0], gap[1])) / slope
        j_lo = jnp.clip(jnp.floor(x * (1.0 / tk)), 0.0, 1.0 * n_tiles)
        j_lo = jnp.minimum(j_lo.astype(jnp.int32)[0, 0], i)

    n_live = i - j_lo
    one = n_live & 1
    two = n_live & 2

    @pl.when(one == 1)
    def _():
        tile(i - 1, 1, masked=False)

    @pl.when(two == 2)
    def _():
        tile(i - one - 2, 2, masked=False)

    def body(step, carry):
        tile(i - one - two - TILES_PER_STEP * (step + 1), TILES_PER_STEP, masked=False)
        return carry

    lax.fori_loop(0, n_live // TILES_PER_STEP, body, 0)

    if fox:
        out = jnp.where(lo_half, acc_ref[0] / acc_ref[0][:, HEAD_DIM:HEAD_DIM + 1],
                        acc_ref[1] / acc_ref[1][:, 0:1])
    else:
        lam = (jnp.exp(jnp.sum(lq1_ref[...] * lk1_ref[...], axis=1, keepdims=True))
               - jnp.exp(jnp.sum(lq2_ref[...] * lk2_ref[...], axis=1, keepdims=True))
               + LAMBDA_INIT)
        a = (acc_ref[0] / jnp.sum(l_ref[0], axis=1, keepdims=True)
             - lam * (acc_ref[1] / jnp.sum(l_ref[1], axis=1, keepdims=True)))
        out = _rms(a, gn_ref[...]) * (1.0 - LAMBDA_INIT)
    o_ref[...] = out.astype(o_ref.dtype)


def _attention(proj, extra, *, fox):
    s = proj.shape[0]
    groups = FOX_HEADS // 2 if fox else DIFF_HEADS
    base = 0 if fox else 3 * (FOX_WIDTH // LANES)
    per = FOX_WIDTH // LANES if fox else DIFF_WIDTH // LANES
    resident = lambda col0: pl.BlockSpec((s, LANES), lambda g, i: (0, col0 + g),
                                         pipeline_mode=pl.Buffered(1))
    in_specs = [
        pl.BlockSpec((TQ, LANES), lambda g, i: (i, base + g)),
        resident(base + per),
        resident(base + 2 * per),
    ]
    args = [proj, proj, proj]
    if fox:
        caug, cend = extra
        in_specs.append(resident(0))
        in_specs.append(pl.BlockSpec(cend.shape, lambda g, i: (0, 0)))
        args += [caug, cend]
        last_scratch = pltpu.VMEM((2, s, LANES), jnp.bfloat16)
    else:
        for a in extra:
            in_specs.append(pl.BlockSpec(a.shape, lambda g, i: (0, 0)))
            args.append(a)
        last_scratch = pltpu.VMEM((2, TQ, LANES), jnp.float32)
    return pl.pallas_call(
        functools.partial(_attn_kernel, fox=fox),
        grid=(groups, s // TQ),
        in_specs=in_specs,
        out_specs=pl.BlockSpec((TQ, LANES), lambda g, i: (i, g)),
        out_shape=jax.ShapeDtypeStruct((s, groups * LANES), jnp.bfloat16),
        scratch_shapes=[
            pltpu.VMEM((2, TQ, LANES), jnp.float32),
            pltpu.VMEM((2, TQ, LANES), jnp.float32),
            pltpu.VMEM((2, 8, LANES), jnp.float32),
            pltpu.VMEM((2, s, LANES), jnp.bfloat16),
            last_scratch,
        ],
        compiler_params=pltpu.CompilerParams(
            dimension_semantics=("arbitrary", "arbitrary"), vmem_limit_bytes=VMEM_LIMIT),
        name="fox_attn" if fox else "diff_attn",
    )(*args)


def _out_ffn_kernel(x_ref, fo_ref, do_ref, wo_ref, gf_ref, wg_ref, wu_ref, wd_ref,
                    gl_ref, o_ref):
    x1 = (x_ref[...]
          + jnp.dot(fo_ref[...], wo_ref[:FOX_WIDTH, :], preferred_element_type=jnp.float32)
          + jnp.dot(do_ref[...], wo_ref[FOX_WIDTH:, :], preferred_element_type=jnp.float32))
    h = _rms(x1, gf_ref[...]).astype(jnp.bfloat16)
    g = jnp.dot(h, wg_ref[...], preferred_element_type=jnp.float32)
    u = jnp.dot(h, wu_ref[...], preferred_element_type=jnp.float32)
    a = (g * jax.nn.sigmoid(g) * u).astype(jnp.bfloat16)
    x2 = x1 + jnp.dot(a, wd_ref[...], preferred_element_type=jnp.float32)
    o_ref[...] = _rms(x2, gl_ref[...])


def _out_ffn(x, fo, do, wo, gf, wg, wu, wd, gl):
    s = x.shape[0]
    const = lambda shape: pl.BlockSpec(shape, lambda i: (0, 0), pipeline_mode=pl.Buffered(1))
    return pl.pallas_call(
        _out_ffn_kernel,
        grid=(s // TM_FFN,),
        in_specs=[
            pl.BlockSpec((TM_FFN, D_MODEL), lambda i: (i, 0)),
            pl.BlockSpec((TM_FFN, FOX_WIDTH), lambda i: (i, 0)),
            pl.BlockSpec((TM_FFN, DIFF_WIDTH), lambda i: (i, 0)),
            const(wo.shape), const(gf.shape), const(wg.shape), const(wu.shape),
            const(wd.shape), const(gl.shape),
        ],
        out_specs=pl.BlockSpec((TM_FFN, D_MODEL), lambda i: (i, 0)),
        out_shape=jax.ShapeDtypeStruct((s, D_MODEL), jnp.float32),
        compiler_params=pltpu.CompilerParams(
            dimension_semantics=("arbitrary",), vmem_limit_bytes=VMEM_LIMIT),
        name="out_ffn",
    )(x, fo, do, wo, gf, wg, wu, wd, gl)


def kernel(x, mix_norm_g, w_in, b_forget, lambda_q1, lambda_k1, lambda_q2, lambda_k2,
           diff_norm_g, w_out, ffn_norm_g, w_gate, w_up, w_down, final_norm_g):
    b, s, d = x.shape
    assert b == 1 and d == D_MODEL and w_in.shape[0] == 1
    assert s % TQ == 0 and s % TM_IN == 0 and s % TM_FFN == 0
    assert TM_IN == TK and TQ == TK and TILES_PER_STEP == 4
    assert s < 2 ** 14 + 1
    bf16 = jnp.bfloat16
    x2d = x.reshape(s, d)

    w = w_in[0]
    o_gate = 3 * FOX_WIDTH
    w_r = jnp.concatenate(
        [w[:, :o_gate], w[:, o_gate + FOX_HEADS:], w[:, o_gate:o_gate + FOX_HEADS],
         jnp.zeros((d, GATE_PAD - FOX_HEADS), w.dtype)], axis=1).astype(bf16)
    b_pad = jnp.pad(b_forget[0].astype(jnp.float32), (0, GATE_PAD - FOX_HEADS)).reshape(1, GATE_PAD)
    place = jnp.asarray(_fox_aug_placement(), bf16)

    proj, caug, cend = _in_proj(x2d, mix_norm_g[0].reshape(1, d), w_r, b_pad, place)

    fox_out = _attention(proj, (caug, cend), fox=True)
    lam_args = tuple(a[0].astype(jnp.float32).reshape(1, HEAD_DIM)
                     for a in (lambda_q1, lambda_k1, lambda_q2, lambda_k2))
    diff_out = _attention(
        proj, lam_args + (diff_norm_g[0].astype(jnp.float32).reshape(1, DIFF_V_DIM),), fox=False)

    out = _out_ffn(
        x2d, fox_out, diff_out, w_out[0].astype(bf16), ffn_norm_g[0].reshape(1, d),
        w_gate[0].astype(bf16), w_up[0].astype(bf16), w_down[0].astype(bf16),
        final_norm_g.reshape(1, d))
    return out.reshape(b, s, d)
```

```python
import functools
import math

import numpy as np
import jax
import jax.numpy as jnp
from jax import lax
from jax.experimental import pallas as pl
from jax.experimental.pallas import tpu as pltpu

D_MODEL = 1024
HEAD_DIM = 64
FOX_HEADS = 8
FOX_WIDTH = FOX_HEADS * HEAD_DIM
DIFF_HEADS = 4
DIFF_V_DIM = 2 * HEAD_DIM
DIFF_WIDTH = DIFF_HEADS * DIFF_V_DIM
D_FF = ((8 * D_MODEL + 767) // 768) * 256
EPS = 1e-6
NEG_INF = -1e30
LOG2E = math.log2(math.e)
Q_SCALE = LOG2E / math.sqrt(HEAD_DIM)
UNDERFLOW_LOG2 = 150.0
NORM_SLACK = 1.01
LAMBDA_INIT = 0.8 - 0.6 * math.exp(-0.3 * 0)

LANES = 128
GATE_PAD = LANES
QKV_WIDTH = 3 * FOX_WIDTH + 3 * DIFF_WIDTH
VMEM_LIMIT = 56 * 1024 * 1024

TM_IN = 512
TQ = 512
TK = 512
TILES_PER_STEP = 4
TM_FFN = 512

_NT = (((1,), (1,)), ((), ()))


def _rms(x, g):
    return x * lax.rsqrt(jnp.mean(x * x, axis=-1, keepdims=True) + EPS) * g


def _split3(x):
    hi = x.astype(jnp.bfloat16)
    r1 = x - hi.astype(jnp.float32)
    mid = r1.astype(jnp.bfloat16)
    lo = (r1 - mid.astype(jnp.float32)).astype(jnp.bfloat16)
    return hi, mid, lo


def _fox_aug_placement():
    p = np.zeros((3 * LANES, FOX_WIDTH), np.float32)
    for part in range(3):
        for head in range(FOX_HEADS):
            lane = HEAD_DIM + part if head % 2 == 0 else part
            p[part * LANES + head, (head // 2) * LANES + lane] = 1.0
    return p


def _in_proj_kernel(x_ref, g_ref, w_ref, b_ref, place_ref, proj_ref, caug_ref, cend_ref,
                    tri_ref, carry_ref):
    step = pl.program_id(0)
    tm = x_ref.shape[0]

    @pl.when(step == 0)
    def _():
        r = lax.broadcasted_iota(jnp.int32, (tm, tm), 0)
        c = lax.broadcasted_iota(jnp.int32, (tm, tm), 1)
        tri_ref[...] = jnp.where(r >= c, 1.0, 0.0).astype(jnp.bfloat16)
        carry_ref[...] = jnp.zeros_like(carry_ref)

    h = _rms(x_ref[...], g_ref[...]).astype(jnp.bfloat16)
    y = jnp.dot(h, w_ref[...], preferred_element_type=jnp.float32)
    bf16 = jnp.bfloat16
    dq0 = 3 * FOX_WIDTH
    proj_ref[:, :FOX_WIDTH] = (y[:, :FOX_WIDTH] * Q_SCALE).astype(bf16)
    proj_ref[:, FOX_WIDTH:dq0] = y[:, FOX_WIDTH:dq0].astype(bf16)
    proj_ref[:, dq0:dq0 + DIFF_WIDTH] = (y[:, dq0:dq0 + DIFF_WIDTH] * Q_SCALE).astype(bf16)
    proj_ref[:, dq0 + DIFF_WIDTH:] = y[:, dq0 + DIFF_WIDTH:QKV_WIDTH].astype(bf16)

    z = y[:, QKV_WIDTH:] + b_ref[...]
    logf = LOG2E * (jnp.minimum(z, 0.0) - jnp.log1p(jnp.exp(-jnp.abs(z))))
    cs = jnp.dot(tri_ref[...], jnp.concatenate(_split3(logf), axis=1),
                 preferred_element_type=jnp.float32)
    c = cs[:, :LANES] + cs[:, LANES:2 * LANES] + cs[:, 2 * LANES:] + carry_ref[0:1, :]
    last = c[tm - 1:tm, :]
    carry_ref[...] = jnp.broadcast_to(last, carry_ref.shape)
    cend_ref[pl.ds(step, 1), :] = last
    caug_ref[...] = jnp.dot(jnp.concatenate(_split3(-c), axis=1), place_ref[...],
                            preferred_element_type=jnp.float32).astype(bf16)


def _in_proj(x, g, w, b, place):
    s = x.shape[0]
    n = w.shape[1]
    n_tiles = s // TM_IN
    return pl.pallas_call(
        _in_proj_kernel,
        grid=(n_tiles,),
        in_specs=[
            pl.BlockSpec((TM_IN, D_MODEL), lambda i: (i, 0)),
            pl.BlockSpec((1, D_MODEL), lambda i: (0, 0)),
            pl.BlockSpec((D_MODEL, n), lambda i: (0, 0)),
            pl.BlockSpec((1, GATE_PAD), lambda i: (0, 0)),
            pl.BlockSpec(place.shape, lambda i: (0, 0)),
        ],
        out_specs=[
            pl.BlockSpec((TM_IN, QKV_WIDTH), lambda i: (i, 0)),
            pl.BlockSpec((TM_IN, FOX_WIDTH), lambda i: (i, 0)),
            pl.BlockSpec((n_tiles, LANES), lambda i: (0, 0)),
        ],
        out_shape=[
            jax.ShapeDtypeStruct((s, QKV_WIDTH), jnp.bfloat16),
            jax.ShapeDtypeStruct((s, FOX_WIDTH), jnp.bfloat16),
            jax.ShapeDtypeStruct((n_tiles, LANES), jnp.float32),
        ],
        scratch_shapes=[
            pltpu.VMEM((TM_IN, TM_IN), jnp.bfloat16),
            pltpu.VMEM((8, LANES), jnp.float32),
        ],
        compiler_params=pltpu.CompilerParams(
            dimension_semantics=("arbitrary",), vmem_limit_bytes=VMEM_LIMIT),
        name="in_proj",
    )(x, g, w, b, place)


def _attn_kernel(*refs, fox):
    if fox:
        (q_ref, k_ref, v_ref, caug_ref, cend_ref,
         o_ref, m_ref, acc_ref, kmax_ref, kaug_ref, vaug_ref) = refs
    else:
        (q_ref, k_ref, v_ref, lq1_ref, lk1_ref, lq2_ref, lk2_ref, gn_ref,
         o_ref, m_ref, acc_ref, kmax_ref, kaug_ref, l_ref) = refs
    grp = pl.program_id(0)
    i = pl.program_id(1)
    tq, tk = TQ, TK
    s_len = k_ref.shape[0]
    n_tiles = s_len // tk
    f32, bf16 = jnp.float32, jnp.bfloat16

    lane = lax.broadcasted_iota(jnp.int32, (1, LANES), 1)
    lo_half = lane < HEAD_DIM
    sub = lane & (HEAD_DIM - 1)

    if not fox:
        slope = LOG2E * jnp.exp2(-2.0 * jnp.full((1, 1), grp + 1, jnp.int32).astype(f32))
        sl = [p.astype(f32) for p in _split3(slope)]

    def half_norms_sq(x):
        l_in = lax.broadcasted_iota(jnp.int32, (LANES, LANES), 0)
        l_out = lax.broadcasted_iota(jnp.int32, (LANES, LANES), 1)
        half_sum = ((l_in // HEAD_DIM) == l_out).astype(bf16)
        return jnp.dot(x * x, half_sum, preferred_element_type=f32)

    @pl.when(i == 0)
    def _():
        def chunk(cidx, mx):
            rows = pl.ds(pl.multiple_of(cidx * tk, tk), tk)
            k = k_ref[rows, :]
            if fox:
                aug = caug_ref[rows, :]
                v = v_ref[rows, :]
                vaug_ref[0, rows, :] = jnp.where(lo_half, v, (lane == HEAD_DIM).astype(bf16))
                vaug_ref[1, rows, :] = jnp.where(lo_half, (lane == 0).astype(bf16), v)
            else:
                pos = cidx * tk + lax.broadcasted_iota(jnp.int32, (tk, LANES), 0)
                aug = jnp.where(sub < 3, pos & ~(LANES - 1),
                                jnp.where(sub < 6, pos & (LANES - 1), 0)).astype(f32).astype(bf16)
            kaug_ref[0, rows, :] = jnp.where(lo_half, k, aug)
            kaug_ref[1, rows, :] = jnp.where(lo_half, aug, k)
            return jnp.maximum(mx, half_norms_sq(k))
        mx = lax.fori_loop(0, n_tiles, chunk, jnp.zeros((tk, LANES), f32))
        kmax = jnp.sqrt(jnp.max(mx, axis=0, keepdims=True)) * NORM_SLACK
        for h in range(2):
            kmax_ref[h] = jnp.broadcast_to(kmax[:, h:h + 1], kmax_ref.shape[1:])

    q = q_ref[...]
    if fox:
        q_aug_lanes = (sub < 3).astype(bf16)
    else:
        q_aug_lanes = jnp.where((sub == 0) | (sub == 3), sl[0],
                                jnp.where((sub == 1) | (sub == 4), sl[1],
                                          jnp.where((sub == 2) | (sub == 5), sl[2], 0.0))
                                ).astype(bf16)
    q_aug = (jnp.where(lo_half, q, q_aug_lanes), jnp.where(lo_half, q_aug_lanes, q))

    m_ref[...] = jnp.full_like(m_ref, NEG_INF)
    acc_ref[...] = jnp.zeros_like(acc_ref)
    if not fox:
        l_ref[...] = jnp.zeros_like(l_ref)

    def tile(j, n_sub, masked):
        width = n_sub * tk
        rows = pl.ds(pl.multiple_of(j * tk, tk), width)
        if masked:
            r = lax.broadcasted_iota(jnp.int32, (tq, tk), 0)
            c = lax.broadcasted_iota(jnp.int32, (tq, tk), 1)
            causal = c <= r
        for h in range(2):
            s = lax.dot_general(q_aug[h], kaug_ref[h, rows, :], _NT, preferred_element_type=f32)
            if masked:
                diag = jnp.where(causal, s[:, width - tk:], NEG_INF)
                s = diag if n_sub == 1 else jnp.concatenate([s[:, :width - tk], diag], axis=1)
            m_prev = m_ref[h]
            m_next = jnp.maximum(m_prev, jnp.max(s, axis=1, keepdims=True))
            alpha = jnp.exp2(m_prev - m_next)
            p = jnp.exp2(s - jnp.tile(m_next, (1, width // LANES)))
            if fox:
                v = vaug_ref[h, rows, :]
            else:
                v = v_ref[rows, :]
                part = p[:, :LANES]
                for blk in range(1, width // LANES):
                    part = part + p[:, blk * LANES:(blk + 1) * LANES]
                l_ref[h] = alpha * l_ref[h] + part
            acc_ref[h] = (acc_ref[h] * alpha
                          + jnp.dot(p.astype(bf16), v, preferred_element_type=f32))
            m_ref[h] = m_next

    @pl.when(i == 0)
    def _():
        tile(0, 1, masked=True)

    @pl.when(i > 0)
    def _():
        tile(i - 1, 2, masked=True)

    top = jnp.maximum(i - 1, 0)
    qn = half_norms_sq(q)
    gap = [jnp.max(jnp.sqrt(qn[:, h:h + 1]) * kmax_ref[h][0:1, 0:1] - m_ref[h][:, 0:1],
                   axis=0, keepdims=True) for h in range(2)]
    if fox:
        tile_idx = lax.broadcasted_iota(jnp.int32, cend_ref.shape, 0)
        head_lane = lax.broadcasted_iota(jnp.int32, (1, LANES), 1) - 2 * grp
        thr = jnp.where(head_lane == 0, gap[0], gap[1])
        dead = (-cend_ref[...] < -UNDERFLOW_LOG2 - thr) & (tile_idx < top)
        cnt = jnp.sum(dead.astype(f32), axis=0, keepdims=True)
        cnt = jnp.where((head_lane == 0) | (head_lane == 1), cnt, 1.0 * n_tiles)
        j_lo = jnp.min(cnt, axis=1, keepdims=True).astype(jnp.int32)[0, 0]
    else:
        x = (-UNDERFLOW_LOG2 - jnp.maximum(gap[0], gap[1])) / slope
        j_lo = jnp.clip(jnp.floor(x * (1.0 / tk)), 0.0, 1.0 * n_tiles)
        j_lo = jnp.minimum(j_lo.astype(jnp.int32)[0, 0], top)

    n_live = top - j_lo
    one = n_live & 1
    two = n_live & 2

    @pl.when(one == 1)
    def _():
        tile(top - 1, 1, masked=False)

    @pl.when(two == 2)
    def _():
        tile(top - one - 2, 2, masked=False)

    def body(step, carry):
        tile(top - one - two - TILES_PER_STEP * (step + 1), TILES_PER_STEP, masked=False)
        return carry

    lax.fori_loop(0, n_live // TILES_PER_STEP, body, 0)

    if fox:
        out = jnp.where(lo_half, acc_ref[0] / acc_ref[0][:, HEAD_DIM:HEAD_DIM + 1],
                        acc_ref[1] / acc_ref[1][:, 0:1])
    else:
        lam = (jnp.exp(jnp.sum(lq1_ref[...] * lk1_ref[...], axis=1, keepdims=True))
               - jnp.exp(jnp.sum(lq2_ref[...] * lk2_ref[...], axis=1, keepdims=True))
               + LAMBDA_INIT)
        a = (acc_ref[0] / jnp.sum(l_ref[0], axis=1, keepdims=True)
             - lam * (acc_ref[1] / jnp.sum(l_ref[1], axis=1, keepdims=True)))
        out = _rms(a, gn_ref[...]) * (1.0 - LAMBDA_INIT)
    o_ref[...] = out.astype(o_ref.dtype)


def _attention(proj, extra, *, fox):
    s = proj.shape[0]
    groups = FOX_HEADS // 2 if fox else DIFF_HEADS
    base = 0 if fox else 3 * (FOX_WIDTH // LANES)
    per = FOX_WIDTH // LANES if fox else DIFF_WIDTH // LANES
    resident = lambda col0: pl.BlockSpec((s, LANES), lambda g, i: (0, col0 + g),
                                         pipeline_mode=pl.Buffered(1))
    in_specs = [
        pl.BlockSpec((TQ, LANES), lambda g, i: (i, base + g)),
        resident(base + per),
        resident(base + 2 * per),
    ]
    args = [proj, proj, proj]
    if fox:
        caug, cend = extra
        in_specs.append(resident(0))
        in_specs.append(pl.BlockSpec(cend.shape, lambda g, i: (0, 0)))
        args += [caug, cend]
        last_scratch = pltpu.VMEM((2, s, LANES), jnp.bfloat16)
    else:
        for a in extra:
            in_specs.append(pl.BlockSpec(a.shape, lambda g, i: (0, 0)))
            args.append(a)
        last_scratch = pltpu.VMEM((2, TQ, LANES), jnp.float32)
    return pl.pallas_call(
        functools.partial(_attn_kernel, fox=fox),
        grid=(groups, s // TQ),
        in_specs=in_specs,
        out_specs=pl.BlockSpec((TQ, LANES), lambda g, i: (i, g)),
        out_shape=jax.ShapeDtypeStruct((s, groups * LANES), jnp.bfloat16),
        scratch_shapes=[
            pltpu.VMEM((2, TQ, LANES), jnp.float32),
            pltpu.VMEM((2, TQ, LANES), jnp.float32),
            pltpu.VMEM((2, 8, LANES), jnp.float32),
            pltpu.VMEM((2, s, LANES), jnp.bfloat16),
            last_scratch,
        ],
        compiler_params=pltpu.CompilerParams(
            dimension_semantics=("arbitrary", "arbitrary"), vmem_limit_bytes=VMEM_LIMIT),
        name="fox_attn" if fox else "diff_attn",
    )(*args)


def _out_ffn_kernel(x_ref, fo_ref, do_ref, wo_ref, gf_ref, wg_ref, wu_ref, wd_ref,
                    gl_ref, o_ref):
    x1 = (x_ref[...]
          + jnp.dot(fo_ref[...], wo_ref[:FOX_WIDTH, :], preferred_element_type=jnp.float32)
          + jnp.dot(do_ref[...], wo_ref[FOX_WIDTH:, :], preferred_element_type=jnp.float32))
    h = _rms(x1, gf_ref[...]).astype(jnp.bfloat16)
    g = jnp.dot(h, wg_ref[...], preferred_element_type=jnp.float32)
    u = jnp.dot(h, wu_ref[...], preferred_element_type=jnp.float32)
    a = (g * jax.nn.sigmoid(g) * u).astype(jnp.bfloat16)
    x2 = x1 + jnp.dot(a, wd_ref[...], preferred_element_type=jnp.float32)
    o_ref[...] = _rms(x2, gl_ref[...])


def _out_ffn(x, fo, do, wo, gf, wg, wu, wd, gl):
    s = x.shape[0]
    const = lambda shape: pl.BlockSpec(shape, lambda i: (0, 0), pipeline_mode=pl.Buffered(1))
    return pl.pallas_call(
        _out_ffn_kernel,
        grid=(s // TM_FFN,),
        in_specs=[
            pl.BlockSpec((TM_FFN, D_MODEL), lambda i: (i, 0)),
            pl.BlockSpec((TM_FFN, FOX_WIDTH), lambda i: (i, 0)),
            pl.BlockSpec((TM_FFN, DIFF_WIDTH), lambda i: (i, 0)),
            const(wo.shape), const(gf.shape), const(wg.shape), const(wu.shape),
            const(wd.shape), const(gl.shape),
        ],
        out_specs=pl.BlockSpec((TM_FFN, D_MODEL), lambda i: (i, 0)),
        out_shape=jax.ShapeDtypeStruct((s, D_MODEL), jnp.float32),
        compiler_params=pltpu.CompilerParams(
            dimension_semantics=("arbitrary",), vmem_limit_bytes=VMEM_LIMIT),
        name="out_ffn",
    )(x, fo, do, wo, gf, wg, wu, wd, gl)


def kernel(x, mix_norm_g, w_in, b_forget, lambda_q1, lambda_k1, lambda_q2, lambda_k2,
           diff_norm_g, w_out, ffn_norm_g, w_gate, w_up, w_down, final_norm_g):
    b, s, d = x.shape
    assert b == 1 and d == D_MODEL and w_in.shape[0] == 1
    assert s % TQ == 0 and s % TM_IN == 0 and s % TM_FFN == 0
    assert TM_IN == TK and TQ == TK and TILES_PER_STEP == 4
    assert s < 2 ** 14 + 1
    bf16 = jnp.bfloat16
    x2d = x.reshape(s, d)

    w = w_in[0]
    o_gate = 3 * FOX_WIDTH
    w_r = jnp.concatenate(
        [w[:, :o_gate], w[:, o_gate + FOX_HEADS:], w[:, o_gate:o_gate + FOX_HEADS],
         jnp.zeros((d, GATE_PAD - FOX_HEADS), w.dtype)], axis=1).astype(bf16)
    b_pad = jnp.pad(b_forget[0].astype(jnp.float32), (0, GATE_PAD - FOX_HEADS)).reshape(1, GATE_PAD)
    place = jnp.asarray(_fox_aug_placement(), bf16)

    proj, caug, cend = _in_proj(x2d, mix_norm_g[0].reshape(1, d), w_r, b_pad, place)

    fox_out = _attention(proj, (caug, cend), fox=True)
    lam_args = tuple(a[0].astype(jnp.float32).reshape(1, HEAD_DIM)
                     for a in (lambda_q1, lambda_k1, lambda_q2, lambda_k2))
    diff_out = _attention(
        proj, lam_args + (diff_norm_g[0].astype(jnp.float32).reshape(1, DIFF_V_DIM),), fox=False)

    out = _out_ffn(
        x2d, fox_out, diff_out, w_out[0].astype(bf16), ffn_norm_g[0].reshape(1, d),
        w_gate[0].astype(bf16), w_up[0].astype(bf16), w_down[0].astype(bf16),
        final_norm_g.reshape(1, d))
    return out.reshape(b, s, d)
```

```python
import functools
import math

import numpy as np
import jax
import jax.numpy as jnp
from jax import lax
from jax.experimental import pallas as pl
from jax.experimental.pallas import tpu as pltpu

D_MODEL = 1024
HEAD_DIM = 64
FOX_HEADS = 8
FOX_WIDTH = FOX_HEADS * HEAD_DIM
DIFF_HEADS = 4
DIFF_V_DIM = 2 * HEAD_DIM
DIFF_WIDTH = DIFF_HEADS * DIFF_V_DIM
D_FF = ((8 * D_MODEL + 767) // 768) * 256
EPS = 1e-6
NEG_INF = -1e30
LOG2E = math.log2(math.e)
Q_SCALE = LOG2E / math.sqrt(HEAD_DIM)
UNDERFLOW_LOG2 = 150.0
NORM_SLACK = 1.01
LAMBDA_INIT = 0.8 - 0.6 * math.exp(-0.3 * 0)

LANES = 128
GATE_PAD = LANES
QKV_WIDTH = 3 * FOX_WIDTH + 3 * DIFF_WIDTH
VMEM_LIMIT = 56 * 1024 * 1024

TM_IN = 512
TQ = 512
TK = 512
TILES_PER_STEP = 4
TM_FFN = 512

_NT = (((1,), (1,)), ((), ()))


def _rms(x, g):
    return x * lax.rsqrt(jnp.mean(x * x, axis=-1, keepdims=True) + EPS) * g


def _split3(x):
    hi = x.astype(jnp.bfloat16)
    r1 = x - hi.astype(jnp.float32)
    mid = r1.astype(jnp.bfloat16)
    lo = (r1 - mid.astype(jnp.float32)).astype(jnp.bfloat16)
    return hi, mid, lo


def _reduce_rows(x, op):
    while x.shape[0] > 8 and x.shape[0] % 2 == 0:
        half = x.shape[0] // 2
        x = op(x[:half], x[half:])
    reduce = jnp.max if op is jnp.maximum else jnp.sum
    return reduce(x, axis=0, keepdims=True)


def _fox_aug_placement():
    p = np.zeros((3 * LANES, FOX_WIDTH), np.float32)
    for part in range(3):
        for head in range(FOX_HEADS):
            lane = HEAD_DIM + part if head % 2 == 0 else part
            p[part * LANES + head, (head // 2) * LANES + lane] = 1.0
    return p


def _in_proj_kernel(x_ref, g_ref, w_ref, b_ref, place_ref, proj_ref, caug_ref, cend_ref,
                    tri_ref, carry_ref):
    step = pl.program_id(0)
    tm = x_ref.shape[0]

    @pl.when(step == 0)
    def _():
        r = lax.broadcasted_iota(jnp.int32, (tm, tm), 0)
        c = lax.broadcasted_iota(jnp.int32, (tm, tm), 1)
        tri_ref[...] = jnp.where(r >= c, 1.0, 0.0).astype(jnp.bfloat16)
        carry_ref[...] = jnp.zeros_like(carry_ref)

    h = _rms(x_ref[...], g_ref[...]).astype(jnp.bfloat16)
    y = jnp.dot(h, w_ref[...], preferred_element_type=jnp.float32)
    bf16 = jnp.bfloat16
    dq0 = 3 * FOX_WIDTH
    proj_ref[:, :FOX_WIDTH] = (y[:, :FOX_WIDTH] * Q_SCALE).astype(bf16)
    proj_ref[:, FOX_WIDTH:dq0] = y[:, FOX_WIDTH:dq0].astype(bf16)
    proj_ref[:, dq0:dq0 + DIFF_WIDTH] = (y[:, dq0:dq0 + DIFF_WIDTH] * Q_SCALE).astype(bf16)
    proj_ref[:, dq0 + DIFF_WIDTH:] = y[:, dq0 + DIFF_WIDTH:QKV_WIDTH].astype(bf16)

    z = y[:, QKV_WIDTH:] + b_ref[...]
    logf = LOG2E * (jnp.minimum(z, 0.0) - jnp.log1p(jnp.exp(-jnp.abs(z))))
    cs = jnp.dot(tri_ref[...], jnp.concatenate(_split3(logf), axis=1),
                 preferred_element_type=jnp.float32)
    c = cs[:, :LANES] + cs[:, LANES:2 * LANES] + cs[:, 2 * LANES:] + carry_ref[0:1, :]
    last = c[tm - 1:tm, :]
    carry_ref[...] = jnp.broadcast_to(last, carry_ref.shape)
    cend_ref[pl.ds(step, 1), :] = last
    caug_ref[...] = jnp.dot(jnp.concatenate(_split3(-c), axis=1), place_ref[...],
                            preferred_element_type=jnp.float32).astype(bf16)


def _in_proj(x, g, w, b, place):
    s = x.shape[0]
    n = w.shape[1]
    n_tiles = s // TM_IN
    return pl.pallas_call(
        _in_proj_kernel,
        grid=(n_tiles,),
        in_specs=[
            pl.BlockSpec((TM_IN, D_MODEL), lambda i: (i, 0)),
            pl.BlockSpec((1, D_MODEL), lambda i: (0, 0)),
            pl.BlockSpec((D_MODEL, n), lambda i: (0, 0)),
            pl.BlockSpec((1, GATE_PAD), lambda i: (0, 0)),
            pl.BlockSpec(place.shape, lambda i: (0, 0)),
        ],
        out_specs=[
            pl.BlockSpec((TM_IN, QKV_WIDTH), lambda i: (i, 0)),
            pl.BlockSpec((TM_IN, FOX_WIDTH), lambda i: (i, 0)),
            pl.BlockSpec((n_tiles, LANES), lambda i: (0, 0)),
        ],
        out_shape=[
            jax.ShapeDtypeStruct((s, QKV_WIDTH), jnp.bfloat16),
            jax.ShapeDtypeStruct((s, FOX_WIDTH), jnp.bfloat16),
            jax.ShapeDtypeStruct((n_tiles, LANES), jnp.float32),
        ],
        scratch_shapes=[
            pltpu.VMEM((TM_IN, TM_IN), jnp.bfloat16),
            pltpu.VMEM((8, LANES), jnp.float32),
        ],
        compiler_params=pltpu.CompilerParams(
            dimension_semantics=("arbitrary",), vmem_limit_bytes=VMEM_LIMIT),
        name="in_proj",
    )(x, g, w, b, place)


def _attn_kernel(*refs, fox):
    if fox:
        (q_ref, k_ref, v_ref, caug_ref, cend_ref,
         o_ref, m_ref, acc_ref, kmax_ref, kaug_ref, vaug_ref) = refs
    else:
        (q_ref, k_ref, v_ref, lq1_ref, lk1_ref, lq2_ref, lk2_ref, gn_ref,
         o_ref, m_ref, acc_ref, kmax_ref, kaug_ref, l_ref) = refs
    grp = pl.program_id(0)
    i = pl.program_id(1)
    tq, tk = TQ, TK
    s_len = k_ref.shape[0]
    n_tiles = s_len // tk
    f32, bf16 = jnp.float32, jnp.bfloat16

    lane = lax.broadcasted_iota(jnp.int32, (1, LANES), 1)
    lo_half = lane < HEAD_DIM
    sub = lane & (HEAD_DIM - 1)

    if not fox:
        slope = LOG2E * jnp.exp2(-2.0 * jnp.full((1, 1), grp + 1, jnp.int32).astype(f32))
        sl = [p.astype(f32) for p in _split3(slope)]
        inv_slope = 1.0 / slope

    def half_norms_sq(x):
        l_in = lax.broadcasted_iota(jnp.int32, (LANES, LANES), 0)
        l_out = lax.broadcasted_iota(jnp.int32, (LANES, LANES), 1)
        half_sum = ((l_in // HEAD_DIM) == l_out).astype(bf16)
        return jnp.dot(x * x, half_sum, preferred_element_type=f32)

    @pl.when(i == 0)
    def _():
        def chunk(cidx, mx):
            rows = pl.ds(pl.multiple_of(cidx * tk, tk), tk)
            k = k_ref[rows, :]
            if fox:
                aug = caug_ref[rows, :]
                v = v_ref[rows, :]
                vaug_ref[0, rows, :] = jnp.where(lo_half, v, (lane == HEAD_DIM).astype(bf16))
                vaug_ref[1, rows, :] = jnp.where(lo_half, (lane == 0).astype(bf16), v)
            else:
                pos = cidx * tk + lax.broadcasted_iota(jnp.int32, (tk, LANES), 0)
                aug = jnp.where(sub < 3, pos & ~(LANES - 1),
                                jnp.where(sub < 6, pos & (LANES - 1), 0)).astype(f32).astype(bf16)
            kaug_ref[0, rows, :] = jnp.where(lo_half, k, aug)
            kaug_ref[1, rows, :] = jnp.where(lo_half, aug, k)
            return jnp.maximum(mx, half_norms_sq(k))
        mx = lax.fori_loop(0, n_tiles, chunk, jnp.zeros((tk, LANES), f32))
        kmax = jnp.sqrt(jnp.max(mx, axis=0, keepdims=True)) * NORM_SLACK
        for h in range(2):
            kmax_ref[h] = jnp.broadcast_to(kmax[:, h:h + 1], kmax_ref.shape[1:])

    q = q_ref[...]
    if fox:
        q_aug_lanes = (sub < 3).astype(bf16)
    else:
        q_aug_lanes = jnp.where((sub == 0) | (sub == 3), sl[0],
                                jnp.where((sub == 1) | (sub == 4), sl[1],
                                          jnp.where((sub == 2) | (sub == 5), sl[2], 0.0))
                                ).astype(bf16)
    q_aug = (jnp.where(lo_half, q, q_aug_lanes), jnp.where(lo_half, q_aug_lanes, q))

    m_ref[...] = jnp.full_like(m_ref, NEG_INF)
    acc_ref[...] = jnp.zeros_like(acc_ref)
    if not fox:
        l_ref[...] = jnp.zeros_like(l_ref)

    def tile(j, n_sub, masked):
        width = n_sub * tk
        rows = pl.ds(pl.multiple_of(j * tk, tk), width)
        if masked:
            r = lax.broadcasted_iota(jnp.int32, (tq, width), 0)
            c = lax.broadcasted_iota(jnp.int32, (tq, width), 1)
            causal = c + (j - i) * tk <= r
        for h in range(2):
            s = lax.dot_general(q_aug[h], kaug_ref[h, rows, :], _NT, preferred_element_type=f32)
            if masked:
                s = jnp.where(causal, s, NEG_INF)
            m_prev = m_ref[h]
            m_next = jnp.maximum(m_prev, jnp.max(s, axis=1, keepdims=True))
            alpha = jnp.exp2(m_prev - m_next)
            p = jnp.exp2(s - jnp.tile(m_next, (1, width // LANES)))
            if fox:
                v = vaug_ref[h, rows, :]
            else:
                v = v_ref[rows, :]
                part = p[:, :LANES]
                for blk in range(1, width // LANES):
                    part = part + p[:, blk * LANES:(blk + 1) * LANES]
                l_ref[h] = alpha * l_ref[h] + part
            acc_ref[h] = (acc_ref[h] * alpha
                          + jnp.dot(p.astype(bf16), v, preferred_element_type=f32))
            m_ref[h] = m_next

    top = jnp.maximum(i - 1, 0)
    qn = half_norms_sq(q)
    bound = [jnp.sqrt(qn[:, h:h + 1]) * kmax_ref[h][0:1, 0:1] for h in range(2)]
    tile(top, 2, masked=True)

    if fox:
        head_lane = lax.broadcasted_iota(jnp.int32, cend_ref.shape, 1) - 2 * grp
        cend = [jnp.sum(jnp.where(head_lane == h, cend_ref[...], 0.0), axis=1, keepdims=True)
                for h in range(2)]
        below_top = lax.broadcasted_iota(jnp.int32, (n_tiles, 1), 0) < top
        dead = below_top
        for h in range(2):
            gap = _reduce_rows(bound[h] - m_ref[h][:, 0:1], jnp.maximum)
            dead = dead & (-cend[h] < -UNDERFLOW_LOG2 - gap)
        j_lo = _reduce_rows(dead.astype(f32), jnp.add).astype(jnp.int32)[0, 0]
    else:
        gap = _reduce_rows(jnp.maximum(bound[0] - m_ref[0][:, 0:1],
                                       bound[1] - m_ref[1][:, 0:1]), jnp.maximum)
        x = (-UNDERFLOW_LOG2 - gap) * inv_slope
        j_lo = jnp.clip(jnp.floor(x * (1.0 / tk)), 0.0, 1.0 * n_tiles)
        j_lo = jnp.minimum(j_lo.astype(jnp.int32)[0, 0], top)

    n_live = top - j_lo
    one = n_live & 1
    two = n_live & 2

    @pl.when(one == 1)
    def _():
        tile(top - 1, 1, masked=False)

    @pl.when(two == 2)
    def _():
        tile(top - one - 2, 2, masked=False)

    def body(step, carry):
        tile(top - one - two - TILES_PER_STEP * (step + 1), TILES_PER_STEP, masked=False)
        return carry

    lax.fori_loop(0, n_live // TILES_PER_STEP, body, 0)

    if fox:
        out = jnp.where(lo_half, acc_ref[0] / acc_ref[0][:, HEAD_DIM:HEAD_DIM + 1],
                        acc_ref[1] / acc_ref[1][:, 0:1])
    else:
        lam = (jnp.exp(jnp.sum(lq1_ref[...] * lk1_ref[...], axis=1, keepdims=True))
               - jnp.exp(jnp.sum(lq2_ref[...] * lk2_ref[...], axis=1, keepdims=True))
               + LAMBDA_INIT)
        a = (acc_ref[0] / jnp.sum(l_ref[0], axis=1, keepdims=True)
             - lam * (acc_ref[1] / jnp.sum(l_ref[1], axis=1, keepdims=True)))
        out = _rms(a, gn_ref[...]) * (1.0 - LAMBDA_INIT)
    o_ref[...] = out.astype(o_ref.dtype)


def _attention(proj, extra, *, fox):
    s = proj.shape[0]
    groups = FOX_HEADS // 2 if fox else DIFF_HEADS
    base = 0 if fox else 3 * (FOX_WIDTH // LANES)
    per = FOX_WIDTH // LANES if fox else DIFF_WIDTH // LANES
    resident = lambda col0: pl.BlockSpec((s, LANES), lambda g, i: (0, col0 + g),
                                         pipeline_mode=pl.Buffered(1))
    in_specs = [
        pl.BlockSpec((TQ, LANES), lambda g, i: (i, base + g)),
        resident(base + per),
        resident(base + 2 * per),
    ]
    args = [proj, proj, proj]
    if fox:
        caug, cend = extra
        in_specs.append(resident(0))
        in_specs.append(pl.BlockSpec(cend.shape, lambda g, i: (0, 0)))
        args += [caug, cend]
        last_scratch = pltpu.VMEM((2, s, LANES), jnp.bfloat16)
    else:
        for a in extra:
            in_specs.append(pl.BlockSpec(a.shape, lambda g, i: (0, 0)))
            args.append(a)
        last_scratch = pltpu.VMEM((2, TQ, LANES), jnp.float32)
    return pl.pallas_call(
        functools.partial(_attn_kernel, fox=fox),
        grid=(groups, s // TQ),
        in_specs=in_specs,
        out_specs=pl.BlockSpec((TQ, LANES), lambda g, i: (i, g)),
        out_shape=jax.ShapeDtypeStruct((s, groups * LANES), jnp.bfloat16),
        scratch_shapes=[
            pltpu.VMEM((2, TQ, LANES), jnp.float32),
            pltpu.VMEM((2, TQ, LANES), jnp.float32),
            pltpu.VMEM((2, 8, LANES), jnp.float32),
            pltpu.VMEM((2, s, LANES), jnp.bfloat16),
            last_scratch,
        ],
        compiler_params=pltpu.CompilerParams(
            dimension_semantics=("arbitrary", "arbitrary"), vmem_limit_bytes=VMEM_LIMIT),
        name="fox_attn" if fox else "diff_attn",
    )(*args)


def _out_ffn_kernel(x_ref, fo_ref, do_ref, wo_ref, gf_ref, wg_ref, wu_ref, wd_ref,
                    gl_ref, o_ref):
    x1 = (x_ref[...]
          + jnp.dot(fo_ref[...], wo_ref[:FOX_WIDTH, :], preferred_element_type=jnp.float32)
          + jnp.dot(do_ref[...], wo_ref[FOX_WIDTH:, :], preferred_element_type=jnp.float32))
    h = _rms(x1, gf_ref[...]).astype(jnp.bfloat16)
    g = jnp.dot(h, wg_ref[...], preferred_element_type=jnp.float32)
    u = jnp.dot(h, wu_ref[...], preferred_element_type=jnp.float32)
    a = (g * jax.nn.sigmoid(g) * u).astype(jnp.bfloat16)
    x2 = x1 + jnp.dot(a, wd_ref[...], preferred_element_type=jnp.float32)
    o_ref[...] = _rms(x2, gl_ref[...])


def _out_ffn(x, fo, do, wo, gf, wg, wu, wd, gl):
    s = x.shape[0]
    const = lambda shape: pl.BlockSpec(shape, lambda i: (0, 0), pipeline_mode=pl.Buffered(1))
    return pl.pallas_call(
        _out_ffn_kernel,
        grid=(s // TM_FFN,),
        in_specs=[
            pl.BlockSpec((TM_FFN, D_MODEL), lambda i: (i, 0)),
            pl.BlockSpec((TM_FFN, FOX_WIDTH), lambda i: (i, 0)),
            pl.BlockSpec((TM_FFN, DIFF_WIDTH), lambda i: (i, 0)),
            const(wo.shape), const(gf.shape), const(wg.shape), const(wu.shape),
            const(wd.shape), const(gl.shape),
        ],
        out_specs=pl.BlockSpec((TM_FFN, D_MODEL), lambda i: (i, 0)),
        out_shape=jax.ShapeDtypeStruct((s, D_MODEL), jnp.float32),
        compiler_params=pltpu.CompilerParams(
            dimension_semantics=("arbitrary",), vmem_limit_bytes=VMEM_LIMIT),
        name="out_ffn",
    )(x, fo, do, wo, gf, wg, wu, wd, gl)


def kernel(x, mix_norm_g, w_in, b_forget, lambda_q1, lambda_k1, lambda_q2, lambda_k2,
           diff_norm_g, w_out, ffn_norm_g, w_gate, w_up, w_down, final_norm_g):
    b, s, d = x.shape
    assert b == 1 and d == D_MODEL and w_in.shape[0] == 1
    assert s % TQ == 0 and s % TM_IN == 0 and s % TM_FFN == 0
    assert TM_IN == TK and TQ == TK and TILES_PER_STEP == 4 and s >= 2 * TK
    assert s < 2 ** 14 + 1
    bf16 = jnp.bfloat16
    x2d = x.reshape(s, d)

    w = w_in[0]
    o_gate = 3 * FOX_WIDTH
    w_r = jnp.concatenate(
        [w[:, :o_gate], w[:, o_gate + FOX_HEADS:], w[:, o_gate:o_gate + FOX_HEADS],
         jnp.zeros((d, GATE_PAD - FOX_HEADS), w.dtype)], axis=1).astype(bf16)
    b_pad = jnp.pad(b_forget[0].astype(jnp.float32), (0, GATE_PAD - FOX_HEADS)).reshape(1, GATE_PAD)
    place = jnp.asarray(_fox_aug_placement(), bf16)

    proj, caug, cend = _in_proj(x2d, mix_norm_g[0].reshape(1, d), w_r, b_pad, place)

    fox_out = _attention(proj, (caug, cend), fox=True)
    lam_args = tuple(a[0].astype(jnp.float32).reshape(1, HEAD_DIM)
                     for a in (lambda_q1, lambda_k1, lambda_q2, lambda_k2))
    diff_out = _attention(
        proj, lam_args + (diff_norm_g[0].astype(jnp.float32).reshape(1, DIFF_V_DIM),), fox=False)

    out = _out_ffn(
        x2d, fox_out, diff_out, w_out[0].astype(bf16), ffn_norm_g[0].reshape(1, d),
        w_gate[0].astype(bf16), w_up[0].astype(bf16), w_down[0].astype(bf16),
        final_norm_g.reshape(1, d))
    return out.reshape(b, s, d)
```

```python
import functools
import math

import numpy as np
import jax
import jax.numpy as jnp
from jax import lax
from jax.experimental import pallas as pl
from jax.experimental.pallas import tpu as pltpu

D_MODEL = 1024
HEAD_DIM = 64
FOX_HEADS = 8
FOX_WIDTH = FOX_HEADS * HEAD_DIM
DIFF_HEADS = 4
DIFF_V_DIM = 2 * HEAD_DIM
DIFF_WIDTH = DIFF_HEADS * DIFF_V_DIM
D_FF = ((8 * D_MODEL + 767) // 768) * 256
EPS = 1e-6
NEG_INF = -1e30
LOG2E = math.log2(math.e)
Q_SCALE = LOG2E / math.sqrt(HEAD_DIM)
UNDERFLOW_LOG2 = 150.0
NORM_SLACK = 1.01
LAMBDA_INIT = 0.8 - 0.6 * math.exp(-0.3 * 0)

LANES = 128
GATE_PAD = LANES
QKV_WIDTH = 3 * FOX_WIDTH + 3 * DIFF_WIDTH
VMEM_LIMIT = 56 * 1024 * 1024

TM_IN = 512
TQ = 512
TK = 512
TILES_PER_STEP = 4
TM_FFN = 512

_NT = (((1,), (1,)), ((), ()))


def _rms(x, g):
    return x * lax.rsqrt(jnp.mean(x * x, axis=-1, keepdims=True) + EPS) * g


def _split3(x):
    hi = x.astype(jnp.bfloat16)
    r1 = x - hi.astype(jnp.float32)
    mid = r1.astype(jnp.bfloat16)
    lo = (r1 - mid.astype(jnp.float32)).astype(jnp.bfloat16)
    return hi, mid, lo


def _reduce_rows(x, op):
    while x.shape[0] > 8 and x.shape[0] % 2 == 0:
        half = x.shape[0] // 2
        x = op(x[:half], x[half:])
    reduce = jnp.max if op is jnp.maximum else jnp.sum
    return reduce(x, axis=0, keepdims=True)


def _fox_aug_placement():
    p = np.zeros((3 * LANES, FOX_WIDTH), np.float32)
    for part in range(3):
        for head in range(FOX_HEADS):
            lane = 3 * (head % 2) + part
            p[part * LANES + head, (head // 2) * LANES + lane] = 1.0
    return p


def _in_proj_kernel(x_ref, g_ref, w_ref, b_ref, place_ref, proj_ref, caug_ref, cend_ref,
                    tri_ref, carry_ref):
    step = pl.program_id(0)
    tm = x_ref.shape[0]

    @pl.when(step == 0)
    def _():
        r = lax.broadcasted_iota(jnp.int32, (tm, tm), 0)
        c = lax.broadcasted_iota(jnp.int32, (tm, tm), 1)
        tri_ref[...] = jnp.where(r >= c, 1.0, 0.0).astype(jnp.bfloat16)
        carry_ref[...] = jnp.zeros_like(carry_ref)

    h = _rms(x_ref[...], g_ref[...]).astype(jnp.bfloat16)
    y = jnp.dot(h, w_ref[...], preferred_element_type=jnp.float32)
    bf16 = jnp.bfloat16
    dq0 = 3 * FOX_WIDTH
    proj_ref[:, :FOX_WIDTH] = (y[:, :FOX_WIDTH] * Q_SCALE).astype(bf16)
    proj_ref[:, FOX_WIDTH:dq0] = y[:, FOX_WIDTH:dq0].astype(bf16)
    proj_ref[:, dq0:dq0 + DIFF_WIDTH] = (y[:, dq0:dq0 + DIFF_WIDTH] * Q_SCALE).astype(bf16)
    proj_ref[:, dq0 + DIFF_WIDTH:] = y[:, dq0 + DIFF_WIDTH:QKV_WIDTH].astype(bf16)

    z = y[:, QKV_WIDTH:] + b_ref[...]
    logf = LOG2E * (jnp.minimum(z, 0.0) - jnp.log1p(jnp.exp(-jnp.abs(z))))
    cs = jnp.dot(tri_ref[...], jnp.concatenate(_split3(logf), axis=1),
                 preferred_element_type=jnp.float32)
    c = cs[:, :LANES] + cs[:, LANES:2 * LANES] + cs[:, 2 * LANES:] + carry_ref[0:1, :]
    last = c[tm - 1:tm, :]
    carry_ref[...] = jnp.broadcast_to(last, carry_ref.shape)
    cend_ref[pl.ds(step, 1), :] = last
    caug_ref[...] = jnp.dot(jnp.concatenate(_split3(-c), axis=1), place_ref[...],
                            preferred_element_type=jnp.float32).astype(bf16)


def _in_proj(x, g, w, b, place):
    s = x.shape[0]
    n = w.shape[1]
    n_tiles = s // TM_IN
    return pl.pallas_call(
        _in_proj_kernel,
        grid=(n_tiles,),
        in_specs=[
            pl.BlockSpec((TM_IN, D_MODEL), lambda i: (i, 0)),
            pl.BlockSpec((1, D_MODEL), lambda i: (0, 0)),
            pl.BlockSpec((D_MODEL, n), lambda i: (0, 0)),
            pl.BlockSpec((1, GATE_PAD), lambda i: (0, 0)),
            pl.BlockSpec(place.shape, lambda i: (0, 0)),
        ],
        out_specs=[
            pl.BlockSpec((TM_IN, QKV_WIDTH), lambda i: (i, 0)),
            pl.BlockSpec((TM_IN, FOX_WIDTH), lambda i: (i, 0)),
            pl.BlockSpec((n_tiles, LANES), lambda i: (0, 0)),
        ],
        out_shape=[
            jax.ShapeDtypeStruct((s, QKV_WIDTH), jnp.bfloat16),
            jax.ShapeDtypeStruct((s, FOX_WIDTH), jnp.bfloat16),
            jax.ShapeDtypeStruct((n_tiles, LANES), jnp.float32),
        ],
        scratch_shapes=[
            pltpu.VMEM((TM_IN, TM_IN), jnp.bfloat16),
            pltpu.VMEM((8, LANES), jnp.float32),
        ],
        compiler_params=pltpu.CompilerParams(
            dimension_semantics=("arbitrary",), vmem_limit_bytes=VMEM_LIMIT),
        name="in_proj",
    )(x, g, w, b, place)


def _attn_kernel(*refs, fox):
    if fox:
        (q_ref, k_ref, v_ref, caug_ref, cend_ref,
         o_ref, m_ref, acc_ref, kmax_ref, kaug_ref, vaug_ref) = refs
    else:
        (q_ref, k_ref, v_ref, lq1_ref, lk1_ref, lq2_ref, lk2_ref, gn_ref,
         o_ref, m_ref, acc_ref, kmax_ref, kaug_ref, l_ref) = refs
    grp = pl.program_id(0)
    i = pl.program_id(1)
    tq, tk = TQ, TK
    s_len = k_ref.shape[0]
    n_tiles = s_len // tk
    f32, bf16 = jnp.float32, jnp.bfloat16

    lane = lax.broadcasted_iota(jnp.int32, (1, LANES), 1)
    lo_half = lane < HEAD_DIM

    if not fox:
        slope = LOG2E * jnp.exp2(-2.0 * jnp.full((1, 1), grp + 1, jnp.int32).astype(f32))
        sl = [p.astype(f32) for p in _split3(slope)]
        inv_slope = 1.0 / slope

    def half_norms_sq(x):
        l_in = lax.broadcasted_iota(jnp.int32, (LANES, LANES), 0)
        l_out = lax.broadcasted_iota(jnp.int32, (LANES, LANES), 1)
        half_sum = ((l_in // HEAD_DIM) == l_out).astype(bf16)
        return jnp.dot(x * x, half_sum, preferred_element_type=f32)

    @pl.when(i == 0)
    def _():
        def chunk(cidx, mx):
            rows = pl.ds(pl.multiple_of(cidx * tk, tk), tk)
            k = k_ref[rows, :]
            if fox:
                aug = caug_ref[rows, :]
                v = v_ref[rows, :]
                vaug_ref[0, rows, :] = jnp.where(lo_half, v, (lane == HEAD_DIM).astype(bf16))
                vaug_ref[1, rows, :] = jnp.where(lo_half, (lane == 0).astype(bf16), v)
            else:
                pos = cidx * tk + lax.broadcasted_iota(jnp.int32, (tk, LANES), 0)
                aug = jnp.where(lane < 3, pos & ~(LANES - 1),
                                jnp.where(lane < 6, pos & (LANES - 1), 0)).astype(f32).astype(bf16)
            kaug_ref[rows, :LANES] = k
            kaug_ref[rows, LANES:] = aug
            return jnp.maximum(mx, half_norms_sq(k))
        mx = lax.fori_loop(0, n_tiles, chunk, jnp.zeros((tk, LANES), f32))
        kmax = jnp.sqrt(jnp.max(mx, axis=0, keepdims=True)) * NORM_SLACK
        for h in range(2):
            kmax_ref[h] = jnp.broadcast_to(kmax[:, h:h + 1], kmax_ref.shape[1:])

    q = q_ref[...]
    zero = jnp.zeros_like(q)
    if fox:
        q_bias = [jnp.broadcast_to(((lane >= 3 * h) & (lane < 3 * h + 3)).astype(bf16), q.shape)
                  for h in range(2)]
    else:
        parts = jnp.where((lane == 0) | (lane == 3), sl[0],
                          jnp.where((lane == 1) | (lane == 4), sl[1],
                                    jnp.where((lane == 2) | (lane == 5), sl[2], 0.0)))
        q_bias = [jnp.broadcast_to(parts.astype(bf16), q.shape)] * 2
    q2 = jnp.concatenate(
        [jnp.concatenate([jnp.where(lo_half, q, zero), q_bias[0]], axis=1),
         jnp.concatenate([jnp.where(lo_half, zero, q), q_bias[1]], axis=1)], axis=0)

    m_ref[...] = jnp.full_like(m_ref, NEG_INF)
    acc_ref[...] = jnp.zeros_like(acc_ref)
    if not fox:
        l_ref[...] = jnp.zeros_like(l_ref)

    def tile(j, n_sub, masked):
        width = n_sub * tk
        rows = pl.ds(pl.multiple_of(j * tk, tk), width)
        s = lax.dot_general(q2, kaug_ref[rows, :], _NT, preferred_element_type=f32)
        if masked:
            r = lax.broadcasted_iota(jnp.int32, (2 * tq, width), 0) & (tq - 1)
            c = lax.broadcasted_iota(jnp.int32, (2 * tq, width), 1)
            s = jnp.where(c + (j - i) * tk <= r, s, NEG_INF)
        m_prev = m_ref[...]
        m_next = jnp.maximum(m_prev, jnp.max(s, axis=1, keepdims=True))
        alpha = jnp.exp2(m_prev - m_next)
        p = jnp.exp2(s - jnp.tile(m_next, (1, width // LANES)))
        pb = p.astype(bf16)
        if fox:
            pv = jnp.concatenate(
                [jnp.dot(pb[:tq], vaug_ref[0, rows, :], preferred_element_type=f32),
                 jnp.dot(pb[tq:], vaug_ref[1, rows, :], preferred_element_type=f32)], axis=0)
        else:
            pv = jnp.dot(pb, v_ref[rows, :], preferred_element_type=f32)
            part = p[:, :LANES]
            for blk in range(1, width // LANES):
                part = part + p[:, blk * LANES:(blk + 1) * LANES]
            l_ref[...] = alpha * l_ref[...] + part
        acc_ref[...] = acc_ref[...] * alpha + pv
        m_ref[...] = m_next

    top = jnp.maximum(i - 1, 0)
    qn = half_norms_sq(q)
    bound = [jnp.sqrt(qn[:, h:h + 1]) * kmax_ref[h][0:1, 0:1] for h in range(2)]
    tile(top, 2, masked=True)

    m_col = [m_ref[h * tq:(h + 1) * tq, 0:1] for h in range(2)]
    if fox:
        head_lane = lax.broadcasted_iota(jnp.int32, cend_ref.shape, 1) - 2 * grp
        cend = [jnp.sum(jnp.where(head_lane == h, cend_ref[...], 0.0), axis=1, keepdims=True)
                for h in range(2)]
        below_top = lax.broadcasted_iota(jnp.int32, (n_tiles, 1), 0) < top
        dead = below_top
        for h in range(2):
            gap = _reduce_rows(bound[h] - m_col[h], jnp.maximum)
            dead = dead & (-cend[h] < -UNDERFLOW_LOG2 - gap)
        j_lo = _reduce_rows(dead.astype(f32), jnp.add).astype(jnp.int32)[0, 0]
    else:
        gap = _reduce_rows(jnp.maximum(bound[0] - m_col[0], bound[1] - m_col[1]), jnp.maximum)
        x = (-UNDERFLOW_LOG2 - gap) * inv_slope
        j_lo = jnp.clip(jnp.floor(x * (1.0 / tk)), 0.0, 1.0 * n_tiles)
        j_lo = jnp.minimum(j_lo.astype(jnp.int32)[0, 0], top)

    n_live = top - j_lo
    one = n_live & 1
    two = n_live & 2

    @pl.when(one == 1)
    def _():
        tile(top - 1, 1, masked=False)

    @pl.when(two == 2)
    def _():
        tile(top - one - 2, 2, masked=False)

    def body(step, carry):
        tile(top - one - two - TILES_PER_STEP * (step + 1), TILES_PER_STEP, masked=False)
        return carry

    lax.fori_loop(0, n_live // TILES_PER_STEP, body, 0)

    acc = [acc_ref[h * tq:(h + 1) * tq, :] for h in range(2)]
    if fox:
        out = jnp.where(lo_half, acc[0] / acc[0][:, HEAD_DIM:HEAD_DIM + 1],
                        acc[1] / acc[1][:, 0:1])
    else:
        lam = (jnp.exp(jnp.sum(lq1_ref[...] * lk1_ref[...], axis=1, keepdims=True))
               - jnp.exp(jnp.sum(lq2_ref[...] * lk2_ref[...], axis=1, keepdims=True))
               + LAMBDA_INIT)
        l = jnp.sum(l_ref[...], axis=1, keepdims=True)
        a = acc[0] / l[:tq] - lam * (acc[1] / l[tq:])
        out = _rms(a, gn_ref[...]) * (1.0 - LAMBDA_INIT)
    o_ref[...] = out.astype(o_ref.dtype)


def _attention(proj, extra, *, fox):
    s = proj.shape[0]
    groups = FOX_HEADS // 2 if fox else DIFF_HEADS
    base = 0 if fox else 3 * (FOX_WIDTH // LANES)
    per = FOX_WIDTH // LANES if fox else DIFF_WIDTH // LANES
    resident = lambda col0: pl.BlockSpec((s, LANES), lambda g, i: (0, col0 + g),
                                         pipeline_mode=pl.Buffered(1))
    in_specs = [
        pl.BlockSpec((TQ, LANES), lambda g, i: (i, base + g)),
        resident(base + per),
        resident(base + 2 * per),
    ]
    args = [proj, proj, proj]
    if fox:
        caug, cend = extra
        in_specs.append(resident(0))
        in_specs.append(pl.BlockSpec(cend.shape, lambda g, i: (0, 0)))
        args += [caug, cend]
        last_scratch = pltpu.VMEM((2, s, LANES), jnp.bfloat16)
    else:
        for a in extra:
            in_specs.append(pl.BlockSpec(a.shape, lambda g, i: (0, 0)))
            args.append(a)
        last_scratch = pltpu.VMEM((2 * TQ, LANES), jnp.float32)
    return pl.pallas_call(
        functools.partial(_attn_kernel, fox=fox),
        grid=(groups, s // TQ),
        in_specs=in_specs,
        out_specs=pl.BlockSpec((TQ, LANES), lambda g, i: (i, g)),
        out_shape=jax.ShapeDtypeStruct((s, groups * LANES), jnp.bfloat16),
        scratch_shapes=[
            pltpu.VMEM((2 * TQ, LANES), jnp.float32),
            pltpu.VMEM((2 * TQ, LANES), jnp.float32),
            pltpu.VMEM((2, 8, LANES), jnp.float32),
            pltpu.VMEM((s, 2 * LANES), jnp.bfloat16),
            last_scratch,
        ],
        compiler_params=pltpu.CompilerParams(
            dimension_semantics=("arbitrary", "arbitrary"), vmem_limit_bytes=VMEM_LIMIT),
        name="fox_attn" if fox else "diff_attn",
    )(*args)


def _out_ffn_kernel(x_ref, fo_ref, do_ref, wo_ref, gf_ref, wg_ref, wu_ref, wd_ref,
                    gl_ref, o_ref):
    x1 = (x_ref[...]
          + jnp.dot(fo_ref[...], wo_ref[:FOX_WIDTH, :], preferred_element_type=jnp.float32)
          + jnp.dot(do_ref[...], wo_ref[FOX_WIDTH:, :], preferred_element_type=jnp.float32))
    h = _rms(x1, gf_ref[...]).astype(jnp.bfloat16)
    g = jnp.dot(h, wg_ref[...], preferred_element_type=jnp.float32)
    u = jnp.dot(h, wu_ref[...], preferred_element_type=jnp.float32)
    a = (g * jax.nn.sigmoid(g) * u).astype(jnp.bfloat16)
    x2 = x1 + jnp.dot(a, wd_ref[...], preferred_element_type=jnp.float32)
    o_ref[...] = _rms(x2, gl_ref[...])


def _out_ffn(x, fo, do, wo, gf, wg, wu, wd, gl):
    s = x.shape[0]
    const = lambda shape: pl.BlockSpec(shape, lambda i: (0, 0), pipeline_mode=pl.Buffered(1))
    return pl.pallas_call(
        _out_ffn_kernel,
        grid=(s // TM_FFN,),
        in_specs=[
            pl.BlockSpec((TM_FFN, D_MODEL), lambda i: (i, 0)),
            pl.BlockSpec((TM_FFN, FOX_WIDTH), lambda i: (i, 0)),
            pl.BlockSpec((TM_FFN, DIFF_WIDTH), lambda i: (i, 0)),
            const(wo.shape), const(gf.shape), const(wg.shape), const(wu.shape),
            const(wd.shape), const(gl.shape),
        ],
        out_specs=pl.BlockSpec((TM_FFN, D_MODEL), lambda i: (i, 0)),
        out_shape=jax.ShapeDtypeStruct((s, D_MODEL), jnp.float32),
        compiler_params=pltpu.CompilerParams(
            dimension_semantics=("arbitrary",), vmem_limit_bytes=VMEM_LIMIT),
        name="out_ffn",
    )(x, fo, do, wo, gf, wg, wu, wd, gl)


def kernel(x, mix_norm_g, w_in, b_forget, lambda_q1, lambda_k1, lambda_q2, lambda_k2,
           diff_norm_g, w_out, ffn_norm_g, w_gate, w_up, w_down, final_norm_g):
    b, s, d = x.shape
    assert b == 1 and d == D_MODEL and w_in.shape[0] == 1
    assert s % TQ == 0 and s % TM_IN == 0 and s % TM_FFN == 0
    assert TM_IN == TK and TQ == TK and TILES_PER_STEP == 4 and s >= 2 * TK
    assert s < 2 ** 14 + 1
    bf16 = jnp.bfloat16
    x2d = x.reshape(s, d)

    w = w_in[0]
    o_gate = 3 * FOX_WIDTH
    w_r = jnp.concatenate(
        [w[:, :o_gate], w[:, o_gate + FOX_HEADS:], w[:, o_gate:o_gate + FOX_HEADS],
         jnp.zeros((d, GATE_PAD - FOX_HEADS), w.dtype)], axis=1).astype(bf16)
    b_pad = jnp.pad(b_forget[0].astype(jnp.float32), (0, GATE_PAD - FOX_HEADS)).reshape(1, GATE_PAD)
    place = jnp.asarray(_fox_aug_placement(), bf16)

    proj, caug, cend = _in_proj(x2d, mix_norm_g[0].reshape(1, d), w_r, b_pad, place)

    fox_out = _attention(proj, (caug, cend), fox=True)
    lam_args = tuple(a[0].astype(jnp.float32).reshape(1, HEAD_DIM)
                     for a in (lambda_q1, lambda_k1, lambda_q2, lambda_k2))
    diff_out = _attention(
        proj, lam_args + (diff_norm_g[0].astype(jnp.float32).reshape(1, DIFF_V_DIM),), fox=False)

    out = _out_ffn(
        x2d, fox_out, diff_out, w_out[0].astype(bf16), ffn_norm_g[0].reshape(1, d),
        w_gate[0].astype(bf16), w_up[0].astype(bf16), w_down[0].astype(bf16),
        final_norm_g.reshape(1, d))
    return out.reshape(b, s, d)
```

```python
import functools
import math

import numpy as np
import jax
import jax.numpy as jnp
from jax import lax
from jax.experimental import pallas as pl
from jax.experimental.pallas import tpu as pltpu

D_MODEL = 1024
HEAD_DIM = 64
FOX_HEADS = 8
FOX_WIDTH = FOX_HEADS * HEAD_DIM
DIFF_HEADS = 4
DIFF_V_DIM = 2 * HEAD_DIM
DIFF_WIDTH = DIFF_HEADS * DIFF_V_DIM
D_FF = ((8 * D_MODEL + 767) // 768) * 256
EPS = 1e-6
NEG_INF = -1e30
LOG2E = math.log2(math.e)
Q_SCALE = LOG2E / math.sqrt(HEAD_DIM)
UNDERFLOW_LOG2 = 150.0
NORM_SLACK = 1.01
LAMBDA_INIT = 0.8 - 0.6 * math.exp(-0.3 * 0)

LANES = 128
GATE_PAD = LANES
QKV_WIDTH = 3 * FOX_WIDTH + 3 * DIFF_WIDTH
VMEM_LIMIT = 56 * 1024 * 1024

TM_IN = 512
TQ = 512
TK = 512
TILES_PER_STEP = 4
TM_FFN = 512

_NT = (((1,), (1,)), ((), ()))


def _rms(x, g):
    return x * lax.rsqrt(jnp.mean(x * x, axis=-1, keepdims=True) + EPS) * g


def _split3(x):
    hi = x.astype(jnp.bfloat16)
    r1 = x - hi.astype(jnp.float32)
    mid = r1.astype(jnp.bfloat16)
    lo = (r1 - mid.astype(jnp.float32)).astype(jnp.bfloat16)
    return hi, mid, lo


def _reduce_rows(x, op):
    while x.shape[0] > 8 and x.shape[0] % 2 == 0:
        half = x.shape[0] // 2
        x = op(x[:half], x[half:])
    reduce = jnp.max if op is jnp.maximum else jnp.sum
    return reduce(x, axis=0, keepdims=True)


def _fox_aug_placement():
    p = np.zeros((3 * LANES, FOX_WIDTH), np.float32)
    for part in range(3):
        for head in range(FOX_HEADS):
            lane = 3 * (head % 2) + part
            p[part * LANES + head, (head // 2) * LANES + lane] = 1.0
    return p


def _in_proj_kernel(x_ref, g_ref, w_ref, b_ref, place_ref, hsum_ref,
                    q_ref, fk_ref, fv_ref, dk_ref, dv_ref, cend_ref, kn2_ref,
                    tri_ref, carry_ref):
    step = pl.program_id(0)
    tm = x_ref.shape[0]
    f32, bf16 = jnp.float32, jnp.bfloat16

    @pl.when(step == 0)
    def _():
        r = lax.broadcasted_iota(jnp.int32, (tm, tm), 0)
        c = lax.broadcasted_iota(jnp.int32, (tm, tm), 1)
        tri_ref[...] = jnp.where(r >= c, 1.0, 0.0).astype(bf16)
        carry_ref[...] = jnp.zeros_like(carry_ref)
        kn2_ref[...] = jnp.zeros_like(kn2_ref)

    h = _rms(x_ref[...], g_ref[...]).astype(bf16)
    y = jnp.dot(h, w_ref[...], preferred_element_type=f32)
    w5 = FOX_WIDTH
    q_ref[:, :w5] = (y[:, :w5] * Q_SCALE).astype(bf16)
    q_ref[:, w5:] = (y[:, 3 * w5:4 * w5] * Q_SCALE).astype(bf16)
    fk = y[:, w5:2 * w5].astype(bf16)
    fv = y[:, 2 * w5:3 * w5].astype(bf16)
    dk = y[:, 4 * w5:5 * w5].astype(bf16)
    dv_ref[...] = y[:, 5 * w5:QKV_WIDTH].astype(bf16)

    z = y[:, QKV_WIDTH:] + b_ref[...]
    logf = LOG2E * (jnp.minimum(z, 0.0) - jnp.log1p(jnp.exp(-jnp.abs(z))))
    cs = jnp.dot(tri_ref[...], jnp.concatenate(_split3(logf), axis=1),
                 preferred_element_type=f32)
    c = cs[:, :LANES] + cs[:, LANES:2 * LANES] + cs[:, 2 * LANES:] + carry_ref[0:1, :]
    last = c[tm - 1:tm, :]
    carry_ref[...] = jnp.broadcast_to(last, carry_ref.shape)
    cend_ref[pl.ds(step, 1), :] = last
    caug = jnp.dot(jnp.concatenate(_split3(-c), axis=1), place_ref[...],
                   preferred_element_type=f32).astype(bf16)

    lane = lax.broadcasted_iota(jnp.int32, (1, LANES), 1)
    lo_half = lane < HEAD_DIM
    pos = step * tm + lax.broadcasted_iota(jnp.int32, (tm, LANES), 0)
    pos_lanes = jnp.where(lane < 3, pos & ~(LANES - 1),
                          jnp.where(lane < 6, pos & (LANES - 1), 0)).astype(f32).astype(bf16)
    for g in range(FOX_WIDTH // LANES):
        cols = slice(g * LANES, (g + 1) * LANES)
        key = slice(2 * g * LANES, (2 * g + 1) * LANES)
        bias = slice((2 * g + 1) * LANES, (2 * g + 2) * LANES)
        fk_ref[:, key] = fk[:, cols]
        fk_ref[:, bias] = caug[:, cols]
        dk_ref[:, key] = dk[:, cols]
        dk_ref[:, bias] = pos_lanes
        fv_ref[:, key] = jnp.where(lo_half, fv[:, cols], (lane == HEAD_DIM).astype(bf16))
        fv_ref[:, bias] = jnp.where(lo_half, (lane == 0).astype(bf16), fv[:, cols])

    n2 = jnp.dot(jnp.concatenate([fk * fk, dk * dk], axis=1), hsum_ref[...],
                 preferred_element_type=f32)
    kn2_ref[...] = jnp.maximum(kn2_ref[...],
                               jnp.broadcast_to(jnp.max(n2, axis=0, keepdims=True), kn2_ref.shape))


def _in_proj(x, g, w, b, place, hsum):
    s = x.shape[0]
    n_tiles = s // TM_IN
    const = lambda a: pl.BlockSpec(a.shape, lambda i: (0, 0))
    row_block = lambda width: pl.BlockSpec((TM_IN, width), lambda i: (i, 0))
    bf16 = jnp.bfloat16
    return pl.pallas_call(
        _in_proj_kernel,
        grid=(n_tiles,),
        in_specs=[row_block(D_MODEL), const(g), const(w), const(b), const(place), const(hsum)],
        out_specs=[
            row_block(2 * FOX_WIDTH),
            row_block(2 * FOX_WIDTH),
            row_block(2 * FOX_WIDTH),
            row_block(2 * DIFF_WIDTH),
            row_block(DIFF_WIDTH),
            pl.BlockSpec((n_tiles, LANES), lambda i: (0, 0)),
            pl.BlockSpec((8, LANES), lambda i: (0, 0)),
        ],
        out_shape=[
            jax.ShapeDtypeStruct((s, 2 * FOX_WIDTH), bf16),
            jax.ShapeDtypeStruct((s, 2 * FOX_WIDTH), bf16),
            jax.ShapeDtypeStruct((s, 2 * FOX_WIDTH), bf16),
            jax.ShapeDtypeStruct((s, 2 * DIFF_WIDTH), bf16),
            jax.ShapeDtypeStruct((s, DIFF_WIDTH), bf16),
            jax.ShapeDtypeStruct((n_tiles, LANES), jnp.float32),
            jax.ShapeDtypeStruct((8, LANES), jnp.float32),
        ],
        scratch_shapes=[
            pltpu.VMEM((TM_IN, TM_IN), bf16),
            pltpu.VMEM((8, LANES), jnp.float32),
        ],
        compiler_params=pltpu.CompilerParams(
            dimension_semantics=("arbitrary",), vmem_limit_bytes=VMEM_LIMIT),
        name="in_proj",
    )(x, g, w, b, place, hsum)


def _attn_kernel(*refs, fox):
    if fox:
        q_ref, kaug_ref, vaug_ref, kn2_ref, cend_ref, o_ref, m_ref, acc_ref = refs
    else:
        (q_ref, kaug_ref, v_ref, kn2_ref, lq1_ref, lk1_ref, lq2_ref, lk2_ref, gn_ref,
         o_ref, m_ref, acc_ref, l_ref) = refs
    grp = pl.program_id(0)
    i = pl.program_id(1)
    tq, tk = TQ, TK
    n_tiles = kaug_ref.shape[0] // tk
    f32, bf16 = jnp.float32, jnp.bfloat16

    lane = lax.broadcasted_iota(jnp.int32, (1, LANES), 1)
    lo_half = lane < HEAD_DIM

    if not fox:
        slope = LOG2E * jnp.exp2(-2.0 * jnp.full((1, 1), grp + 1, jnp.int32).astype(f32))
        sl = [p.astype(f32) for p in _split3(slope)]
        inv_slope = 1.0 / slope

    def half_norms_sq(x):
        l_in = lax.broadcasted_iota(jnp.int32, (LANES, LANES), 0)
        l_out = lax.broadcasted_iota(jnp.int32, (LANES, LANES), 1)
        half_sum = ((l_in // HEAD_DIM) == l_out).astype(bf16)
        return jnp.dot(x * x, half_sum, preferred_element_type=f32)

    q = q_ref[...]
    zero = jnp.zeros_like(q)
    if fox:
        q_bias = [jnp.broadcast_to(((lane >= 3 * h) & (lane < 3 * h + 3)).astype(bf16), q.shape)
                  for h in range(2)]
    else:
        parts = jnp.where((lane == 0) | (lane == 3), sl[0],
                          jnp.where((lane == 1) | (lane == 4), sl[1],
                                    jnp.where((lane == 2) | (lane == 5), sl[2], 0.0)))
        q_bias = [jnp.broadcast_to(parts.astype(bf16), q.shape)] * 2
    q2 = jnp.concatenate(
        [jnp.concatenate([jnp.where(lo_half, q, zero), q_bias[0]], axis=1),
         jnp.concatenate([jnp.where(lo_half, zero, q), q_bias[1]], axis=1)], axis=0)

    m_ref[...] = jnp.full_like(m_ref, NEG_INF)
    acc_ref[...] = jnp.zeros_like(acc_ref)
    if not fox:
        l_ref[...] = jnp.zeros_like(l_ref)

    def tile(j, n_sub, masked):
        width = n_sub * tk
        rows = pl.ds(pl.multiple_of(j * tk, tk), width)
        s = lax.dot_general(q2, kaug_ref[rows, :], _NT, preferred_element_type=f32)
        if masked:
            r = lax.broadcasted_iota(jnp.int32, (2 * tq, width), 0) & (tq - 1)
            c = lax.broadcasted_iota(jnp.int32, (2 * tq, width), 1)
            s = jnp.where(c + (j - i) * tk <= r, s, NEG_INF)
        m_prev = m_ref[...]
        m_next = jnp.maximum(m_prev, jnp.max(s, axis=1, keepdims=True))
        alpha = jnp.exp2(m_prev - m_next)
        p = jnp.exp2(s - jnp.tile(m_next, (1, width // LANES)))
        pb = p.astype(bf16)
        if fox:
            pv = jnp.concatenate(
                [jnp.dot(pb[:tq], vaug_ref[rows, :LANES], preferred_element_type=f32),
                 jnp.dot(pb[tq:], vaug_ref[rows, LANES:], preferred_element_type=f32)], axis=0)
        else:
            pv = jnp.dot(pb, v_ref[rows, :], preferred_element_type=f32)
            part = p[:, :LANES]
            for blk in range(1, width // LANES):
                part = part + p[:, blk * LANES:(blk + 1) * LANES]
            l_ref[...] = alpha * l_ref[...] + part
        acc_ref[...] = acc_ref[...] * alpha + pv
        m_ref[...] = m_next

    top = jnp.maximum(i - 1, 0)
    qn = half_norms_sq(q)
    first_norm_lane = (0 if fox else FOX_HEADS) + 2 * grp
    kmax = [jnp.sqrt(jnp.sum(jnp.where(lane == first_norm_lane + h, kn2_ref[0:1, :], 0.0),
                             axis=1, keepdims=True)) * NORM_SLACK for h in range(2)]
    bound = [jnp.sqrt(qn[:, h:h + 1]) * kmax[h] for h in range(2)]
    tile(top, 2, masked=True)

    m_col = [m_ref[h * tq:(h + 1) * tq, 0:1] for h in range(2)]
    if fox:
        head_lane = lax.broadcasted_iota(jnp.int32, cend_ref.shape, 1) - 2 * grp
        cend = [jnp.sum(jnp.where(head_lane == h, cend_ref[...], 0.0), axis=1, keepdims=True)
                for h in range(2)]
        below_top = lax.broadcasted_iota(jnp.int32, (n_tiles, 1), 0) < top
        dead = below_top
        for h in range(2):
            gap = _reduce_rows(bound[h] - m_col[h], jnp.maximum)
            dead = dead & (-cend[h] < -UNDERFLOW_LOG2 - gap)
        j_lo = _reduce_rows(dead.astype(f32), jnp.add).astype(jnp.int32)[0, 0]
    else:
        gap = _reduce_rows(jnp.maximum(bound[0] - m_col[0], bound[1] - m_col[1]), jnp.maximum)
        x = (-UNDERFLOW_LOG2 - gap) * inv_slope
        j_lo = jnp.clip(jnp.floor(x * (1.0 / tk)), 0.0, 1.0 * n_tiles)
        j_lo = jnp.minimum(j_lo.astype(jnp.int32)[0, 0], top)

    n_live = top - j_lo
    one = n_live & 1
    two = n_live & 2

    @pl.when(one == 1)
    def _():
        tile(top - 1, 1, masked=False)

    @pl.when(two == 2)
    def _():
        tile(top - one - 2, 2, masked=False)

    def body(step, carry):
        tile(top - one - two - TILES_PER_STEP * (step + 1), TILES_PER_STEP, masked=False)
        return carry

    lax.fori_loop(0, n_live // TILES_PER_STEP, body, 0)

    acc = [acc_ref[h * tq:(h + 1) * tq, :] for h in range(2)]
    if fox:
        out = jnp.where(lo_half, acc[0] / acc[0][:, HEAD_DIM:HEAD_DIM + 1],
                        acc[1] / acc[1][:, 0:1])
    else:
        lam = (jnp.exp(jnp.sum(lq1_ref[...] * lk1_ref[...], axis=1, keepdims=True))
               - jnp.exp(jnp.sum(lq2_ref[...] * lk2_ref[...], axis=1, keepdims=True))
               + LAMBDA_INIT)
        l = jnp.sum(l_ref[...], axis=1, keepdims=True)
        a = acc[0] / l[:tq] - lam * (acc[1] / l[tq:])
        out = _rms(a, gn_ref[...]) * (1.0 - LAMBDA_INIT)
    o_ref[...] = out.astype(o_ref.dtype)


def _attention(q, kaug, v, kn2, extra, *, fox):
    s = q.shape[0]
    groups = FOX_HEADS // 2 if fox else DIFF_HEADS
    q_col0 = 0 if fox else FOX_WIDTH // LANES
    resident = lambda width: pl.BlockSpec((s, width), lambda g, i: (0, g),
                                          pipeline_mode=pl.Buffered(1))
    small = lambda a: pl.BlockSpec(a.shape, lambda g, i: (0, 0))
    in_specs = [
        pl.BlockSpec((TQ, LANES), lambda g, i: (i, q_col0 + g)),
        resident(2 * LANES),
        resident(2 * LANES if fox else LANES),
        small(kn2),
    ] + [small(a) for a in extra]
    scratch = [
        pltpu.VMEM((2 * TQ, LANES), jnp.float32),
        pltpu.VMEM((2 * TQ, LANES), jnp.float32),
    ]
    if not fox:
        scratch.append(pltpu.VMEM((2 * TQ, LANES), jnp.float32))
    return pl.pallas_call(
        functools.partial(_attn_kernel, fox=fox),
        grid=(groups, s // TQ),
        in_specs=in_specs,
        out_specs=pl.BlockSpec((TQ, LANES), lambda g, i: (i, g)),
        out_shape=jax.ShapeDtypeStruct((s, groups * LANES), jnp.bfloat16),
        scratch_shapes=scratch,
        compiler_params=pltpu.CompilerParams(
            dimension_semantics=("arbitrary", "arbitrary"), vmem_limit_bytes=VMEM_LIMIT),
        name="fox_attn" if fox else "diff_attn",
    )(q, kaug, v, kn2, *extra)


def _out_ffn_kernel(x_ref, fo_ref, do_ref, wo_ref, gf_ref, wg_ref, wu_ref, wd_ref,
                    gl_ref, o_ref):
    x1 = (x_ref[...]
          + jnp.dot(fo_ref[...], wo_ref[:FOX_WIDTH, :], preferred_element_type=jnp.float32)
          + jnp.dot(do_ref[...], wo_ref[FOX_WIDTH:, :], preferred_element_type=jnp.float32))
    h = _rms(x1, gf_ref[...]).astype(jnp.bfloat16)
    g = jnp.dot(h, wg_ref[...], preferred_element_type=jnp.float32)
    u = jnp.dot(h, wu_ref[...], preferred_element_type=jnp.float32)
    a = (g * jax.nn.sigmoid(g) * u).astype(jnp.bfloat16)
    x2 = x1 + jnp.dot(a, wd_ref[...], preferred_element_type=jnp.float32)
    o_ref[...] = _rms(x2, gl_ref[...])


def _out_ffn(x, fo, do, wo, gf, wg, wu, wd, gl):
    s = x.shape[0]
    const = lambda shape: pl.BlockSpec(shape, lambda i: (0, 0), pipeline_mode=pl.Buffered(1))
    return pl.pallas_call(
        _out_ffn_kernel,
        grid=(s // TM_FFN,),
        in_specs=[
            pl.BlockSpec((TM_FFN, D_MODEL), lambda i: (i, 0)),
            pl.BlockSpec((TM_FFN, FOX_WIDTH), lambda i: (i, 0)),
            pl.BlockSpec((TM_FFN, DIFF_WIDTH), lambda i: (i, 0)),
            const(wo.shape), const(gf.shape), const(wg.shape), const(wu.shape),
            const(wd.shape), const(gl.shape),
        ],
        out_specs=pl.BlockSpec((TM_FFN, D_MODEL), lambda i: (i, 0)),
        out_shape=jax.ShapeDtypeStruct((s, D_MODEL), jnp.float32),
        compiler_params=pltpu.CompilerParams(
            dimension_semantics=("arbitrary",), vmem_limit_bytes=VMEM_LIMIT),
        name="out_ffn",
    )(x, fo, do, wo, gf, wg, wu, wd, gl)


def kernel(x, mix_norm_g, w_in, b_forget, lambda_q1, lambda_k1, lambda_q2, lambda_k2,
           diff_norm_g, w_out, ffn_norm_g, w_gate, w_up, w_down, final_norm_g):
    b, s, d = x.shape
    assert b == 1 and d == D_MODEL and w_in.shape[0] == 1
    assert s % TQ == 0 and s % TM_IN == 0 and s % TM_FFN == 0
    assert TM_IN == TK and TQ == TK and TILES_PER_STEP == 4 and s >= 2 * TK
    assert s < 2 ** 14 + 1
    bf16 = jnp.bfloat16
    x2d = x.reshape(s, d)

    w = w_in[0]
    o_gate = 3 * FOX_WIDTH
    w_r = jnp.concatenate(
        [w[:, :o_gate], w[:, o_gate + FOX_HEADS:], w[:, o_gate:o_gate + FOX_HEADS],
         jnp.zeros((d, GATE_PAD - FOX_HEADS), w.dtype)], axis=1).astype(bf16)
    b_pad = jnp.pad(b_forget[0].astype(jnp.float32), (0, GATE_PAD - FOX_HEADS)).reshape(1, GATE_PAD)
    place = jnp.asarray(_fox_aug_placement(), bf16)
    hsum = jnp.asarray(np.arange(2 * FOX_WIDTH)[:, None] // HEAD_DIM == np.arange(LANES)[None, :],
                       bf16)

    q, fox_k, fox_v, diff_k, diff_v, cend, kn2 = _in_proj(
        x2d, mix_norm_g[0].reshape(1, d), w_r, b_pad, place, hsum)

    fox_out = _attention(q, fox_k, fox_v, kn2, (cend,), fox=True)
    lam_args = tuple(a[0].astype(jnp.float32).reshape(1, HEAD_DIM)
                     for a in (lambda_q1, lambda_k1, lambda_q2, lambda_k2))
    diff_out = _attention(
        q, diff_k, diff_v, kn2,
        lam_args + (diff_norm_g[0].astype(jnp.float32).reshape(1, DIFF_V_DIM),), fox=False)

    out = _out_ffn(
        x2d, fox_out, diff_out, w_out[0].astype(bf16), ffn_norm_g[0].reshape(1, d),
        w_gate[0].astype(bf16), w_up[0].astype(bf16), w_down[0].astype(bf16),
        final_norm_g.reshape(1, d))
    return out.reshape(b, s, d)
```

```python
import functools
import math

import numpy as np
import jax
import jax.numpy as jnp
from jax import lax
from jax.experimental import pallas as pl
from jax.experimental.pallas import tpu as pltpu

D_MODEL = 1024
HEAD_DIM = 64
FOX_HEADS = 8
FOX_WIDTH = FOX_HEADS * HEAD_DIM
DIFF_HEADS = 4
DIFF_V_DIM = 2 * HEAD_DIM
DIFF_WIDTH = DIFF_HEADS * DIFF_V_DIM
D_FF = ((8 * D_MODEL + 767) // 768) * 256
EPS = 1e-6
NEG_INF = -1e30
LOG2E = math.log2(math.e)
Q_SCALE = LOG2E / math.sqrt(HEAD_DIM)
UNDERFLOW_LOG2 = 150.0
NORM_SLACK = 1.01
LAMBDA_INIT = 0.8 - 0.6 * math.exp(-0.3 * 0)

LANES = 128
GATE_PAD = LANES
QKV_WIDTH = 3 * FOX_WIDTH + 3 * DIFF_WIDTH
VMEM_LIMIT = 56 * 1024 * 1024

TM_IN = 512
TQ = 512
TK = 512
TILES_PER_STEP = 4
TM_FFN = 512

_NT = (((1,), (1,)), ((), ()))


def _rms(x, g):
    return x * lax.rsqrt(jnp.mean(x * x, axis=-1, keepdims=True) + EPS) * g


def _split3(x):
    hi = x.astype(jnp.bfloat16)
    r1 = x - hi.astype(jnp.float32)
    mid = r1.astype(jnp.bfloat16)
    lo = (r1 - mid.astype(jnp.float32)).astype(jnp.bfloat16)
    return hi, mid, lo


def _reduce_rows(x, op):
    while x.shape[0] > 8 and x.shape[0] % 2 == 0:
        half = x.shape[0] // 2
        x = op(x[:half], x[half:])
    reduce = jnp.max if op is jnp.maximum else jnp.sum
    return reduce(x, axis=0, keepdims=True)


def _fox_aug_placement():
    p = np.zeros((LANES, FOX_WIDTH), np.float32)
    for part in range(3):
        for head in range(FOX_HEADS):
            lane = 3 * (head % 2) + part
            p[part * FOX_HEADS + head, (head // 2) * LANES + lane] = 1.0
    return p


def _in_proj_kernel(x_ref, g_ref, w_ref, b_ref, place_ref, hsum_ref,
                    q_ref, fk_ref, fv_ref, dk_ref, dv_ref, cend_ref, kn2_ref,
                    tri_ref, carry_ref):
    step = pl.program_id(0)
    tm = x_ref.shape[0]
    f32, bf16 = jnp.float32, jnp.bfloat16

    @pl.when(step == 0)
    def _():
        r = lax.broadcasted_iota(jnp.int32, (tm, tm), 0)
        c = lax.broadcasted_iota(jnp.int32, (tm, tm), 1)
        tri_ref[...] = jnp.where(r >= c, 1.0, 0.0).astype(bf16)
        carry_ref[...] = jnp.zeros_like(carry_ref)
        kn2_ref[...] = jnp.zeros_like(kn2_ref)

    lane = lax.broadcasted_iota(jnp.int32, (1, LANES), 1)
    lo_half = lane < HEAD_DIM
    head_lanes = lane < FOX_HEADS
    w5 = FOX_WIDTH

    def pack3(x):
        hi, mid, lo = (p.astype(f32) for p in _split3(jnp.where(head_lanes, x, 0.0)))
        return (hi + pltpu.roll(mid, FOX_HEADS, axis=1)
                + pltpu.roll(lo, 2 * FOX_HEADS, axis=1)).astype(bf16)

    h = _rms(x_ref[...], g_ref[...]).astype(bf16)

    z = jnp.dot(h, w_ref[:, QKV_WIDTH:], preferred_element_type=f32) + b_ref[...]
    logf = LOG2E * (jnp.minimum(z, 0.0) - jnp.log1p(jnp.exp(-jnp.abs(z))))
    cs = jnp.dot(tri_ref[...], pack3(logf), preferred_element_type=f32)
    c = (cs + pltpu.roll(cs, LANES - FOX_HEADS, axis=1)
         + pltpu.roll(cs, LANES - 2 * FOX_HEADS, axis=1) + carry_ref[0:1, :])
    last = c[tm - 1:tm, :]
    carry_ref[...] = jnp.broadcast_to(last, carry_ref.shape)
    cend_ref[pl.ds(step, 1), :] = last
    caug = jnp.dot(pack3(-c), place_ref[...], preferred_element_type=f32).astype(bf16)

    yk = jnp.dot(h, w_ref[:, :2 * w5], preferred_element_type=f32)
    fk = yk[:, :w5].astype(bf16)
    dk = yk[:, w5:].astype(bf16)
    n2 = jnp.dot(jnp.concatenate([fk * fk, dk * dk], axis=1), hsum_ref[...],
                 preferred_element_type=f32)
    kn2_ref[...] = jnp.maximum(kn2_ref[...],
                               jnp.broadcast_to(jnp.max(n2, axis=0, keepdims=True), kn2_ref.shape))

    y = jnp.dot(h, w_ref[:, 2 * w5:QKV_WIDTH], preferred_element_type=f32)
    q_ref[...] = (y[:, :2 * w5] * Q_SCALE).astype(bf16)
    fv = y[:, 2 * w5:3 * w5].astype(bf16)
    dv_ref[...] = y[:, 3 * w5:].astype(bf16)

    pos = step * tm + lax.broadcasted_iota(jnp.int32, (tm, LANES), 0)
    pos_lanes = jnp.where(lane < 3, pos & ~(LANES - 1),
                          jnp.where(lane < 6, pos & (LANES - 1), 0)).astype(f32).astype(bf16)
    for g in range(FOX_WIDTH // LANES):
        cols = slice(g * LANES, (g + 1) * LANES)
        key = slice(2 * g * LANES, (2 * g + 1) * LANES)
        bias = slice((2 * g + 1) * LANES, (2 * g + 2) * LANES)
        fk_ref[:, key] = fk[:, cols]
        fk_ref[:, bias] = caug[:, cols]
        dk_ref[:, key] = dk[:, cols]
        dk_ref[:, bias] = pos_lanes
        fv_ref[:, key] = jnp.where(lo_half, fv[:, cols], (lane == HEAD_DIM).astype(bf16))
        fv_ref[:, bias] = jnp.where(lo_half, (lane == 0).astype(bf16), fv[:, cols])


def _in_proj(x, g, w, b, place, hsum):
    s = x.shape[0]
    n_tiles = s // TM_IN
    const = lambda a: pl.BlockSpec(a.shape, lambda i: (0, 0))
    row_block = lambda width: pl.BlockSpec((TM_IN, width), lambda i: (i, 0))
    bf16 = jnp.bfloat16
    return pl.pallas_call(
        _in_proj_kernel,
        grid=(n_tiles,),
        in_specs=[row_block(D_MODEL), const(g), const(w), const(b), const(place), const(hsum)],
        out_specs=[
            row_block(2 * FOX_WIDTH),
            row_block(2 * FOX_WIDTH),
            row_block(2 * FOX_WIDTH),
            row_block(2 * DIFF_WIDTH),
            row_block(DIFF_WIDTH),
            pl.BlockSpec((n_tiles, LANES), lambda i: (0, 0)),
            pl.BlockSpec((8, LANES), lambda i: (0, 0)),
        ],
        out_shape=[
            jax.ShapeDtypeStruct((s, 2 * FOX_WIDTH), bf16),
            jax.ShapeDtypeStruct((s, 2 * FOX_WIDTH), bf16),
            jax.ShapeDtypeStruct((s, 2 * FOX_WIDTH), bf16),
            jax.ShapeDtypeStruct((s, 2 * DIFF_WIDTH), bf16),
            jax.ShapeDtypeStruct((s, DIFF_WIDTH), bf16),
            jax.ShapeDtypeStruct((n_tiles, LANES), jnp.float32),
            jax.ShapeDtypeStruct((8, LANES), jnp.float32),
        ],
        scratch_shapes=[
            pltpu.VMEM((TM_IN, TM_IN), bf16),
            pltpu.VMEM((8, LANES), jnp.float32),
        ],
        compiler_params=pltpu.CompilerParams(
            dimension_semantics=("arbitrary",), vmem_limit_bytes=VMEM_LIMIT),
        name="in_proj",
    )(x, g, w, b, place, hsum)


def _attn_kernel(*refs, fox):
    if fox:
        q_ref, kaug_ref, vaug_ref, kn2_ref, cend_ref, o_ref, m_ref, acc_ref = refs
    else:
        (q_ref, kaug_ref, v_ref, kn2_ref, lq1_ref, lk1_ref, lq2_ref, lk2_ref, gn_ref,
         o_ref, m_ref, acc_ref, l_ref) = refs
    grp = pl.program_id(0)
    i = pl.program_id(1)
    tq, tk = TQ, TK
    n_tiles = kaug_ref.shape[0] // tk
    f32, bf16 = jnp.float32, jnp.bfloat16

    lane = lax.broadcasted_iota(jnp.int32, (1, LANES), 1)
    lo_half = lane < HEAD_DIM

    if not fox:
        slope = LOG2E * jnp.exp2(-2.0 * jnp.full((1, 1), grp + 1, jnp.int32).astype(f32))
        sl = [p.astype(f32) for p in _split3(slope)]
        inv_slope = 1.0 / slope

    def half_norms_sq(x):
        l_in = lax.broadcasted_iota(jnp.int32, (LANES, LANES), 0)
        l_out = lax.broadcasted_iota(jnp.int32, (LANES, LANES), 1)
        half_sum = ((l_in // HEAD_DIM) == l_out).astype(bf16)
        return jnp.dot(x * x, half_sum, preferred_element_type=f32)

    q = q_ref[...]
    zero = jnp.zeros_like(q)
    if fox:
        q_bias = [jnp.broadcast_to(((lane >= 3 * h) & (lane < 3 * h + 3)).astype(bf16), q.shape)
                  for h in range(2)]
    else:
        parts = jnp.where((lane == 0) | (lane == 3), sl[0],
                          jnp.where((lane == 1) | (lane == 4), sl[1],
                                    jnp.where((lane == 2) | (lane == 5), sl[2], 0.0)))
        q_bias = [jnp.broadcast_to(parts.astype(bf16), q.shape)] * 2
    q2 = jnp.concatenate(
        [jnp.concatenate([jnp.where(lo_half, q, zero), q_bias[0]], axis=1),
         jnp.concatenate([jnp.where(lo_half, zero, q), q_bias[1]], axis=1)], axis=0)

    m_ref[...] = jnp.full_like(m_ref, NEG_INF)
    acc_ref[...] = jnp.zeros_like(acc_ref)
    if not fox:
        l_ref[...] = jnp.zeros_like(l_ref)

    def tile(j, n_sub, masked):
        width = n_sub * tk
        rows = pl.ds(pl.multiple_of(j * tk, tk), width)
        s = lax.dot_general(q2, kaug_ref[rows, :], _NT, preferred_element_type=f32)
        if masked:
            r = lax.broadcasted_iota(jnp.int32, (2 * tq, width), 0) & (tq - 1)
            c = lax.broadcasted_iota(jnp.int32, (2 * tq, width), 1)
            s = jnp.where(c + (j - i) * tk <= r, s, NEG_INF)
        m_prev = m_ref[...]
        m_next = jnp.maximum(m_prev, jnp.max(s, axis=1, keepdims=True))
        alpha = jnp.exp2(m_prev - m_next)
        p = jnp.exp2(s - jnp.tile(m_next, (1, width // LANES)))
        pb = p.astype(bf16)
        if fox:
            pv = jnp.concatenate(
                [jnp.dot(pb[:tq], vaug_ref[rows, :LANES], preferred_element_type=f32),
                 jnp.dot(pb[tq:], vaug_ref[rows, LANES:], preferred_element_type=f32)], axis=0)
        else:
            pv = jnp.dot(pb, v_ref[rows, :], preferred_element_type=f32)
            part = p[:, :LANES]
            for blk in range(1, width // LANES):
                part = part + p[:, blk * LANES:(blk + 1) * LANES]
            l_ref[...] = alpha * l_ref[...] + part
        acc_ref[...] = acc_ref[...] * alpha + pv
        m_ref[...] = m_next

    top = jnp.maximum(i - 1, 0)
    qn = half_norms_sq(q)
    first_norm_lane = (0 if fox else FOX_HEADS) + 2 * grp
    kmax = [jnp.sqrt(jnp.sum(jnp.where(lane == first_norm_lane + h, kn2_ref[0:1, :], 0.0),
                             axis=1, keepdims=True)) * NORM_SLACK for h in range(2)]
    bound = [jnp.sqrt(qn[:, h:h + 1]) * kmax[h] for h in range(2)]
    tile(top, 2, masked=True)

    m_col = [m_ref[h * tq:(h + 1) * tq, 0:1] for h in range(2)]
    if fox:
        head_lane = lax.broadcasted_iota(jnp.int32, cend_ref.shape, 1) - 2 * grp
        cend = [jnp.sum(jnp.where(head_lane == h, cend_ref[...], 0.0), axis=1, keepdims=True)
                for h in range(2)]
        below_top = lax.broadcasted_iota(jnp.int32, (n_tiles, 1), 0) < top
        dead = below_top
        for h in range(2):
            gap = _reduce_rows(bound[h] - m_col[h], jnp.maximum)
            dead = dead & (-cend[h] < -UNDERFLOW_LOG2 - gap)
        j_lo = _reduce_rows(dead.astype(f32), jnp.add).astype(jnp.int32)[0, 0]
    else:
        gap = _reduce_rows(jnp.maximum(bound[0] - m_col[0], bound[1] - m_col[1]), jnp.maximum)
        x = (-UNDERFLOW_LOG2 - gap) * inv_slope
        j_lo = jnp.clip(jnp.floor(x * (1.0 / tk)), 0.0, 1.0 * n_tiles)
        j_lo = jnp.minimum(j_lo.astype(jnp.int32)[0, 0], top)

    n_live = top - j_lo
    one = n_live & 1
    two = n_live & 2

    @pl.when(one == 1)
    def _():
        tile(top - 1, 1, masked=False)

    @pl.when(two == 2)
    def _():
        tile(top - one - 2, 2, masked=False)

    def body(step, carry):
        tile(top - one - two - TILES_PER_STEP * (step + 1), TILES_PER_STEP, masked=False)
        return carry

    lax.fori_loop(0, n_live // TILES_PER_STEP, body, 0)

    acc = [acc_ref[h * tq:(h + 1) * tq, :] for h in range(2)]
    if fox:
        out = jnp.where(lo_half, acc[0] / acc[0][:, HEAD_DIM:HEAD_DIM + 1],
                        acc[1] / acc[1][:, 0:1])
    else:
        lam = (jnp.exp(jnp.sum(lq1_ref[...] * lk1_ref[...], axis=1, keepdims=True))
               - jnp.exp(jnp.sum(lq2_ref[...] * lk2_ref[...], axis=1, keepdims=True))
               + LAMBDA_INIT)
        l = jnp.sum(l_ref[...], axis=1, keepdims=True)
        a = acc[0] / l[:tq] - lam * (acc[1] / l[tq:])
        out = _rms(a, gn_ref[...]) * (1.0 - LAMBDA_INIT)
    o_ref[...] = out.astype(o_ref.dtype)


def _attention(q, kaug, v, kn2, extra, *, fox):
    s = q.shape[0]
    groups = FOX_HEADS // 2 if fox else DIFF_HEADS
    q_col0 = 0 if fox else FOX_WIDTH // LANES
    resident = lambda width: pl.BlockSpec((s, width), lambda g, i: (0, g),
                                          pipeline_mode=pl.Buffered(1))
    small = lambda a: pl.BlockSpec(a.shape, lambda g, i: (0, 0))
    in_specs = [
        pl.BlockSpec((TQ, LANES), lambda g, i: (i, q_col0 + g)),
        resident(2 * LANES),
        resident(2 * LANES if fox else LANES),
        small(kn2),
    ] + [small(a) for a in extra]
    scratch = [
        pltpu.VMEM((2 * TQ, LANES), jnp.float32),
        pltpu.VMEM((2 * TQ, LANES), jnp.float32),
    ]
    if not fox:
        scratch.append(pltpu.VMEM((2 * TQ, LANES), jnp.float32))
    return pl.pallas_call(
        functools.partial(_attn_kernel, fox=fox),
        grid=(groups, s // TQ),
        in_specs=in_specs,
        out_specs=pl.BlockSpec((TQ, LANES), lambda g, i: (i, g)),
        out_shape=jax.ShapeDtypeStruct((s, groups * LANES), jnp.bfloat16),
        scratch_shapes=scratch,
        compiler_params=pltpu.CompilerParams(
            dimension_semantics=("arbitrary", "arbitrary"), vmem_limit_bytes=VMEM_LIMIT),
        name="fox_attn" if fox else "diff_attn",
    )(q, kaug, v, kn2, *extra)


def _out_ffn_kernel(x_ref, fo_ref, do_ref, wo_ref, gf_ref, wg_ref, wu_ref, wd_ref,
                    gl_ref, o_ref):
    x1 = (x_ref[...]
          + jnp.dot(fo_ref[...], wo_ref[:FOX_WIDTH, :], preferred_element_type=jnp.float32)
          + jnp.dot(do_ref[...], wo_ref[FOX_WIDTH:, :], preferred_element_type=jnp.float32))
    h = _rms(x1, gf_ref[...]).astype(jnp.bfloat16)
    g = jnp.dot(h, wg_ref[...], preferred_element_type=jnp.float32)
    u = jnp.dot(h, wu_ref[...], preferred_element_type=jnp.float32)
    a = (g * jax.nn.sigmoid(g) * u).astype(jnp.bfloat16)
    x2 = x1 + jnp.dot(a, wd_ref[...], preferred_element_type=jnp.float32)
    o_ref[...] = _rms(x2, gl_ref[...])


def _out_ffn(x, fo, do, wo, gf, wg, wu, wd, gl):
    s = x.shape[0]
    const = lambda shape: pl.BlockSpec(shape, lambda i: (0, 0), pipeline_mode=pl.Buffered(1))
    return pl.pallas_call(
        _out_ffn_kernel,
        grid=(s // TM_FFN,),
        in_specs=[
            pl.BlockSpec((TM_FFN, D_MODEL), lambda i: (i, 0)),
            pl.BlockSpec((TM_FFN, FOX_WIDTH), lambda i: (i, 0)),
            pl.BlockSpec((TM_FFN, DIFF_WIDTH), lambda i: (i, 0)),
            const(wo.shape), const(gf.shape), const(wg.shape), const(wu.shape),
            const(wd.shape), const(gl.shape),
        ],
        out_specs=pl.BlockSpec((TM_FFN, D_MODEL), lambda i: (i, 0)),
        out_shape=jax.ShapeDtypeStruct((s, D_MODEL), jnp.float32),
        compiler_params=pltpu.CompilerParams(
            dimension_semantics=("arbitrary",), vmem_limit_bytes=VMEM_LIMIT),
        name="out_ffn",
    )(x, fo, do, wo, gf, wg, wu, wd, gl)


def kernel(x, mix_norm_g, w_in, b_forget, lambda_q1, lambda_k1, lambda_q2, lambda_k2,
           diff_norm_g, w_out, ffn_norm_g, w_gate, w_up, w_down, final_norm_g):
    b, s, d = x.shape
    assert b == 1 and d == D_MODEL and w_in.shape[0] == 1
    assert s % TQ == 0 and s % TM_IN == 0 and s % TM_FFN == 0
    assert TM_IN == TK and TQ == TK and TILES_PER_STEP == 4 and s >= 2 * TK
    assert s < 2 ** 14 + 1
    bf16 = jnp.bfloat16
    x2d = x.reshape(s, d)

    w = w_in[0]
    col = np.cumsum((0, FOX_WIDTH, FOX_WIDTH, FOX_WIDTH, FOX_HEADS, DIFF_WIDTH, DIFF_WIDTH,
                     DIFF_WIDTH))
    fq, fk, fv, gate, dq, dk, dv = (w[:, col[n]:col[n + 1]] for n in range(7))
    w_r = jnp.concatenate(
        [fk, dk, fq, dq, fv, dv, gate, jnp.zeros((d, GATE_PAD - FOX_HEADS), w.dtype)],
        axis=1).astype(bf16)
    b_pad = jnp.pad(b_forget[0].astype(jnp.float32), (0, GATE_PAD - FOX_HEADS)).reshape(1, GATE_PAD)
    place = jnp.asarray(_fox_aug_placement(), bf16)
    hsum = jnp.asarray(np.arange(2 * FOX_WIDTH)[:, None] // HEAD_DIM == np.arange(LANES)[None, :],
                       bf16)

    q, fox_k, fox_v, diff_k, diff_v, cend, kn2 = _in_proj(
        x2d, mix_norm_g[0].reshape(1, d), w_r, b_pad, place, hsum)

    fox_out = _attention(q, fox_k, fox_v, kn2, (cend,), fox=True)
    lam_args = tuple(a[0].astype(jnp.float32).reshape(1, HEAD_DIM)
                     for a in (lambda_q1, lambda_k1, lambda_q2, lambda_k2))
    diff_out = _attention(
        q, diff_k, diff_v, kn2,
        lam_args + (diff_norm_g[0].astype(jnp.float32).reshape(1, DIFF_V_DIM),), fox=False)

    out = _out_ffn(
        x2d, fox_out, diff_out, w_out[0].astype(bf16), ffn_norm_g[0].reshape(1, d),
        w_gate[0].astype(bf16), w_up[0].astype(bf16), w_down[0].astype(bf16),
        final_norm_g.reshape(1, d))
    return out.reshape(b, s, d)
```

```python
import functools
import math

import numpy as np
import jax
import jax.numpy as jnp
from jax import lax
from jax.experimental import pallas as pl
from jax.experimental.pallas import tpu as pltpu

D_MODEL = 1024
HEAD_DIM = 64
FOX_HEADS = 8
FOX_WIDTH = FOX_HEADS * HEAD_DIM
DIFF_HEADS = 4
DIFF_V_DIM = 2 * HEAD_DIM
DIFF_WIDTH = DIFF_HEADS * DIFF_V_DIM
D_FF = ((8 * D_MODEL + 767) // 768) * 256
EPS = 1e-6
NEG_INF = -1e30
LOG2E = math.log2(math.e)
Q_SCALE = LOG2E / math.sqrt(HEAD_DIM)
UNDERFLOW_LOG2 = 127.0
NORM_SLACK = 1.01
LAMBDA_INIT = 0.8 - 0.6 * math.exp(-0.3 * 0)

LANES = 128
GATE_PAD = LANES
QKV_WIDTH = 3 * FOX_WIDTH + 3 * DIFF_WIDTH
VMEM_LIMIT = 56 * 1024 * 1024

TM_IN = 512
TQ = 512
TK = 512
TILES_PER_STEP = 4
TM_FFN = 512

_NT = (((1,), (1,)), ((), ()))


def _rms(x, g):
    return x * lax.rsqrt(jnp.mean(x * x, axis=-1, keepdims=True) + EPS) * g


def _split3(x):
    hi = x.astype(jnp.bfloat16)
    r1 = x - hi.astype(jnp.float32)
    mid = r1.astype(jnp.bfloat16)
    lo = (r1 - mid.astype(jnp.float32)).astype(jnp.bfloat16)
    return hi, mid, lo


def _reduce_rows(x, op):
    while x.shape[0] > 8 and x.shape[0] % 2 == 0:
        half = x.shape[0] // 2
        x = op(x[:half], x[half:])
    reduce = jnp.max if op is jnp.maximum else jnp.sum
    return reduce(x, axis=0, keepdims=True)


def _fox_aug_placement():
    p = np.zeros((LANES, FOX_WIDTH), np.float32)
    for part in range(3):
        for head in range(FOX_HEADS):
            lane = 3 * (head % 2) + part
            p[part * FOX_HEADS + head, (head // 2) * LANES + lane] = 1.0
    return p


def _in_proj_kernel(x_ref, g_ref, w_ref, b_ref, place_ref, hsum_ref,
                    q_ref, fk_ref, fv_ref, dk_ref, dv_ref, cend_ref, kn2_ref,
                    tri_ref, carry_ref):
    step = pl.program_id(0)
    tm = x_ref.shape[0]
    f32, bf16 = jnp.float32, jnp.bfloat16

    @pl.when(step == 0)
    def _():
        r = lax.broadcasted_iota(jnp.int32, (tm, tm), 0)
        c = lax.broadcasted_iota(jnp.int32, (tm, tm), 1)
        tri_ref[...] = jnp.where(r >= c, 1.0, 0.0).astype(bf16)
        carry_ref[...] = jnp.zeros_like(carry_ref)
        kn2_ref[...] = jnp.zeros_like(kn2_ref)

    lane = lax.broadcasted_iota(jnp.int32, (1, LANES), 1)
    lo_half = lane < HEAD_DIM
    head_lanes = lane < FOX_HEADS
    w5 = FOX_WIDTH

    def pack3(x):
        hi, mid, lo = (p.astype(f32) for p in _split3(jnp.where(head_lanes, x, 0.0)))
        return (hi + pltpu.roll(mid, FOX_HEADS, axis=1)
                + pltpu.roll(lo, 2 * FOX_HEADS, axis=1)).astype(bf16)

    h = _rms(x_ref[...], g_ref[...]).astype(bf16)

    z = jnp.dot(h, w_ref[:, QKV_WIDTH:], preferred_element_type=f32) + b_ref[...]
    logf = LOG2E * (jnp.minimum(z, 0.0) - jnp.log1p(jnp.exp(-jnp.abs(z))))
    cs = jnp.dot(tri_ref[...], pack3(logf), preferred_element_type=f32)
    c = (cs + pltpu.roll(cs, LANES - FOX_HEADS, axis=1)
         + pltpu.roll(cs, LANES - 2 * FOX_HEADS, axis=1) + carry_ref[0:1, :])
    last = c[tm - 1:tm, :]
    carry_ref[...] = jnp.broadcast_to(last, carry_ref.shape)
    cend_ref[pl.ds(step, 1), :] = last
    caug = jnp.dot(pack3(-c), place_ref[...], preferred_element_type=f32).astype(bf16)

    yk = jnp.dot(h, w_ref[:, :2 * w5], preferred_element_type=f32)
    fk = yk[:, :w5].astype(bf16)
    dk = yk[:, w5:].astype(bf16)
    n2 = jnp.dot(jnp.concatenate([fk * fk, dk * dk], axis=1), hsum_ref[...],
                 preferred_element_type=f32)
    kn2_ref[...] = jnp.maximum(kn2_ref[...],
                               jnp.broadcast_to(jnp.max(n2, axis=0, keepdims=True), kn2_ref.shape))

    y = jnp.dot(h, w_ref[:, 2 * w5:QKV_WIDTH], preferred_element_type=f32)
    q_ref[...] = (y[:, :2 * w5] * Q_SCALE).astype(bf16)
    fv = y[:, 2 * w5:3 * w5].astype(bf16)
    dv_ref[...] = y[:, 3 * w5:].astype(bf16)

    pos = step * tm + lax.broadcasted_iota(jnp.int32, (tm, LANES), 0)
    pos_lanes = jnp.where(lane < 3, pos & ~(LANES - 1),
                          jnp.where(lane < 6, pos & (LANES - 1), 0)).astype(f32).astype(bf16)
    for g in range(FOX_WIDTH // LANES):
        cols = slice(g * LANES, (g + 1) * LANES)
        key = slice(2 * g * LANES, (2 * g + 1) * LANES)
        bias = slice((2 * g + 1) * LANES, (2 * g + 2) * LANES)
        fk_ref[:, key] = fk[:, cols]
        fk_ref[:, bias] = caug[:, cols]
        dk_ref[:, key] = dk[:, cols]
        dk_ref[:, bias] = pos_lanes
        fv_ref[:, key] = jnp.where(lo_half, fv[:, cols], (lane == HEAD_DIM).astype(bf16))
        fv_ref[:, bias] = jnp.where(lo_half, (lane == 0).astype(bf16), fv[:, cols])


def _in_proj(x, g, w, b, place, hsum):
    s = x.shape[0]
    n_tiles = s // TM_IN
    const = lambda a: pl.BlockSpec(a.shape, lambda i: (0, 0))
    row_block = lambda width: pl.BlockSpec((TM_IN, width), lambda i: (i, 0))
    bf16 = jnp.bfloat16
    return pl.pallas_call(
        _in_proj_kernel,
        grid=(n_tiles,),
        in_specs=[row_block(D_MODEL), const(g), const(w), const(b), const(place), const(hsum)],
        out_specs=[
            row_block(2 * FOX_WIDTH),
            row_block(2 * FOX_WIDTH),
            row_block(2 * FOX_WIDTH),
            row_block(2 * DIFF_WIDTH),
            row_block(DIFF_WIDTH),
            pl.BlockSpec((n_tiles, LANES), lambda i: (0, 0)),
            pl.BlockSpec((8, LANES), lambda i: (0, 0)),
        ],
        out_shape=[
            jax.ShapeDtypeStruct((s, 2 * FOX_WIDTH), bf16),
            jax.ShapeDtypeStruct((s, 2 * FOX_WIDTH), bf16),
            jax.ShapeDtypeStruct((s, 2 * FOX_WIDTH), bf16),
            jax.ShapeDtypeStruct((s, 2 * DIFF_WIDTH), bf16),
            jax.ShapeDtypeStruct((s, DIFF_WIDTH), bf16),
            jax.ShapeDtypeStruct((n_tiles, LANES), jnp.float32),
            jax.ShapeDtypeStruct((8, LANES), jnp.float32),
        ],
        scratch_shapes=[
            pltpu.VMEM((TM_IN, TM_IN), bf16),
            pltpu.VMEM((8, LANES), jnp.float32),
        ],
        compiler_params=pltpu.CompilerParams(
            dimension_semantics=("arbitrary",), vmem_limit_bytes=VMEM_LIMIT),
        name="in_proj",
    )(x, g, w, b, place, hsum)


def _attn_kernel(*refs, fox):
    if fox:
        q_ref, kaug_ref, vaug_ref, kn2_ref, cend_ref, o_ref, m_ref, acc_ref = refs
    else:
        (q_ref, kaug_ref, v_ref, kn2_ref, lq1_ref, lk1_ref, lq2_ref, lk2_ref, gn_ref,
         o_ref, m_ref, acc_ref, l_ref) = refs
    grp = pl.program_id(0)
    i = pl.program_id(1)
    tq, tk = TQ, TK
    n_tiles = kaug_ref.shape[0] // tk
    f32, bf16 = jnp.float32, jnp.bfloat16

    lane = lax.broadcasted_iota(jnp.int32, (1, LANES), 1)
    lo_half = lane < HEAD_DIM

    if not fox:
        slope = LOG2E * jnp.exp2(-2.0 * jnp.full((1, 1), grp + 1, jnp.int32).astype(f32))
        sl = [p.astype(f32) for p in _split3(slope)]
        inv_slope = 1.0 / slope

    def half_norms_sq(x):
        l_in = lax.broadcasted_iota(jnp.int32, (LANES, LANES), 0)
        l_out = lax.broadcasted_iota(jnp.int32, (LANES, LANES), 1)
        half_sum = ((l_in // HEAD_DIM) == l_out).astype(bf16)
        return jnp.dot(x * x, half_sum, preferred_element_type=f32)

    q = q_ref[...]
    zero = jnp.zeros_like(q)
    if fox:
        q_bias = [jnp.broadcast_to(((lane >= 3 * h) & (lane < 3 * h + 3)).astype(bf16), q.shape)
                  for h in range(2)]
    else:
        parts = jnp.where((lane == 0) | (lane == 3), sl[0],
                          jnp.where((lane == 1) | (lane == 4), sl[1],
                                    jnp.where((lane == 2) | (lane == 5), sl[2], 0.0)))
        q_bias = [jnp.broadcast_to(parts.astype(bf16), q.shape)] * 2
    q2 = jnp.concatenate(
        [jnp.concatenate([jnp.where(lo_half, q, zero), q_bias[0]], axis=1),
         jnp.concatenate([jnp.where(lo_half, zero, q), q_bias[1]], axis=1)], axis=0)

    m_ref[...] = jnp.full_like(m_ref, NEG_INF)
    acc_ref[...] = jnp.zeros_like(acc_ref)
    if not fox:
        l_ref[...] = jnp.zeros_like(l_ref)

    def tile(j, n_sub, masked):
        width = n_sub * tk
        rows = pl.ds(pl.multiple_of(j * tk, tk), width)
        s = lax.dot_general(q2, kaug_ref[rows, :], _NT, preferred_element_type=f32)
        if masked:
            r = lax.broadcasted_iota(jnp.int32, (2 * tq, width), 0) & (tq - 1)
            c = lax.broadcasted_iota(jnp.int32, (2 * tq, width), 1)
            s = jnp.where(c + (j - i) * tk <= r, s, NEG_INF)
        m_prev = m_ref[...]
        m_next = jnp.maximum(m_prev, jnp.max(s, axis=1, keepdims=True))
        alpha = jnp.exp2(m_prev - m_next)
        p = jnp.exp2(s - jnp.tile(m_next, (1, width // LANES)))
        pb = p.astype(bf16)
        if fox:
            pv = jnp.concatenate(
                [jnp.dot(pb[:tq], vaug_ref[rows, :LANES], preferred_element_type=f32),
                 jnp.dot(pb[tq:], vaug_ref[rows, LANES:], preferred_element_type=f32)], axis=0)
        else:
            pv = jnp.dot(pb, v_ref[rows, :], preferred_element_type=f32)
            part = p[:, :LANES]
            for blk in range(1, width // LANES):
                part = part + p[:, blk * LANES:(blk + 1) * LANES]
            l_ref[...] = alpha * l_ref[...] + part
        acc_ref[...] = acc_ref[...] * alpha + pv
        m_ref[...] = m_next

    top = jnp.maximum(i - 1, 0)
    qn = half_norms_sq(q)
    first_norm_lane = (0 if fox else FOX_HEADS) + 2 * grp
    kmax = [jnp.sqrt(jnp.sum(jnp.where(lane == first_norm_lane + h, kn2_ref[0:1, :], 0.0),
                             axis=1, keepdims=True)) * NORM_SLACK for h in range(2)]
    bound = [jnp.sqrt(qn[:, h:h + 1]) * kmax[h] for h in range(2)]
    tile(top, 2, masked=True)

    m_col = [m_ref[h * tq:(h + 1) * tq, 0:1] for h in range(2)]
    if fox:
        head_lane = lax.broadcasted_iota(jnp.int32, cend_ref.shape, 1) - 2 * grp
        cend = [jnp.sum(jnp.where(head_lane == h, cend_ref[...], 0.0), axis=1, keepdims=True)
                for h in range(2)]
        below_top = lax.broadcasted_iota(jnp.int32, (n_tiles, 1), 0) < top
        dead = below_top
        for h in range(2):
            gap = _reduce_rows(bound[h] - m_col[h], jnp.maximum)
            dead = dead & (-cend[h] < -UNDERFLOW_LOG2 - gap)
        j_lo = _reduce_rows(dead.astype(f32), jnp.add).astype(jnp.int32)[0, 0]
    else:
        gap = _reduce_rows(jnp.maximum(bound[0] - m_col[0], bound[1] - m_col[1]), jnp.maximum)
        x = (-UNDERFLOW_LOG2 - gap) * inv_slope
        j_lo = jnp.clip(jnp.floor(x * (1.0 / tk)), 0.0, 1.0 * n_tiles)
        j_lo = jnp.minimum(j_lo.astype(jnp.int32)[0, 0], top)

    n_live = top - j_lo
    one = n_live & 1
    two = n_live & 2

    @pl.when(one == 1)
    def _():
        tile(top - 1, 1, masked=False)

    @pl.when(two == 2)
    def _():
        tile(top - one - 2, 2, masked=False)

    def body(step, carry):
        tile(top - one - two - TILES_PER_STEP * (step + 1), TILES_PER_STEP, masked=False)
        return carry

    lax.fori_loop(0, n_live // TILES_PER_STEP, body, 0)

    acc = [acc_ref[h * tq:(h + 1) * tq, :] for h in range(2)]
    if fox:
        out = jnp.where(lo_half, acc[0] / acc[0][:, HEAD_DIM:HEAD_DIM + 1],
                        acc[1] / acc[1][:, 0:1])
    else:
        lam = (jnp.exp(jnp.sum(lq1_ref[...] * lk1_ref[...], axis=1, keepdims=True))
               - jnp.exp(jnp.sum(lq2_ref[...] * lk2_ref[...], axis=1, keepdims=True))
               + LAMBDA_INIT)
        l = jnp.sum(l_ref[...], axis=1, keepdims=True)
        a = acc[0] / l[:tq] - lam * (acc[1] / l[tq:])
        out = _rms(a, gn_ref[...]) * (1.0 - LAMBDA_INIT)
    o_ref[...] = out.astype(o_ref.dtype)


def _attention(q, kaug, v, kn2, extra, *, fox):
    s = q.shape[0]
    groups = FOX_HEADS // 2 if fox else DIFF_HEADS
    q_col0 = 0 if fox else FOX_WIDTH // LANES
    resident = lambda width: pl.BlockSpec((s, width), lambda g, i: (0, g),
                                          pipeline_mode=pl.Buffered(1))
    small = lambda a: pl.BlockSpec(a.shape, lambda g, i: (0, 0))
    in_specs = [
        pl.BlockSpec((TQ, LANES), lambda g, i: (i, q_col0 + g)),
        resident(2 * LANES),
        resident(2 * LANES if fox else LANES),
        small(kn2),
    ] + [small(a) for a in extra]
    scratch = [
        pltpu.VMEM((2 * TQ, LANES), jnp.float32),
        pltpu.VMEM((2 * TQ, LANES), jnp.float32),
    ]
    if not fox:
        scratch.append(pltpu.VMEM((2 * TQ, LANES), jnp.float32))
    return pl.pallas_call(
        functools.partial(_attn_kernel, fox=fox),
        grid=(groups, s // TQ),
        in_specs=in_specs,
        out_specs=pl.BlockSpec((TQ, LANES), lambda g, i: (i, g)),
        out_shape=jax.ShapeDtypeStruct((s, groups * LANES), jnp.bfloat16),
        scratch_shapes=scratch,
        compiler_params=pltpu.CompilerParams(
            dimension_semantics=("arbitrary", "arbitrary"), vmem_limit_bytes=VMEM_LIMIT),
        name="fox_attn" if fox else "diff_attn",
    )(q, kaug, v, kn2, *extra)


def _out_ffn_kernel(x_ref, fo_ref, do_ref, wo_ref, gf_ref, wg_ref, wu_ref, wd_ref,
                    gl_ref, o_ref):
    x1 = (x_ref[...]
          + jnp.dot(fo_ref[...], wo_ref[:FOX_WIDTH, :], preferred_element_type=jnp.float32)
          + jnp.dot(do_ref[...], wo_ref[FOX_WIDTH:, :], preferred_element_type=jnp.float32))
    h = _rms(x1, gf_ref[...]).astype(jnp.bfloat16)
    g = jnp.dot(h, wg_ref[...], preferred_element_type=jnp.float32)
    u = jnp.dot(h, wu_ref[...], preferred_element_type=jnp.float32)
    a = (g * jax.nn.sigmoid(g) * u).astype(jnp.bfloat16)
    x2 = x1 + jnp.dot(a, wd_ref[...], preferred_element_type=jnp.float32)
    o_ref[...] = _rms(x2, gl_ref[...])


def _out_ffn(x, fo, do, wo, gf, wg, wu, wd, gl):
    s = x.shape[0]
    const = lambda shape: pl.BlockSpec(shape, lambda i: (0, 0), pipeline_mode=pl.Buffered(1))
    return pl.pallas_call(
        _out_ffn_kernel,
        grid=(s // TM_FFN,),
        in_specs=[
            pl.BlockSpec((TM_FFN, D_MODEL), lambda i: (i, 0)),
            pl.BlockSpec((TM_FFN, FOX_WIDTH), lambda i: (i, 0)),
            pl.BlockSpec((TM_FFN, DIFF_WIDTH), lambda i: (i, 0)),
            const(wo.shape), const(gf.shape), const(wg.shape), const(wu.shape),
            const(wd.shape), const(gl.shape),
        ],
        out_specs=pl.BlockSpec((TM_FFN, D_MODEL), lambda i: (i, 0)),
        out_shape=jax.ShapeDtypeStruct((s, D_MODEL), jnp.float32),
        compiler_params=pltpu.CompilerParams(
            dimension_semantics=("arbitrary",), vmem_limit_bytes=VMEM_LIMIT),
        name="out_ffn",
    )(x, fo, do, wo, gf, wg, wu, wd, gl)


def kernel(x, mix_norm_g, w_in, b_forget, lambda_q1, lambda_k1, lambda_q2, lambda_k2,
           diff_norm_g, w_out, ffn_norm_g, w_gate, w_up, w_down, final_norm_g):
    b, s, d = x.shape
    assert b == 1 and d == D_MODEL and w_in.shape[0] == 1
    assert s % TQ == 0 and s % TM_IN == 0 and s % TM_FFN == 0
    assert TM_IN == TK and TQ == TK and TILES_PER_STEP == 4 and s >= 2 * TK
    assert s < 2 ** 14 + 1
    bf16 = jnp.bfloat16
    x2d = x.reshape(s, d)

    w = w_in[0]
    col = np.cumsum((0, FOX_WIDTH, FOX_WIDTH, FOX_WIDTH, FOX_HEADS, DIFF_WIDTH, DIFF_WIDTH,
                     DIFF_WIDTH))
    fq, fk, fv, gate, dq, dk, dv = (w[:, col[n]:col[n + 1]] for n in range(7))
    w_r = jnp.concatenate(
        [fk, dk, fq, dq, fv, dv, gate, jnp.zeros((d, GATE_PAD - FOX_HEADS), w.dtype)],
        axis=1).astype(bf16)
    b_pad = jnp.pad(b_forget[0].astype(jnp.float32), (0, GATE_PAD - FOX_HEADS)).reshape(1, GATE_PAD)
    place = jnp.asarray(_fox_aug_placement(), bf16)
    hsum = jnp.asarray(np.arange(2 * FOX_WIDTH)[:, None] // HEAD_DIM == np.arange(LANES)[None, :],
                       bf16)

    q, fox_k, fox_v, diff_k, diff_v, cend, kn2 = _in_proj(
        x2d, mix_norm_g[0].reshape(1, d), w_r, b_pad, place, hsum)

    fox_out = _attention(q, fox_k, fox_v, kn2, (cend,), fox=True)
    lam_args = tuple(a[0].astype(jnp.float32).reshape(1, HEAD_DIM)
                     for a in (lambda_q1, lambda_k1, lambda_q2, lambda_k2))
    diff_out = _attention(
        q, diff_k, diff_v, kn2,
        lam_args + (diff_norm_g[0].astype(jnp.float32).reshape(1, DIFF_V_DIM),), fox=False)

    out = _out_ffn(
        x2d, fox_out, diff_out, w_out[0].astype(bf16), ffn_norm_g[0].reshape(1, d),
        w_gate[0].astype(bf16), w_up[0].astype(bf16), w_down[0].astype(bf16),
        final_norm_g.reshape(1, d))
    return out.reshape(b, s, d)
```

```python
import functools
import math

import numpy as np
import jax
import jax.numpy as jnp
from jax import lax
from jax.experimental import pallas as pl
from jax.experimental.pallas import tpu as pltpu

D_MODEL = 1024
HEAD_DIM = 64
FOX_HEADS = 8
FOX_WIDTH = FOX_HEADS * HEAD_DIM
DIFF_HEADS = 4
DIFF_V_DIM = 2 * HEAD_DIM
DIFF_WIDTH = DIFF_HEADS * DIFF_V_DIM
D_FF = ((8 * D_MODEL + 767) // 768) * 256
EPS = 1e-6
NEG_INF = -1e30
LOG2E = math.log2(math.e)
Q_SCALE = LOG2E / math.sqrt(HEAD_DIM)
UNDERFLOW_LOG2 = 127.0
NORM_SLACK = 1.01
LAMBDA_INIT = 0.8 - 0.6 * math.exp(-0.3 * 0)

LANES = 128
GATE_PAD = LANES
QKV_WIDTH = 3 * FOX_WIDTH + 3 * DIFF_WIDTH
VMEM_LIMIT = 56 * 1024 * 1024

TM_IN = 512
TQ = 512
TK = 512
TILES_PER_STEP = 4
FIRST_TILES_FOX = 4
FIRST_TILES_DIFF = 2
TM_FFN = 512

_NT = (((1,), (1,)), ((), ()))


def _rms(x, g):
    return x * lax.rsqrt(jnp.mean(x * x, axis=-1, keepdims=True) + EPS) * g


def _split3(x):
    hi = x.astype(jnp.bfloat16)
    r1 = x - hi.astype(jnp.float32)
    mid = r1.astype(jnp.bfloat16)
    lo = (r1 - mid.astype(jnp.float32)).astype(jnp.bfloat16)
    return hi, mid, lo


def _reduce_rows(x, op):
    while x.shape[0] > 8 and x.shape[0] % 2 == 0:
        half = x.shape[0] // 2
        x = op(x[:half], x[half:])
    reduce = jnp.max if op is jnp.maximum else jnp.sum
    return reduce(x, axis=0, keepdims=True)


def _fox_aug_placement():
    p = np.zeros((LANES, FOX_WIDTH), np.float32)
    for part in range(3):
        for head in range(FOX_HEADS):
            lane = 3 * (head % 2) + part
            p[part * FOX_HEADS + head, (head // 2) * LANES + lane] = 1.0
    return p


def _in_proj_kernel(x_ref, g_ref, w_ref, b_ref, place_ref, hsum_ref,
                    q_ref, fk_ref, fv_ref, dk_ref, dv_ref, cend_ref, kn2_ref,
                    tri_ref, carry_ref):
    step = pl.program_id(0)
    tm = x_ref.shape[0]
    f32, bf16 = jnp.float32, jnp.bfloat16

    @pl.when(step == 0)
    def _():
        r = lax.broadcasted_iota(jnp.int32, (tm, tm), 0)
        c = lax.broadcasted_iota(jnp.int32, (tm, tm), 1)
        tri_ref[...] = jnp.where(r >= c, 1.0, 0.0).astype(bf16)
        carry_ref[...] = jnp.zeros_like(carry_ref)
        kn2_ref[...] = jnp.zeros_like(kn2_ref)

    lane = lax.broadcasted_iota(jnp.int32, (1, LANES), 1)
    lo_half = lane < HEAD_DIM
    head_lanes = lane < FOX_HEADS
    w5 = FOX_WIDTH

    def pack3(x):
        hi, mid, lo = (p.astype(f32) for p in _split3(jnp.where(head_lanes, x, 0.0)))
        return (hi + pltpu.roll(mid, FOX_HEADS, axis=1)
                + pltpu.roll(lo, 2 * FOX_HEADS, axis=1)).astype(bf16)

    h = _rms(x_ref[...], g_ref[...]).astype(bf16)

    z = jnp.dot(h, w_ref[:, QKV_WIDTH:], preferred_element_type=f32) + b_ref[...]
    logf = LOG2E * (jnp.minimum(z, 0.0) - jnp.log1p(jnp.exp(-jnp.abs(z))))
    cs = jnp.dot(tri_ref[...], pack3(logf), preferred_element_type=f32)
    c = (cs + pltpu.roll(cs, LANES - FOX_HEADS, axis=1)
         + pltpu.roll(cs, LANES - 2 * FOX_HEADS, axis=1) + carry_ref[0:1, :])
    last = c[tm - 1:tm, :]
    carry_ref[...] = jnp.broadcast_to(last, carry_ref.shape)
    cend_ref[pl.ds(step, 1), :] = last
    caug = jnp.dot(pack3(-c), place_ref[...], preferred_element_type=f32).astype(bf16)

    yk = jnp.dot(h, w_ref[:, :2 * w5], preferred_element_type=f32)
    fk = yk[:, :w5].astype(bf16)
    dk = yk[:, w5:].astype(bf16)
    n2 = jnp.dot(jnp.concatenate([fk * fk, dk * dk], axis=1), hsum_ref[...],
                 preferred_element_type=f32)
    kn2_ref[...] = jnp.maximum(kn2_ref[...],
                               jnp.broadcast_to(jnp.max(n2, axis=0, keepdims=True), kn2_ref.shape))

    y = jnp.dot(h, w_ref[:, 2 * w5:QKV_WIDTH], preferred_element_type=f32)
    q_ref[...] = (y[:, :2 * w5] * Q_SCALE).astype(bf16)
    fv = y[:, 2 * w5:3 * w5].astype(bf16)
    dv_ref[...] = y[:, 3 * w5:].astype(bf16)

    pos = step * tm + lax.broadcasted_iota(jnp.int32, (tm, LANES), 0)
    pos_lanes = jnp.where(lane < 3, pos & ~(LANES - 1),
                          jnp.where(lane < 6, pos & (LANES - 1), 0)).astype(f32).astype(bf16)
    for g in range(FOX_WIDTH // LANES):
        cols = slice(g * LANES, (g + 1) * LANES)
        key = slice(2 * g * LANES, (2 * g + 1) * LANES)
        bias = slice((2 * g + 1) * LANES, (2 * g + 2) * LANES)
        fk_ref[:, key] = fk[:, cols]
        fk_ref[:, bias] = caug[:, cols]
        dk_ref[:, key] = dk[:, cols]
        dk_ref[:, bias] = pos_lanes
        fv_ref[:, key] = jnp.where(lo_half, fv[:, cols], (lane == HEAD_DIM).astype(bf16))
        fv_ref[:, bias] = jnp.where(lo_half, (lane == 0).astype(bf16), fv[:, cols])


def _in_proj(x, g, w, b, place, hsum):
    s = x.shape[0]
    n_tiles = s // TM_IN
    const = lambda a: pl.BlockSpec(a.shape, lambda i: (0, 0))
    row_block = lambda width: pl.BlockSpec((TM_IN, width), lambda i: (i, 0))
    bf16 = jnp.bfloat16
    return pl.pallas_call(
        _in_proj_kernel,
        grid=(n_tiles,),
        in_specs=[row_block(D_MODEL), const(g), const(w), const(b), const(place), const(hsum)],
        out_specs=[
            row_block(2 * FOX_WIDTH),
            row_block(2 * FOX_WIDTH),
            row_block(2 * FOX_WIDTH),
            row_block(2 * DIFF_WIDTH),
            row_block(DIFF_WIDTH),
            pl.BlockSpec((n_tiles, LANES), lambda i: (0, 0)),
            pl.BlockSpec((8, LANES), lambda i: (0, 0)),
        ],
        out_shape=[
            jax.ShapeDtypeStruct((s, 2 * FOX_WIDTH), bf16),
            jax.ShapeDtypeStruct((s, 2 * FOX_WIDTH), bf16),
            jax.ShapeDtypeStruct((s, 2 * FOX_WIDTH), bf16),
            jax.ShapeDtypeStruct((s, 2 * DIFF_WIDTH), bf16),
            jax.ShapeDtypeStruct((s, DIFF_WIDTH), bf16),
            jax.ShapeDtypeStruct((n_tiles, LANES), jnp.float32),
            jax.ShapeDtypeStruct((8, LANES), jnp.float32),
        ],
        scratch_shapes=[
            pltpu.VMEM((TM_IN, TM_IN), bf16),
            pltpu.VMEM((8, LANES), jnp.float32),
        ],
        compiler_params=pltpu.CompilerParams(
            dimension_semantics=("arbitrary",), vmem_limit_bytes=VMEM_LIMIT),
        name="in_proj",
    )(x, g, w, b, place, hsum)


def _attn_kernel(*refs, fox):
    if fox:
        q_ref, kaug_ref, vaug_ref, kn2_ref, cend_ref, o_ref, m_ref, acc_ref = refs
    else:
        (q_ref, kaug_ref, v_ref, kn2_ref, lq1_ref, lk1_ref, lq2_ref, lk2_ref, gn_ref,
         o_ref, m_ref, acc_ref, l_ref) = refs
    grp = pl.program_id(0)
    i = pl.program_id(1)
    tq, tk = TQ, TK
    n_tiles = kaug_ref.shape[0] // tk
    f32, bf16 = jnp.float32, jnp.bfloat16

    lane = lax.broadcasted_iota(jnp.int32, (1, LANES), 1)
    lo_half = lane < HEAD_DIM

    if not fox:
        slope = LOG2E * jnp.exp2(-2.0 * jnp.full((1, 1), grp + 1, jnp.int32).astype(f32))
        sl = [p.astype(f32) for p in _split3(slope)]
        inv_slope = 1.0 / slope

    def half_norms_sq(x):
        l_in = lax.broadcasted_iota(jnp.int32, (LANES, LANES), 0)
        l_out = lax.broadcasted_iota(jnp.int32, (LANES, LANES), 1)
        half_sum = ((l_in // HEAD_DIM) == l_out).astype(bf16)
        return jnp.dot(x * x, half_sum, preferred_element_type=f32)

    q = q_ref[...]
    zero = jnp.zeros_like(q)
    if fox:
        q_bias = [jnp.broadcast_to(((lane >= 3 * h) & (lane < 3 * h + 3)).astype(bf16), q.shape)
                  for h in range(2)]
    else:
        parts = jnp.where((lane == 0) | (lane == 3), sl[0],
                          jnp.where((lane == 1) | (lane == 4), sl[1],
                                    jnp.where((lane == 2) | (lane == 5), sl[2], 0.0)))
        q_bias = [jnp.broadcast_to(parts.astype(bf16), q.shape)] * 2
    q2 = jnp.concatenate(
        [jnp.concatenate([jnp.where(lo_half, q, zero), q_bias[0]], axis=1),
         jnp.concatenate([jnp.where(lo_half, zero, q), q_bias[1]], axis=1)], axis=0)

    m_ref[...] = jnp.full_like(m_ref, NEG_INF)
    acc_ref[...] = jnp.zeros_like(acc_ref)
    if not fox:
        l_ref[...] = jnp.zeros_like(l_ref)

    def tile(j, n_sub, masked):
        width = n_sub * tk
        rows = pl.ds(pl.multiple_of(j * tk, tk), width)
        s = lax.dot_general(q2, kaug_ref[rows, :], _NT, preferred_element_type=f32)
        if masked == "diagonal_last":
            r = lax.broadcasted_iota(jnp.int32, (2 * tq, tk), 0) & (tq - 1)
            c = lax.broadcasted_iota(jnp.int32, (2 * tq, tk), 1)
            diag = jnp.where(c <= r, s[:, width - tk:], NEG_INF)
            s = diag if n_sub == 1 else jnp.concatenate([s[:, :width - tk], diag], axis=1)
        elif masked == "by_position":
            r = lax.broadcasted_iota(jnp.int32, (2 * tq, width), 0) & (tq - 1)
            c = lax.broadcasted_iota(jnp.int32, (2 * tq, width), 1)
            s = jnp.where(c + (j - i) * tk <= r, s, NEG_INF)
        m_prev = m_ref[...]
        m_next = jnp.maximum(m_prev, jnp.max(s, axis=1, keepdims=True))
        alpha = jnp.exp2(m_prev - m_next)
        p = jnp.exp2(s - jnp.tile(m_next, (1, width // LANES)))
        pb = p.astype(bf16)
        if fox:
            pv = jnp.concatenate(
                [jnp.dot(pb[:tq], vaug_ref[rows, :LANES], preferred_element_type=f32),
                 jnp.dot(pb[tq:], vaug_ref[rows, LANES:], preferred_element_type=f32)], axis=0)
        else:
            pv = jnp.dot(pb, v_ref[rows, :], preferred_element_type=f32)
            part = p[:, :LANES]
            for blk in range(1, width // LANES):
                part = part + p[:, blk * LANES:(blk + 1) * LANES]
            l_ref[...] = alpha * l_ref[...] + part
        acc_ref[...] = acc_ref[...] * alpha + pv
        m_ref[...] = m_next

    first = FIRST_TILES_FOX if fox else FIRST_TILES_DIFF
    top = jnp.maximum(i - (first - 1), 0)

    def first_step(masked):
        qn = half_norms_sq(q)
        first_norm_lane = (0 if fox else FOX_HEADS) + 2 * grp
        kmax = [jnp.sqrt(jnp.sum(jnp.where(lane == first_norm_lane + h, kn2_ref[0:1, :], 0.0),
                                 axis=1, keepdims=True)) * NORM_SLACK for h in range(2)]
        bound = [jnp.sqrt(qn[:, h:h + 1]) * kmax[h] for h in range(2)]
        tile(top, first, masked)
        m_col = [m_ref[h * tq:(h + 1) * tq, 0:1] for h in range(2)]
        if fox:
            head_lane = lax.broadcasted_iota(jnp.int32, cend_ref.shape, 1) - 2 * grp
            cend = [jnp.sum(jnp.where(head_lane == h, cend_ref[...], 0.0), axis=1, keepdims=True)
                    for h in range(2)]
            below_top = lax.broadcasted_iota(jnp.int32, (n_tiles, 1), 0) < top
            dead = below_top
            for h in range(2):
                gap = _reduce_rows(bound[h] - m_col[h], jnp.maximum)
                dead = dead & (-cend[h] < -UNDERFLOW_LOG2 - gap)
            return _reduce_rows(dead.astype(f32), jnp.add).astype(jnp.int32)[0, 0]
        gap = _reduce_rows(jnp.maximum(bound[0] - m_col[0], bound[1] - m_col[1]), jnp.maximum)
        x = (-UNDERFLOW_LOG2 - gap) * inv_slope
        j_lo = jnp.clip(jnp.floor(x * (1.0 / tk)), 0.0, 1.0 * n_tiles)
        return jnp.minimum(j_lo.astype(jnp.int32)[0, 0], top)

    j_lo = lax.cond(i >= first - 1,
                    functools.partial(first_step, "diagonal_last"),
                    functools.partial(first_step, "by_position"))

    n_live = top - j_lo
    one = n_live & 1
    two = n_live & 2

    @pl.when(one == 1)
    def _():
        tile(top - 1, 1, masked=False)

    @pl.when(two == 2)
    def _():
        tile(top - one - 2, 2, masked=False)

    def body(step, carry):
        tile(top - one - two - TILES_PER_STEP * (step + 1), TILES_PER_STEP, masked=False)
        return carry

    lax.fori_loop(0, n_live // TILES_PER_STEP, body, 0)

    acc = [acc_ref[h * tq:(h + 1) * tq, :] for h in range(2)]
    if fox:
        out = jnp.where(lo_half, acc[0] / acc[0][:, HEAD_DIM:HEAD_DIM + 1],
                        acc[1] / acc[1][:, 0:1])
    else:
        lam = (jnp.exp(jnp.sum(lq1_ref[...] * lk1_ref[...], axis=1, keepdims=True))
               - jnp.exp(jnp.sum(lq2_ref[...] * lk2_ref[...], axis=1, keepdims=True))
               + LAMBDA_INIT)
        l = jnp.sum(l_ref[...], axis=1, keepdims=True)
        a = acc[0] / l[:tq] - lam * (acc[1] / l[tq:])
        out = _rms(a, gn_ref[...]) * (1.0 - LAMBDA_INIT)
    o_ref[...] = out.astype(o_ref.dtype)


def _attention(q, kaug, v, kn2, extra, *, fox):
    s = q.shape[0]
    groups = FOX_HEADS // 2 if fox else DIFF_HEADS
    q_col0 = 0 if fox else FOX_WIDTH // LANES
    resident = lambda width: pl.BlockSpec((s, width), lambda g, i: (0, g),
                                          pipeline_mode=pl.Buffered(1))
    small = lambda a: pl.BlockSpec(a.shape, lambda g, i: (0, 0))
    in_specs = [
        pl.BlockSpec((TQ, LANES), lambda g, i: (i, q_col0 + g)),
        resident(2 * LANES),
        resident(2 * LANES if fox else LANES),
        small(kn2),
    ] + [small(a) for a in extra]
    scratch = [
        pltpu.VMEM((2 * TQ, LANES), jnp.float32),
        pltpu.VMEM((2 * TQ, LANES), jnp.float32),
    ]
    if not fox:
        scratch.append(pltpu.VMEM((2 * TQ, LANES), jnp.float32))
    return pl.pallas_call(
        functools.partial(_attn_kernel, fox=fox),
        grid=(groups, s // TQ),
        in_specs=in_specs,
        out_specs=pl.BlockSpec((TQ, LANES), lambda g, i: (i, g)),
        out_shape=jax.ShapeDtypeStruct((s, groups * LANES), jnp.bfloat16),
        scratch_shapes=scratch,
        compiler_params=pltpu.CompilerParams(
            dimension_semantics=("arbitrary", "arbitrary"), vmem_limit_bytes=VMEM_LIMIT),
        name="fox_attn" if fox else "diff_attn",
    )(q, kaug, v, kn2, *extra)


def _out_ffn_kernel(x_ref, fo_ref, do_ref, wo_ref, gf_ref, wg_ref, wu_ref, wd_ref,
                    gl_ref, o_ref):
    x1 = (x_ref[...]
          + jnp.dot(fo_ref[...], wo_ref[:FOX_WIDTH, :], preferred_element_type=jnp.float32)
          + jnp.dot(do_ref[...], wo_ref[FOX_WIDTH:, :], preferred_element_type=jnp.float32))
    h = _rms(x1, gf_ref[...]).astype(jnp.bfloat16)
    g = jnp.dot(h, wg_ref[...], preferred_element_type=jnp.float32)
    u = jnp.dot(h, wu_ref[...], preferred_element_type=jnp.float32)
    a = (g * jax.nn.sigmoid(g) * u).astype(jnp.bfloat16)
    x2 = x1 + jnp.dot(a, wd_ref[...], preferred_element_type=jnp.float32)
    o_ref[...] = _rms(x2, gl_ref[...])


def _out_ffn(x, fo, do, wo, gf, wg, wu, wd, gl):
    s = x.shape[0]
    const = lambda shape: pl.BlockSpec(shape, lambda i: (0, 0), pipeline_mode=pl.Buffered(1))
    return pl.pallas_call(
        _out_ffn_kernel,
        grid=(s // TM_FFN,),
        in_specs=[
            pl.BlockSpec((TM_FFN, D_MODEL), lambda i: (i, 0)),
            pl.BlockSpec((TM_FFN, FOX_WIDTH), lambda i: (i, 0)),
            pl.BlockSpec((TM_FFN, DIFF_WIDTH), lambda i: (i, 0)),
            const(wo.shape), const(gf.shape), const(wg.shape), const(wu.shape),
            const(wd.shape), const(gl.shape),
        ],
        out_specs=pl.BlockSpec((TM_FFN, D_MODEL), lambda i: (i, 0)),
        out_shape=jax.ShapeDtypeStruct((s, D_MODEL), jnp.float32),
        compiler_params=pltpu.CompilerParams(
            dimension_semantics=("arbitrary",), vmem_limit_bytes=VMEM_LIMIT),
        name="out_ffn",
    )(x, fo, do, wo, gf, wg, wu, wd, gl)


def kernel(x, mix_norm_g, w_in, b_forget, lambda_q1, lambda_k1, lambda_q2, lambda_k2,
           diff_norm_g, w_out, ffn_norm_g, w_gate, w_up, w_down, final_norm_g):
    b, s, d = x.shape
    assert b == 1 and d == D_MODEL and w_in.shape[0] == 1
    assert s % TQ == 0 and s % TM_IN == 0 and s % TM_FFN == 0
    assert TM_IN == TK and TQ == TK and TILES_PER_STEP == 4
    assert s >= max(FIRST_TILES_FOX, FIRST_TILES_DIFF) * TK
    assert s < 2 ** 14 + 1
    bf16 = jnp.bfloat16
    x2d = x.reshape(s, d)

    w = w_in[0]
    col = np.cumsum((0, FOX_WIDTH, FOX_WIDTH, FOX_WIDTH, FOX_HEADS, DIFF_WIDTH, DIFF_WIDTH,
                     DIFF_WIDTH))
    fq, fk, fv, gate, dq, dk, dv = (w[:, col[n]:col[n + 1]] for n in range(7))
    w_r = jnp.concatenate(
        [fk, dk, fq, dq, fv, dv, gate, jnp.zeros((d, GATE_PAD - FOX_HEADS), w.dtype)],
        axis=1).astype(bf16)
    b_pad = jnp.pad(b_forget[0].astype(jnp.float32), (0, GATE_PAD - FOX_HEADS)).reshape(1, GATE_PAD)
    place = jnp.asarray(_fox_aug_placement(), bf16)
    hsum = jnp.asarray(np.arange(2 * FOX_WIDTH)[:, None] // HEAD_DIM == np.arange(LANES)[None, :],
                       bf16)

    q, fox_k, fox_v, diff_k, diff_v, cend, kn2 = _in_proj(
        x2d, mix_norm_g[0].reshape(1, d), w_r, b_pad, place, hsum)

    fox_out = _attention(q, fox_k, fox_v, kn2, (cend,), fox=True)
    lam_args = tuple(a[0].astype(jnp.float32).reshape(1, HEAD_DIM)
                     for a in (lambda_q1, lambda_k1, lambda_q2, lambda_k2))
    diff_out = _attention(
        q, diff_k, diff_v, kn2,
        lam_args + (diff_norm_g[0].astype(jnp.float32).reshape(1, DIFF_V_DIM),), fox=False)

    out = _out_ffn(
        x2d, fox_out, diff_out, w_out[0].astype(bf16), ffn_norm_g[0].reshape(1, d),
        w_gate[0].astype(bf16), w_up[0].astype(bf16), w_down[0].astype(bf16),
        final_norm_g.reshape(1, d))
    return out.reshape(b, s, d)
```

```python
import functools
import math

import numpy as np
import jax
import jax.numpy as jnp
from jax import lax
from jax.experimental import pallas as pl
from jax.experimental.pallas import tpu as pltpu

D_MODEL = 1024
HEAD_DIM = 64
FOX_HEADS = 8
FOX_WIDTH = FOX_HEADS * HEAD_DIM
DIFF_HEADS = 4
DIFF_V_DIM = 2 * HEAD_DIM
DIFF_WIDTH = DIFF_HEADS * DIFF_V_DIM
D_FF = ((8 * D_MODEL + 767) // 768) * 256
EPS = 1e-6
NEG_INF = -1e30
LOG2E = math.log2(math.e)
Q_SCALE = LOG2E / math.sqrt(HEAD_DIM)
UNDERFLOW_LOG2 = 127.0
NORM_SLACK = 1.01
LAMBDA_INIT = 0.8 - 0.6 * math.exp(-0.3 * 0)

LANES = 128
GATE_PAD = LANES
QKV_WIDTH = 3 * FOX_WIDTH + 3 * DIFF_WIDTH
VMEM_LIMIT = 56 * 1024 * 1024

TM_IN = 512
TQ = 512
TK = 512
TILES_PER_STEP = 4
FIRST_TILES_FOX = 4
FIRST_TILES_DIFF = 2
Q_TILES_PER_STEP = 4
TM_FFN = 512

_NT = (((1,), (1,)), ((), ()))


def _rms(x, g):
    return x * lax.rsqrt(jnp.mean(x * x, axis=-1, keepdims=True) + EPS) * g


def _split3(x):
    hi = x.astype(jnp.bfloat16)
    r1 = x - hi.astype(jnp.float32)
    mid = r1.astype(jnp.bfloat16)
    lo = (r1 - mid.astype(jnp.float32)).astype(jnp.bfloat16)
    return hi, mid, lo


def _reduce_rows(x, op):
    while x.shape[0] > 8 and x.shape[0] % 2 == 0:
        half = x.shape[0] // 2
        x = op(x[:half], x[half:])
    reduce = jnp.max if op is jnp.maximum else jnp.sum
    return reduce(x, axis=0, keepdims=True)


def _fox_aug_placement():
    p = np.zeros((LANES, FOX_WIDTH), np.float32)
    for part in range(3):
        for head in range(FOX_HEADS):
            lane = 3 * (head % 2) + part
            p[part * FOX_HEADS + head, (head // 2) * LANES + lane] = 1.0
    return p


def _in_proj_kernel(x_ref, g_ref, w_ref, b_ref, place_ref, hsum_ref,
                    q_ref, fk_ref, fv_ref, dk_ref, dv_ref, cend_ref, kn2_ref,
                    tri_ref, carry_ref):
    step = pl.program_id(0)
    tm = x_ref.shape[0]
    f32, bf16 = jnp.float32, jnp.bfloat16

    @pl.when(step == 0)
    def _():
        r = lax.broadcasted_iota(jnp.int32, (tm, tm), 0)
        c = lax.broadcasted_iota(jnp.int32, (tm, tm), 1)
        tri_ref[...] = jnp.where(r >= c, 1.0, 0.0).astype(bf16)
        carry_ref[...] = jnp.zeros_like(carry_ref)
        kn2_ref[...] = jnp.zeros_like(kn2_ref)

    lane = lax.broadcasted_iota(jnp.int32, (1, LANES), 1)
    lo_half = lane < HEAD_DIM
    head_lanes = lane < FOX_HEADS
    w5 = FOX_WIDTH

    def pack3(x):
        hi, mid, lo = (p.astype(f32) for p in _split3(jnp.where(head_lanes, x, 0.0)))
        return (hi + pltpu.roll(mid, FOX_HEADS, axis=1)
                + pltpu.roll(lo, 2 * FOX_HEADS, axis=1)).astype(bf16)

    h = _rms(x_ref[...], g_ref[...]).astype(bf16)

    z = jnp.dot(h, w_ref[:, QKV_WIDTH:], preferred_element_type=f32) + b_ref[...]
    logf = LOG2E * (jnp.minimum(z, 0.0) - jnp.log1p(jnp.exp(-jnp.abs(z))))
    cs = jnp.dot(tri_ref[...], pack3(logf), preferred_element_type=f32)
    c = (cs + pltpu.roll(cs, LANES - FOX_HEADS, axis=1)
         + pltpu.roll(cs, LANES - 2 * FOX_HEADS, axis=1) + carry_ref[0:1, :])
    last = c[tm - 1:tm, :]
    carry_ref[...] = jnp.broadcast_to(last, carry_ref.shape)
    cend_ref[pl.ds(step, 1), :] = last
    caug = jnp.dot(pack3(-c), place_ref[...], preferred_element_type=f32).astype(bf16)

    yk = jnp.dot(h, w_ref[:, :2 * w5], preferred_element_type=f32)
    fk = yk[:, :w5].astype(bf16)
    dk = yk[:, w5:].astype(bf16)
    n2 = jnp.dot(jnp.concatenate([fk * fk, dk * dk], axis=1), hsum_ref[...],
                 preferred_element_type=f32)
    kn2_ref[...] = jnp.maximum(kn2_ref[...],
                               jnp.broadcast_to(jnp.max(n2, axis=0, keepdims=True), kn2_ref.shape))

    y = jnp.dot(h, w_ref[:, 2 * w5:QKV_WIDTH], preferred_element_type=f32)
    q_ref[...] = (y[:, :2 * w5] * Q_SCALE).astype(bf16)
    fv = y[:, 2 * w5:3 * w5].astype(bf16)
    dv_ref[...] = y[:, 3 * w5:].astype(bf16)

    pos = step * tm + lax.broadcasted_iota(jnp.int32, (tm, LANES), 0)
    pos_lanes = jnp.where(lane < 3, pos & ~(LANES - 1),
                          jnp.where(lane < 6, pos & (LANES - 1), 0)).astype(f32).astype(bf16)
    for g in range(FOX_WIDTH // LANES):
        cols = slice(g * LANES, (g + 1) * LANES)
        key = slice(2 * g * LANES, (2 * g + 1) * LANES)
        bias = slice((2 * g + 1) * LANES, (2 * g + 2) * LANES)
        fk_ref[:, key] = fk[:, cols]
        fk_ref[:, bias] = caug[:, cols]
        dk_ref[:, key] = dk[:, cols]
        dk_ref[:, bias] = pos_lanes
        fv_ref[:, key] = jnp.where(lo_half, fv[:, cols], (lane == HEAD_DIM).astype(bf16))
        fv_ref[:, bias] = jnp.where(lo_half, (lane == 0).astype(bf16), fv[:, cols])


def _in_proj(x, g, w, b, place, hsum):
    s = x.shape[0]
    n_tiles = s // TM_IN
    const = lambda a: pl.BlockSpec(a.shape, lambda i: (0, 0))
    row_block = lambda width: pl.BlockSpec((TM_IN, width), lambda i: (i, 0))
    bf16 = jnp.bfloat16
    return pl.pallas_call(
        _in_proj_kernel,
        grid=(n_tiles,),
        in_specs=[row_block(D_MODEL), const(g), const(w), const(b), const(place), const(hsum)],
        out_specs=[
            row_block(2 * FOX_WIDTH),
            row_block(2 * FOX_WIDTH),
            row_block(2 * FOX_WIDTH),
            row_block(2 * DIFF_WIDTH),
            row_block(DIFF_WIDTH),
            pl.BlockSpec((n_tiles, LANES), lambda i: (0, 0)),
            pl.BlockSpec((8, LANES), lambda i: (0, 0)),
        ],
        out_shape=[
            jax.ShapeDtypeStruct((s, 2 * FOX_WIDTH), bf16),
            jax.ShapeDtypeStruct((s, 2 * FOX_WIDTH), bf16),
            jax.ShapeDtypeStruct((s, 2 * FOX_WIDTH), bf16),
            jax.ShapeDtypeStruct((s, 2 * DIFF_WIDTH), bf16),
            jax.ShapeDtypeStruct((s, DIFF_WIDTH), bf16),
            jax.ShapeDtypeStruct((n_tiles, LANES), jnp.float32),
            jax.ShapeDtypeStruct((8, LANES), jnp.float32),
        ],
        scratch_shapes=[
            pltpu.VMEM((TM_IN, TM_IN), bf16),
            pltpu.VMEM((8, LANES), jnp.float32),
        ],
        compiler_params=pltpu.CompilerParams(
            dimension_semantics=("arbitrary",), vmem_limit_bytes=VMEM_LIMIT),
        name="in_proj",
    )(x, g, w, b, place, hsum)


def _attn_kernel(*refs, fox):
    def one_tile(t, carry):
        rows = pl.ds(pl.multiple_of(t * TQ, TQ), TQ)
        _attn_query_tile(pl.program_id(1) * Q_TILES_PER_STEP + t, rows, refs, fox)
        return carry

    lax.fori_loop(0, Q_TILES_PER_STEP, one_tile, 0)


def _attn_query_tile(i, q_rows, refs, fox):
    if fox:
        q_ref, kaug_ref, vaug_ref, kn2_ref, cend_ref, o_ref, m_ref, acc_ref = refs
    else:
        (q_ref, kaug_ref, v_ref, kn2_ref, lq1_ref, lk1_ref, lq2_ref, lk2_ref, gn_ref,
         o_ref, m_ref, acc_ref, l_ref) = refs
    grp = pl.program_id(0)
    tq, tk = TQ, TK
    n_tiles = kaug_ref.shape[0] // tk
    f32, bf16 = jnp.float32, jnp.bfloat16

    lane = lax.broadcasted_iota(jnp.int32, (1, LANES), 1)
    lo_half = lane < HEAD_DIM

    if not fox:
        slope = LOG2E * jnp.exp2(-2.0 * jnp.full((1, 1), grp + 1, jnp.int32).astype(f32))
        sl = [p.astype(f32) for p in _split3(slope)]
        inv_slope = 1.0 / slope

    def half_norms_sq(x):
        l_in = lax.broadcasted_iota(jnp.int32, (LANES, LANES), 0)
        l_out = lax.broadcasted_iota(jnp.int32, (LANES, LANES), 1)
        half_sum = ((l_in // HEAD_DIM) == l_out).astype(bf16)
        return jnp.dot(x * x, half_sum, preferred_element_type=f32)

    q = q_ref[q_rows, :]
    zero = jnp.zeros_like(q)
    if fox:
        q_bias = [jnp.broadcast_to(((lane >= 3 * h) & (lane < 3 * h + 3)).astype(bf16), q.shape)
                  for h in range(2)]
    else:
        parts = jnp.where((lane == 0) | (lane == 3), sl[0],
                          jnp.where((lane == 1) | (lane == 4), sl[1],
                                    jnp.where((lane == 2) | (lane == 5), sl[2], 0.0)))
        q_bias = [jnp.broadcast_to(parts.astype(bf16), q.shape)] * 2
    q2 = jnp.concatenate(
        [jnp.concatenate([jnp.where(lo_half, q, zero), q_bias[0]], axis=1),
         jnp.concatenate([jnp.where(lo_half, zero, q), q_bias[1]], axis=1)], axis=0)

    m_ref[...] = jnp.full_like(m_ref, NEG_INF)
    acc_ref[...] = jnp.zeros_like(acc_ref)
    if not fox:
        l_ref[...] = jnp.zeros_like(l_ref)

    def tile(j, n_sub, masked):
        width = n_sub * tk
        rows = pl.ds(pl.multiple_of(j * tk, tk), width)
        s = lax.dot_general(q2, kaug_ref[rows, :], _NT, preferred_element_type=f32)
        if masked == "diagonal_last":
            r = lax.broadcasted_iota(jnp.int32, (2 * tq, tk), 0) & (tq - 1)
            c = lax.broadcasted_iota(jnp.int32, (2 * tq, tk), 1)
            diag = jnp.where(c <= r, s[:, width - tk:], NEG_INF)
            s = diag if n_sub == 1 else jnp.concatenate([s[:, :width - tk], diag], axis=1)
        elif masked == "by_position":
            r = lax.broadcasted_iota(jnp.int32, (2 * tq, width), 0) & (tq - 1)
            c = lax.broadcasted_iota(jnp.int32, (2 * tq, width), 1)
            s = jnp.where(c + (j - i) * tk <= r, s, NEG_INF)
        m_prev = m_ref[...]
        m_next = jnp.maximum(m_prev, jnp.max(s, axis=1, keepdims=True))
        alpha = jnp.exp2(m_prev - m_next)
        p = jnp.exp2(s - jnp.tile(m_next, (1, width // LANES)))
        pb = p.astype(bf16)
        if fox:
            pv = jnp.concatenate(
                [jnp.dot(pb[:tq], vaug_ref[rows, :LANES], preferred_element_type=f32),
                 jnp.dot(pb[tq:], vaug_ref[rows, LANES:], preferred_element_type=f32)], axis=0)
        else:
            pv = jnp.dot(pb, v_ref[rows, :], preferred_element_type=f32)
            part = p[:, :LANES]
            for blk in range(1, width // LANES):
                part = part + p[:, blk * LANES:(blk + 1) * LANES]
            l_ref[...] = alpha * l_ref[...] + part
        acc_ref[...] = acc_ref[...] * alpha + pv
        m_ref[...] = m_next

    first = FIRST_TILES_FOX if fox else FIRST_TILES_DIFF
    top = jnp.maximum(i - (first - 1), 0)

    def first_step(masked):
        qn = half_norms_sq(q)
        first_norm_lane = (0 if fox else FOX_HEADS) + 2 * grp
        kmax = [jnp.sqrt(jnp.sum(jnp.where(lane == first_norm_lane + h, kn2_ref[0:1, :], 0.0),
                                 axis=1, keepdims=True)) * NORM_SLACK for h in range(2)]
        bound = [jnp.sqrt(qn[:, h:h + 1]) * kmax[h] for h in range(2)]
        tile(top, first, masked)
        m_col = [m_ref[h * tq:(h + 1) * tq, 0:1] for h in range(2)]
        if fox:
            head_lane = lax.broadcasted_iota(jnp.int32, cend_ref.shape, 1) - 2 * grp
            cend = [jnp.sum(jnp.where(head_lane == h, cend_ref[...], 0.0), axis=1, keepdims=True)
                    for h in range(2)]
            below_top = lax.broadcasted_iota(jnp.int32, (n_tiles, 1), 0) < top
            dead = below_top
            for h in range(2):
                gap = _reduce_rows(bound[h] - m_col[h], jnp.maximum)
                dead = dead & (-cend[h] < -UNDERFLOW_LOG2 - gap)
            return _reduce_rows(dead.astype(f32), jnp.add).astype(jnp.int32)[0, 0]
        gap = _reduce_rows(jnp.maximum(bound[0] - m_col[0], bound[1] - m_col[1]), jnp.maximum)
        x = (-UNDERFLOW_LOG2 - gap) * inv_slope
        j_lo = jnp.clip(jnp.floor(x * (1.0 / tk)), 0.0, 1.0 * n_tiles)
        return jnp.minimum(j_lo.astype(jnp.int32)[0, 0], top)

    j_lo = lax.cond(i >= first - 1,
                    functools.partial(first_step, "diagonal_last"),
                    functools.partial(first_step, "by_position"))

    n_live = top - j_lo
    one = n_live & 1
    two = n_live & 2

    @pl.when(one == 1)
    def _():
        tile(top - 1, 1, masked=False)

    @pl.when(two == 2)
    def _():
        tile(top - one - 2, 2, masked=False)

    def body(step, carry):
        tile(top - one - two - TILES_PER_STEP * (step + 1), TILES_PER_STEP, masked=False)
        return carry

    lax.fori_loop(0, n_live // TILES_PER_STEP, body, 0)

    acc = [acc_ref[h * tq:(h + 1) * tq, :] for h in range(2)]
    if fox:
        out = jnp.where(lo_half, acc[0] / acc[0][:, HEAD_DIM:HEAD_DIM + 1],
                        acc[1] / acc[1][:, 0:1])
    else:
        lam = (jnp.exp(jnp.sum(lq1_ref[...] * lk1_ref[...], axis=1, keepdims=True))
               - jnp.exp(jnp.sum(lq2_ref[...] * lk2_ref[...], axis=1, keepdims=True))
               + LAMBDA_INIT)
        l = jnp.sum(l_ref[...], axis=1, keepdims=True)
        a = acc[0] / l[:tq] - lam * (acc[1] / l[tq:])
        out = _rms(a, gn_ref[...]) * (1.0 - LAMBDA_INIT)
    o_ref[q_rows, :] = out.astype(o_ref.dtype)


def _attention(q, kaug, v, kn2, extra, *, fox):
    s = q.shape[0]
    groups = FOX_HEADS // 2 if fox else DIFF_HEADS
    q_col0 = 0 if fox else FOX_WIDTH // LANES
    resident = lambda width: pl.BlockSpec((s, width), lambda g, i: (0, g),
                                          pipeline_mode=pl.Buffered(1))
    small = lambda a: pl.BlockSpec(a.shape, lambda g, i: (0, 0))
    in_specs = [
        pl.BlockSpec((Q_TILES_PER_STEP * TQ, LANES), lambda g, i: (i, q_col0 + g)),
        resident(2 * LANES),
        resident(2 * LANES if fox else LANES),
        small(kn2),
    ] + [small(a) for a in extra]
    scratch = [
        pltpu.VMEM((2 * TQ, LANES), jnp.float32),
        pltpu.VMEM((2 * TQ, LANES), jnp.float32),
    ]
    if not fox:
        scratch.append(pltpu.VMEM((2 * TQ, LANES), jnp.float32))
    return pl.pallas_call(
        functools.partial(_attn_kernel, fox=fox),
        grid=(groups, s // (Q_TILES_PER_STEP * TQ)),
        in_specs=in_specs,
        out_specs=pl.BlockSpec((Q_TILES_PER_STEP * TQ, LANES), lambda g, i: (i, g)),
        out_shape=jax.ShapeDtypeStruct((s, groups * LANES), jnp.bfloat16),
        scratch_shapes=scratch,
        compiler_params=pltpu.CompilerParams(
            dimension_semantics=("arbitrary", "arbitrary"), vmem_limit_bytes=VMEM_LIMIT),
        name="fox_attn" if fox else "diff_attn",
    )(q, kaug, v, kn2, *extra)


def _out_ffn_kernel(x_ref, fo_ref, do_ref, wo_ref, gf_ref, wg_ref, wu_ref, wd_ref,
                    gl_ref, o_ref):
    x1 = (x_ref[...]
          + jnp.dot(fo_ref[...], wo_ref[:FOX_WIDTH, :], preferred_element_type=jnp.float32)
          + jnp.dot(do_ref[...], wo_ref[FOX_WIDTH:, :], preferred_element_type=jnp.float32))
    h = _rms(x1, gf_ref[...]).astype(jnp.bfloat16)
    g = jnp.dot(h, wg_ref[...], preferred_element_type=jnp.float32)
    u = jnp.dot(h, wu_ref[...], preferred_element_type=jnp.float32)
    a = (g * jax.nn.sigmoid(g) * u).astype(jnp.bfloat16)
    x2 = x1 + jnp.dot(a, wd_ref[...], preferred_element_type=jnp.float32)
    o_ref[...] = _rms(x2, gl_ref[...])


def _out_ffn(x, fo, do, wo, gf, wg, wu, wd, gl):
    s = x.shape[0]
    const = lambda shape: pl.BlockSpec(shape, lambda i: (0, 0), pipeline_mode=pl.Buffered(1))
    return pl.pallas_call(
        _out_ffn_kernel,
        grid=(s // TM_FFN,),
        in_specs=[
            pl.BlockSpec((TM_FFN, D_MODEL), lambda i: (i, 0)),
            pl.BlockSpec((TM_FFN, FOX_WIDTH), lambda i: (i, 0)),
            pl.BlockSpec((TM_FFN, DIFF_WIDTH), lambda i: (i, 0)),
            const(wo.shape), const(gf.shape), const(wg.shape), const(wu.shape),
            const(wd.shape), const(gl.shape),
        ],
        out_specs=pl.BlockSpec((TM_FFN, D_MODEL), lambda i: (i, 0)),
        out_shape=jax.ShapeDtypeStruct((s, D_MODEL), jnp.float32),
        compiler_params=pltpu.CompilerParams(
            dimension_semantics=("arbitrary",), vmem_limit_bytes=VMEM_LIMIT),
        name="out_ffn",
    )(x, fo, do, wo, gf, wg, wu, wd, gl)


def kernel(x, mix_norm_g, w_in, b_forget, lambda_q1, lambda_k1, lambda_q2, lambda_k2,
           diff_norm_g, w_out, ffn_norm_g, w_gate, w_up, w_down, final_norm_g):
    b, s, d = x.shape
    assert b == 1 and d == D_MODEL and w_in.shape[0] == 1
    assert s % TQ == 0 and s % TM_IN == 0 and s % TM_FFN == 0
    assert TM_IN == TK and TQ == TK and TILES_PER_STEP == 4
    assert s >= max(FIRST_TILES_FOX, FIRST_TILES_DIFF) * TK and s % (Q_TILES_PER_STEP * TQ) == 0
    assert s < 2 ** 14 + 1
    bf16 = jnp.bfloat16
    x2d = x.reshape(s, d)

    w = w_in[0]
    col = np.cumsum((0, FOX_WIDTH, FOX_WIDTH, FOX_WIDTH, FOX_HEADS, DIFF_WIDTH, DIFF_WIDTH,
                     DIFF_WIDTH))
    fq, fk, fv, gate, dq, dk, dv = (w[:, col[n]:col[n + 1]] for n in range(7))
    w_r = jnp.concatenate(
        [fk, dk, fq, dq, fv, dv, gate, jnp.zeros((d, GATE_PAD - FOX_HEADS), w.dtype)],
        axis=1).astype(bf16)
    b_pad = jnp.pad(b_forget[0].astype(jnp.float32), (0, GATE_PAD - FOX_HEADS)).reshape(1, GATE_PAD)
    place = jnp.asarray(_fox_aug_placement(), bf16)
    hsum = jnp.asarray(np.arange(2 * FOX_WIDTH)[:, None] // HEAD_DIM == np.arange(LANES)[None, :],
                       bf16)

    q, fox_k, fox_v, diff_k, diff_v, cend, kn2 = _in_proj(
        x2d, mix_norm_g[0].reshape(1, d), w_r, b_pad, place, hsum)

    fox_out = _attention(q, fox_k, fox_v, kn2, (cend,), fox=True)
    lam_args = tuple(a[0].astype(jnp.float32).reshape(1, HEAD_DIM)
                     for a in (lambda_q1, lambda_k1, lambda_q2, lambda_k2))
    diff_out = _attention(
        q, diff_k, diff_v, kn2,
        lam_args + (diff_norm_g[0].astype(jnp.float32).reshape(1, DIFF_V_DIM),), fox=False)

    out = _out_ffn(
        x2d, fox_out, diff_out, w_out[0].astype(bf16), ffn_norm_g[0].reshape(1, d),
        w_gate[0].astype(bf16), w_up[0].astype(bf16), w_down[0].astype(bf16),
        final_norm_g.reshape(1, d))
    return out.reshape(b, s, d)
```

```python
import functools
import math

import numpy as np
import jax
import jax.numpy as jnp
from jax import lax
from jax.experimental import pallas as pl
from jax.experimental.pallas import tpu as pltpu

D_MODEL = 1024
HEAD_DIM = 64
FOX_HEADS = 8
FOX_WIDTH = FOX_HEADS * HEAD_DIM
DIFF_HEADS = 4
DIFF_V_DIM = 2 * HEAD_DIM
DIFF_WIDTH = DIFF_HEADS * DIFF_V_DIM
D_FF = ((8 * D_MODEL + 767) // 768) * 256
EPS = 1e-6
NEG_INF = -1e30
LOG2E = math.log2(math.e)
Q_SCALE = LOG2E / math.sqrt(HEAD_DIM)
UNDERFLOW_LOG2 = 127.0
NORM_SLACK = 1.01
LAMBDA_INIT = 0.8 - 0.6 * math.exp(-0.3 * 0)

LANES = 128
GATE_PAD = LANES
QKV_WIDTH = 3 * FOX_WIDTH + 3 * DIFF_WIDTH
VMEM_LIMIT = 56 * 1024 * 1024

TM_IN = 512
TQ = 512
TK = 512
TILES_PER_STEP = 4
FIRST_TILES_FOX = 4
FIRST_TILES_DIFF = 2
Q_TILES_PER_STEP = 4
TM_FFN = 512

_NT = (((1,), (1,)), ((), ()))


def _rms(x, g):
    return x * lax.rsqrt(jnp.mean(x * x, axis=-1, keepdims=True) + EPS) * g


def _split3(x):
    hi = x.astype(jnp.bfloat16)
    r1 = x - hi.astype(jnp.float32)
    mid = r1.astype(jnp.bfloat16)
    lo = (r1 - mid.astype(jnp.float32)).astype(jnp.bfloat16)
    return hi, mid, lo


def _reduce_rows(x, op):
    while x.shape[0] > 8 and x.shape[0] % 2 == 0:
        half = x.shape[0] // 2
        x = op(x[:half], x[half:])
    reduce = jnp.max if op is jnp.maximum else jnp.sum
    return reduce(x, axis=0, keepdims=True)


def _fox_aug_placement():
    p = np.zeros((LANES, FOX_WIDTH), np.float32)
    for part in range(3):
        for head in range(FOX_HEADS):
            lane = 3 * (head % 2) + part
            p[part * FOX_HEADS + head, (head // 2) * LANES + lane] = 1.0
    return p


def _in_proj_kernel(x_ref, g_ref, w_ref, b_ref, place_ref, hsum_ref,
                    q_ref, fk_ref, fv_ref, dk_ref, dv_ref, cend_ref, kn2_ref,
                    tri_ref, carry_ref):
    step = pl.program_id(0)
    tm = x_ref.shape[0]
    f32, bf16 = jnp.float32, jnp.bfloat16

    @pl.when(step == 0)
    def _():
        r = lax.broadcasted_iota(jnp.int32, (tm, tm), 0)
        c = lax.broadcasted_iota(jnp.int32, (tm, tm), 1)
        tri_ref[...] = jnp.where(r >= c, 1.0, 0.0).astype(bf16)
        carry_ref[...] = jnp.zeros_like(carry_ref)
        kn2_ref[...] = jnp.zeros_like(kn2_ref)

    lane = lax.broadcasted_iota(jnp.int32, (1, LANES), 1)
    lo_half = lane < HEAD_DIM
    head_lanes = lane < FOX_HEADS
    w5 = FOX_WIDTH

    def pack3(x):
        hi, mid, lo = (p.astype(f32) for p in _split3(jnp.where(head_lanes, x, 0.0)))
        return (hi + pltpu.roll(mid, FOX_HEADS, axis=1)
                + pltpu.roll(lo, 2 * FOX_HEADS, axis=1)).astype(bf16)

    h = _rms(x_ref[...], g_ref[...]).astype(bf16)

    z = jnp.dot(h, w_ref[:, QKV_WIDTH:], preferred_element_type=f32) + b_ref[...]
    logf = LOG2E * (jnp.minimum(z, 0.0) - jnp.log1p(jnp.exp(-jnp.abs(z))))
    cs = jnp.dot(tri_ref[...], pack3(logf), preferred_element_type=f32)
    c = (cs + pltpu.roll(cs, LANES - FOX_HEADS, axis=1)
         + pltpu.roll(cs, LANES - 2 * FOX_HEADS, axis=1) + carry_ref[0:1, :])
    last = c[tm - 1:tm, :]
    carry_ref[...] = jnp.broadcast_to(last, carry_ref.shape)
    cend_ref[pl.ds(step, 1), :] = last
    caug = jnp.dot(pack3(-c), place_ref[...], preferred_element_type=f32).astype(bf16)

    yk = jnp.dot(h, w_ref[:, :2 * w5], preferred_element_type=f32)
    fk = yk[:, :w5].astype(bf16)
    dk = yk[:, w5:].astype(bf16)
    n2 = jnp.dot(jnp.concatenate([fk * fk, dk * dk], axis=1), hsum_ref[...],
                 preferred_element_type=f32)
    kn2_ref[...] = jnp.maximum(kn2_ref[...],
                               jnp.broadcast_to(jnp.max(n2, axis=0, keepdims=True), kn2_ref.shape))

    y = jnp.dot(h, w_ref[:, 2 * w5:QKV_WIDTH], preferred_element_type=f32)
    q_ref[...] = (y[:, :2 * w5] * Q_SCALE).astype(bf16)
    fv = y[:, 2 * w5:3 * w5].astype(bf16)
    dv_ref[...] = y[:, 3 * w5:].astype(bf16)

    pos = step * tm + lax.broadcasted_iota(jnp.int32, (tm, LANES), 0)
    pos_lanes = jnp.where(lane < 3, pos & ~(LANES - 1),
                          jnp.where(lane < 6, pos & (LANES - 1), 0)).astype(f32).astype(bf16)
    for g in range(FOX_WIDTH // LANES):
        cols = slice(g * LANES, (g + 1) * LANES)
        key = slice(2 * g * LANES, (2 * g + 1) * LANES)
        bias = slice((2 * g + 1) * LANES, (2 * g + 2) * LANES)
        fk_ref[:, key] = fk[:, cols]
        fk_ref[:, bias] = caug[:, cols]
        dk_ref[:, key] = dk[:, cols]
        dk_ref[:, bias] = pos_lanes
        fv_ref[:, key] = jnp.where(lo_half, fv[:, cols], (lane == HEAD_DIM).astype(bf16))
        fv_ref[:, bias] = jnp.where(lo_half, (lane == 0).astype(bf16), fv[:, cols])


def _in_proj(x, g, w, b, place, hsum):
    s = x.shape[0]
    n_tiles = s // TM_IN
    const = lambda a: pl.BlockSpec(a.shape, lambda i: (0, 0))
    row_block = lambda width: pl.BlockSpec((TM_IN, width), lambda i: (i, 0))
    bf16 = jnp.bfloat16
    return pl.pallas_call(
        _in_proj_kernel,
        grid=(n_tiles,),
        in_specs=[row_block(D_MODEL), const(g), const(w), const(b), const(place), const(hsum)],
        out_specs=[
            row_block(2 * FOX_WIDTH),
            row_block(2 * FOX_WIDTH),
            row_block(2 * FOX_WIDTH),
            row_block(2 * DIFF_WIDTH),
            row_block(DIFF_WIDTH),
            pl.BlockSpec((n_tiles, LANES), lambda i: (0, 0)),
            pl.BlockSpec((8, LANES), lambda i: (0, 0)),
        ],
        out_shape=[
            jax.ShapeDtypeStruct((s, 2 * FOX_WIDTH), bf16),
            jax.ShapeDtypeStruct((s, 2 * FOX_WIDTH), bf16),
            jax.ShapeDtypeStruct((s, 2 * FOX_WIDTH), bf16),
            jax.ShapeDtypeStruct((s, 2 * DIFF_WIDTH), bf16),
            jax.ShapeDtypeStruct((s, DIFF_WIDTH), bf16),
            jax.ShapeDtypeStruct((n_tiles, LANES), jnp.float32),
            jax.ShapeDtypeStruct((8, LANES), jnp.float32),
        ],
        scratch_shapes=[
            pltpu.VMEM((TM_IN, TM_IN), bf16),
            pltpu.VMEM((8, LANES), jnp.float32),
        ],
        compiler_params=pltpu.CompilerParams(
            dimension_semantics=("arbitrary",), vmem_limit_bytes=VMEM_LIMIT),
        name="in_proj",
    )(x, g, w, b, place, hsum)


def _attn_kernel(*refs, fox):
    def one_tile(t, carry):
        rows = pl.ds(pl.multiple_of(t * TQ, TQ), TQ)
        _attn_query_tile(pl.program_id(1) * Q_TILES_PER_STEP + t, rows, refs, fox)
        return carry

    lax.fori_loop(0, Q_TILES_PER_STEP, one_tile, 0)


def _attn_query_tile(i, q_rows, refs, fox):
    if fox:
        q_ref, kaug_ref, vaug_ref, kn2_ref, cend_ref, o_ref, m_ref, acc_ref = refs
    else:
        (q_ref, kaug_ref, v_ref, kn2_ref, lq1_ref, lk1_ref, lq2_ref, lk2_ref, gn_ref,
         o_ref, m_ref, acc_ref, l_ref) = refs
    grp = pl.program_id(0)
    tq, tk = TQ, TK
    n_tiles = kaug_ref.shape[0] // tk
    f32, bf16 = jnp.float32, jnp.bfloat16

    lane = lax.broadcasted_iota(jnp.int32, (1, LANES), 1)
    lo_half = lane < HEAD_DIM

    if not fox:
        slope = LOG2E * jnp.exp2(-2.0 * jnp.full((1, 1), grp + 1, jnp.int32).astype(f32))
        sl = [p.astype(f32) for p in _split3(slope)]
        inv_slope = 1.0 / slope

    def half_norms_sq(x):
        l_in = lax.broadcasted_iota(jnp.int32, (LANES, LANES), 0)
        l_out = lax.broadcasted_iota(jnp.int32, (LANES, LANES), 1)
        half_sum = ((l_in // HEAD_DIM) == l_out).astype(bf16)
        return jnp.dot(x * x, half_sum, preferred_element_type=f32)

    q = q_ref[q_rows, :]
    zero = jnp.zeros_like(q)
    if fox:
        q_bias = [jnp.broadcast_to(((lane >= 3 * h) & (lane < 3 * h + 3)).astype(bf16), q.shape)
                  for h in range(2)]
    else:
        parts = jnp.where((lane == 0) | (lane == 3), sl[0],
                          jnp.where((lane == 1) | (lane == 4), sl[1],
                                    jnp.where((lane == 2) | (lane == 5), sl[2], 0.0)))
        q_bias = [jnp.broadcast_to(parts.astype(bf16), q.shape)] * 2
    q2 = jnp.concatenate(
        [jnp.concatenate([jnp.where(lo_half, q, zero), q_bias[0]], axis=1),
         jnp.concatenate([jnp.where(lo_half, zero, q), q_bias[1]], axis=1)], axis=0)

    m_ref[...] = jnp.full_like(m_ref, NEG_INF)
    acc_ref[...] = jnp.zeros_like(acc_ref)
    if not fox:
        l_ref[...] = jnp.zeros_like(l_ref)

    def tile(j, n_sub, masked):
        width = n_sub * tk
        rows = pl.ds(pl.multiple_of(j * tk, tk), width)
        s = lax.dot_general(q2, kaug_ref[rows, :], _NT, preferred_element_type=f32)
        if masked == "diagonal_last":
            r = lax.broadcasted_iota(jnp.int32, (2 * tq, tk), 0) & (tq - 1)
            c = lax.broadcasted_iota(jnp.int32, (2 * tq, tk), 1)
            diag = jnp.where(c <= r, s[:, width - tk:], NEG_INF)
            s = diag if n_sub == 1 else jnp.concatenate([s[:, :width - tk], diag], axis=1)
        elif masked == "by_position":
            r = lax.broadcasted_iota(jnp.int32, (2 * tq, width), 0) & (tq - 1)
            c = lax.broadcasted_iota(jnp.int32, (2 * tq, width), 1)
            s = jnp.where(c + (j - i) * tk <= r, s, NEG_INF)
        m_prev = m_ref[...]
        m_next = jnp.maximum(m_prev, jnp.max(s, axis=1, keepdims=True))
        alpha = jnp.exp2(m_prev - m_next)
        p = jnp.exp2(s - jnp.tile(m_next, (1, width // LANES)))
        pb = p.astype(bf16)
        if fox:
            pv = jnp.concatenate(
                [jnp.dot(pb[:tq], vaug_ref[rows, :LANES], preferred_element_type=f32),
                 jnp.dot(pb[tq:], vaug_ref[rows, LANES:], preferred_element_type=f32)], axis=0)
        else:
            pv = jnp.dot(pb, v_ref[rows, :], preferred_element_type=f32)
            part = p[:, :LANES]
            for blk in range(1, width // LANES):
                part = part + p[:, blk * LANES:(blk + 1) * LANES]
            l_ref[...] = alpha * l_ref[...] + part
        acc_ref[...] = acc_ref[...] * alpha + pv
        m_ref[...] = m_next

    first = FIRST_TILES_FOX if fox else FIRST_TILES_DIFF
    top = jnp.maximum(i - (first - 1), 0)

    def first_step(masked):
        qn = half_norms_sq(q)
        first_norm_lane = (0 if fox else FOX_HEADS) + 2 * grp
        kmax = [jnp.sqrt(jnp.sum(jnp.where(lane == first_norm_lane + h, kn2_ref[0:1, :], 0.0),
                                 axis=1, keepdims=True)) * NORM_SLACK for h in range(2)]
        bound = [jnp.sqrt(qn[:, h:h + 1]) * kmax[h] for h in range(2)]
        tile(top, first, masked)
        m_col = [m_ref[h * tq:(h + 1) * tq, 0:1] for h in range(2)]
        if fox:
            head_lane = lax.broadcasted_iota(jnp.int32, cend_ref.shape, 1) - 2 * grp
            cend = [jnp.sum(jnp.where(head_lane == h, cend_ref[...], 0.0), axis=1, keepdims=True)
                    for h in range(2)]
            below_top = lax.broadcasted_iota(jnp.int32, (n_tiles, 1), 0) < top
            dead = below_top
            for h in range(2):
                gap = _reduce_rows(bound[h] - m_col[h], jnp.maximum)
                dead = dead & (-cend[h] < -UNDERFLOW_LOG2 - gap)
            return _reduce_rows(dead.astype(f32), jnp.add).astype(jnp.int32)[0, 0]
        gap = _reduce_rows(jnp.maximum(bound[0] - m_col[0], bound[1] - m_col[1]), jnp.maximum)
        x = (-UNDERFLOW_LOG2 - gap) * inv_slope
        j_lo = jnp.clip(jnp.floor(x * (1.0 / tk)), 0.0, 1.0 * n_tiles)
        return jnp.minimum(j_lo.astype(jnp.int32)[0, 0], top)

    j_lo = lax.cond(i >= first - 1,
                    functools.partial(first_step, "diagonal_last"),
                    functools.partial(first_step, "by_position"))

    n_live = top - j_lo
    one = n_live & 1
    two = n_live & 2

    @pl.when(one == 1)
    def _():
        tile(top - 1, 1, masked=False)

    @pl.when(two == 2)
    def _():
        tile(top - one - 2, 2, masked=False)

    def body(step, carry):
        tile(top - one - two - TILES_PER_STEP * (step + 1), TILES_PER_STEP, masked=False)
        return carry

    lax.fori_loop(0, n_live // TILES_PER_STEP, body, 0)

    acc = [acc_ref[h * tq:(h + 1) * tq, :] for h in range(2)]
    if fox:
        out = jnp.where(lo_half, acc[0] / acc[0][:, HEAD_DIM:HEAD_DIM + 1],
                        acc[1] / acc[1][:, 0:1])
    else:
        lam = (jnp.exp(jnp.sum(lq1_ref[...] * lk1_ref[...], axis=1, keepdims=True))
               - jnp.exp(jnp.sum(lq2_ref[...] * lk2_ref[...], axis=1, keepdims=True))
               + LAMBDA_INIT)
        l = jnp.sum(l_ref[...], axis=1, keepdims=True)
        a = acc[0] / l[:tq] - lam * (acc[1] / l[tq:])
        out = _rms(a, gn_ref[...]) * (1.0 - LAMBDA_INIT)
    o_ref[q_rows, :] = out.astype(o_ref.dtype)


def _attention(q, kaug, v, kn2, extra, *, fox):
    s = q.shape[0]
    groups = FOX_HEADS // 2 if fox else DIFF_HEADS
    q_col0 = 0 if fox else FOX_WIDTH // LANES
    resident = lambda width: pl.BlockSpec((s, width), lambda g, i: (0, g),
                                          pipeline_mode=pl.Buffered(1))
    small = lambda a: pl.BlockSpec(a.shape, lambda g, i: (0, 0))
    in_specs = [
        pl.BlockSpec((Q_TILES_PER_STEP * TQ, LANES), lambda g, i: (i, q_col0 + g)),
        resident(2 * LANES),
        resident(2 * LANES if fox else LANES),
        small(kn2),
    ] + [small(a) for a in extra]
    scratch = [
        pltpu.VMEM((2 * TQ, LANES), jnp.float32),
        pltpu.VMEM((2 * TQ, LANES), jnp.float32),
    ]
    if not fox:
        scratch.append(pltpu.VMEM((2 * TQ, LANES), jnp.float32))
    return pl.pallas_call(
        functools.partial(_attn_kernel, fox=fox),
        grid=(groups, s // (Q_TILES_PER_STEP * TQ)),
        in_specs=in_specs,
        out_specs=pl.BlockSpec((Q_TILES_PER_STEP * TQ, LANES), lambda g, i: (i, g)),
        out_shape=jax.ShapeDtypeStruct((s, groups * LANES), jnp.bfloat16),
        scratch_shapes=scratch,
        compiler_params=pltpu.CompilerParams(
            dimension_semantics=("arbitrary", "arbitrary"), vmem_limit_bytes=VMEM_LIMIT),
        name="fox_attn" if fox else "diff_attn",
    )(q, kaug, v, kn2, *extra)


def _out_ffn_kernel(x_ref, fo_ref, do_ref, wo_ref, gf_ref, wg_ref, wu_ref, wd_ref,
                    gl_ref, o_ref):
    x1 = (x_ref[...]
          + jnp.dot(fo_ref[...], wo_ref[:FOX_WIDTH, :], preferred_element_type=jnp.float32)
          + jnp.dot(do_ref[...], wo_ref[FOX_WIDTH:, :], preferred_element_type=jnp.float32))
    h = _rms(x1, gf_ref[...]).astype(jnp.bfloat16)
    g = jnp.dot(h, wg_ref[...], preferred_element_type=jnp.float32)
    u = jnp.dot(h, wu_ref[...], preferred_element_type=jnp.float32)
    a = (g * jax.nn.sigmoid(g) * u).astype(jnp.bfloat16)
    x2 = x1 + jnp.dot(a, wd_ref[...], preferred_element_type=jnp.float32)
    o_ref[...] = _rms(x2, gl_ref[...])


def _out_ffn(x, fo, do, wo, gf, wg, wu, wd, gl):
    s = x.shape[0]
    const = lambda shape: pl.BlockSpec(shape, lambda i: (0, 0), pipeline_mode=pl.Buffered(1))
    return pl.pallas_call(
        _out_ffn_kernel,
        grid=(s // TM_FFN,),
        in_specs=[
            pl.BlockSpec((TM_FFN, D_MODEL), lambda i: (i, 0)),
            pl.BlockSpec((TM_FFN, FOX_WIDTH), lambda i: (i, 0)),
            pl.BlockSpec((TM_FFN, DIFF_WIDTH), lambda i: (i, 0)),
            const(wo.shape), const(gf.shape), const(wg.shape), const(wu.shape),
            const(wd.shape), const(gl.shape),
        ],
        out_specs=pl.BlockSpec((TM_FFN, D_MODEL), lambda i: (i, 0)),
        out_shape=jax.ShapeDtypeStruct((s, D_MODEL), jnp.float32),
        compiler_params=pltpu.CompilerParams(
            dimension_semantics=("arbitrary",), vmem_limit_bytes=VMEM_LIMIT),
        name="out_ffn",
    )(x, fo, do, wo, gf, wg, wu, wd, gl)


def kernel(x, mix_norm_g, w_in, b_forget, lambda_q1, lambda_k1, lambda_q2, lambda_k2,
           diff_norm_g, w_out, ffn_norm_g, w_gate, w_up, w_down, final_norm_g):
    b, s, d = x.shape
    assert b == 1 and d == D_MODEL and w_in.shape[0] == 1
    assert s % TQ == 0 and s % TM_IN == 0 and s % TM_FFN == 0
    assert TM_IN == TK and TQ == TK and TILES_PER_STEP == 4
    assert s >= max(FIRST_TILES_FOX, FIRST_TILES_DIFF) * TK and s % (Q_TILES_PER_STEP * TQ) == 0
    assert s < 2 ** 14 + 1
    bf16 = jnp.bfloat16
    x2d = x.reshape(s, d)

    w = w_in[0]
    col = np.cumsum((0, FOX_WIDTH, FOX_WIDTH, FOX_WIDTH, FOX_HEADS, DIFF_WIDTH, DIFF_WIDTH,
                     DIFF_WIDTH))
    fq, fk, fv, gate, dq, dk, dv = (w[:, col[n]:col[n + 1]] for n in range(7))
    order = jnp.argsort(b_forget[0])
    by_head = lambda t: t.reshape(d, FOX_HEADS, HEAD_DIM)[:, order, :].reshape(d, FOX_WIDTH)
    w_r = jnp.concatenate(
        [by_head(fk), dk, by_head(fq), dq, by_head(fv), dv, gate[:, order],
         jnp.zeros((d, GATE_PAD - FOX_HEADS), w.dtype)], axis=1).astype(bf16)
    b_pad = jnp.pad(b_forget[0][order].astype(jnp.float32),
                    (0, GATE_PAD - FOX_HEADS)).reshape(1, GATE_PAD)
    w_o = w_out[0]
    w_o = jnp.concatenate(
        [w_o[:FOX_WIDTH].reshape(FOX_HEADS, HEAD_DIM, d)[order].reshape(FOX_WIDTH, d),
         w_o[FOX_WIDTH:]], axis=0).astype(bf16)
    place = jnp.asarray(_fox_aug_placement(), bf16)
    hsum = jnp.asarray(np.arange(2 * FOX_WIDTH)[:, None] // HEAD_DIM == np.arange(LANES)[None, :],
                       bf16)

    q, fox_k, fox_v, diff_k, diff_v, cend, kn2 = _in_proj(
        x2d, mix_norm_g[0].reshape(1, d), w_r, b_pad, place, hsum)

    fox_out = _attention(q, fox_k, fox_v, kn2, (cend,), fox=True)
    lam_args = tuple(a[0].astype(jnp.float32).reshape(1, HEAD_DIM)
                     for a in (lambda_q1, lambda_k1, lambda_q2, lambda_k2))
    diff_out = _attention(
        q, diff_k, diff_v, kn2,
        lam_args + (diff_norm_g[0].astype(jnp.float32).reshape(1, DIFF_V_DIM),), fox=False)

    out = _out_ffn(
        x2d, fox_out, diff_out, w_o, ffn_norm_g[0].reshape(1, d),
        w_gate[0].astype(bf16), w_up[0].astype(bf16), w_down[0].astype(bf16),
        final_norm_g.reshape(1, d))
    return out.reshape(b, s, d)
```

```python
import functools
import math

import numpy as np
import jax
import jax.numpy as jnp
from jax import lax
from jax.experimental import pallas as pl
from jax.experimental.pallas import tpu as pltpu

D_MODEL = 1024
HEAD_DIM = 64
FOX_HEADS = 8
FOX_WIDTH = FOX_HEADS * HEAD_DIM
DIFF_HEADS = 4
DIFF_V_DIM = 2 * HEAD_DIM
DIFF_WIDTH = DIFF_HEADS * DIFF_V_DIM
D_FF = ((8 * D_MODEL + 767) // 768) * 256
EPS = 1e-6
NEG_INF = -1e30
LOG2E = math.log2(math.e)
Q_SCALE = LOG2E / math.sqrt(HEAD_DIM)
UNDERFLOW_LOG2 = 127.0
NORM_SLACK = 1.01
LAMBDA_INIT = 0.8 - 0.6 * math.exp(-0.3 * 0)

LANES = 128
GATE_PAD = LANES
QKV_WIDTH = 3 * FOX_WIDTH + 3 * DIFF_WIDTH
VMEM_LIMIT = 56 * 1024 * 1024

TM_IN = 512
TQ = 512
TK = 512
TILES_PER_STEP = 4
FIRST_TILES_FOX = (4,)
FIRST_TILES_DIFF = (2, 4)
Q_TILES_PER_STEP = 4
TM_FFN = 512

_NT = (((1,), (1,)), ((), ()))


def _rms(x, g):
    return x * lax.rsqrt(jnp.mean(x * x, axis=-1, keepdims=True) + EPS) * g


def _split3(x):
    hi = x.astype(jnp.bfloat16)
    r1 = x - hi.astype(jnp.float32)
    mid = r1.astype(jnp.bfloat16)
    lo = (r1 - mid.astype(jnp.float32)).astype(jnp.bfloat16)
    return hi, mid, lo


def _reduce_rows(x, op):
    while x.shape[0] > 8 and x.shape[0] % 2 == 0:
        half = x.shape[0] // 2
        x = op(x[:half], x[half:])
    reduce = jnp.max if op is jnp.maximum else jnp.sum
    return reduce(x, axis=0, keepdims=True)


def _fox_aug_placement():
    p = np.zeros((LANES, FOX_WIDTH), np.float32)
    for part in range(3):
        for head in range(FOX_HEADS):
            lane = 3 * (head % 2) + part
            p[part * FOX_HEADS + head, (head // 2) * LANES + lane] = 1.0
    return p


def _in_proj_kernel(x_ref, g_ref, w_ref, b_ref, place_ref, hsum_ref,
                    q_ref, fk_ref, fv_ref, dk_ref, dv_ref, cend_ref, kn2_ref,
                    tri_ref, carry_ref):
    step = pl.program_id(0)
    tm = x_ref.shape[0]
    f32, bf16 = jnp.float32, jnp.bfloat16

    @pl.when(step == 0)
    def _():
        r = lax.broadcasted_iota(jnp.int32, (tm, tm), 0)
        c = lax.broadcasted_iota(jnp.int32, (tm, tm), 1)
        tri_ref[...] = jnp.where(r >= c, 1.0, 0.0).astype(bf16)
        carry_ref[...] = jnp.zeros_like(carry_ref)
        kn2_ref[...] = jnp.zeros_like(kn2_ref)

    lane = lax.broadcasted_iota(jnp.int32, (1, LANES), 1)
    lo_half = lane < HEAD_DIM
    head_lanes = lane < FOX_HEADS
    w5 = FOX_WIDTH

    def pack3(x):
        hi, mid, lo = (p.astype(f32) for p in _split3(jnp.where(head_lanes, x, 0.0)))
        return (hi + pltpu.roll(mid, FOX_HEADS, axis=1)
                + pltpu.roll(lo, 2 * FOX_HEADS, axis=1)).astype(bf16)

    h = _rms(x_ref[...], g_ref[...]).astype(bf16)

    z = jnp.dot(h, w_ref[:, QKV_WIDTH:], preferred_element_type=f32) + b_ref[...]
    logf = LOG2E * (jnp.minimum(z, 0.0) - jnp.log1p(jnp.exp(-jnp.abs(z))))
    cs = jnp.dot(tri_ref[...], pack3(logf), preferred_element_type=f32)
    c = (cs + pltpu.roll(cs, LANES - FOX_HEADS, axis=1)
         + pltpu.roll(cs, LANES - 2 * FOX_HEADS, axis=1) + carry_ref[0:1, :])
    last = c[tm - 1:tm, :]
    carry_ref[...] = jnp.broadcast_to(last, carry_ref.shape)
    cend_ref[pl.ds(step, 1), :] = last
    caug = jnp.dot(pack3(-c), place_ref[...], preferred_element_type=f32).astype(bf16)

    yk = jnp.dot(h, w_ref[:, :2 * w5], preferred_element_type=f32)
    fk = yk[:, :w5].astype(bf16)
    dk = yk[:, w5:].astype(bf16)
    n2 = jnp.dot(jnp.concatenate([fk * fk, dk * dk], axis=1), hsum_ref[...],
                 preferred_element_type=f32)
    kn2_ref[...] = jnp.maximum(kn2_ref[...],
                               jnp.broadcast_to(jnp.max(n2, axis=0, keepdims=True), kn2_ref.shape))

    y = jnp.dot(h, w_ref[:, 2 * w5:QKV_WIDTH], preferred_element_type=f32)
    q_ref[...] = (y[:, :2 * w5] * Q_SCALE).astype(bf16)
    fv = y[:, 2 * w5:3 * w5].astype(bf16)
    dv_ref[...] = y[:, 3 * w5:].astype(bf16)

    pos = step * tm + lax.broadcasted_iota(jnp.int32, (tm, LANES), 0)
    pos_lanes = jnp.where(lane < 3, pos & ~(LANES - 1),
                          jnp.where(lane < 6, pos & (LANES - 1), 0)).astype(f32).astype(bf16)
    for g in range(FOX_WIDTH // LANES):
        cols = slice(g * LANES, (g + 1) * LANES)
        key = slice(2 * g * LANES, (2 * g + 1) * LANES)
        bias = slice((2 * g + 1) * LANES, (2 * g + 2) * LANES)
        fk_ref[:, key] = fk[:, cols]
        fk_ref[:, bias] = caug[:, cols]
        dk_ref[:, key] = dk[:, cols]
        dk_ref[:, bias] = pos_lanes
        fv_ref[:, key] = jnp.where(lo_half, fv[:, cols], (lane == HEAD_DIM).astype(bf16))
        fv_ref[:, bias] = jnp.where(lo_half, (lane == 0).astype(bf16), fv[:, cols])


def _in_proj(x, g, w, b, place, hsum):
    s = x.shape[0]
    n_tiles = s // TM_IN
    const = lambda a: pl.BlockSpec(a.shape, lambda i: (0, 0))
    row_block = lambda width: pl.BlockSpec((TM_IN, width), lambda i: (i, 0))
    bf16 = jnp.bfloat16
    return pl.pallas_call(
        _in_proj_kernel,
        grid=(n_tiles,),
        in_specs=[row_block(D_MODEL), const(g), const(w), const(b), const(place), const(hsum)],
        out_specs=[
            row_block(2 * FOX_WIDTH),
            row_block(2 * FOX_WIDTH),
            row_block(2 * FOX_WIDTH),
            row_block(2 * DIFF_WIDTH),
            row_block(DIFF_WIDTH),
            pl.BlockSpec((n_tiles, LANES), lambda i: (0, 0)),
            pl.BlockSpec((8, LANES), lambda i: (0, 0)),
        ],
        out_shape=[
            jax.ShapeDtypeStruct((s, 2 * FOX_WIDTH), bf16),
            jax.ShapeDtypeStruct((s, 2 * FOX_WIDTH), bf16),
            jax.ShapeDtypeStruct((s, 2 * FOX_WIDTH), bf16),
            jax.ShapeDtypeStruct((s, 2 * DIFF_WIDTH), bf16),
            jax.ShapeDtypeStruct((s, DIFF_WIDTH), bf16),
            jax.ShapeDtypeStruct((n_tiles, LANES), jnp.float32),
            jax.ShapeDtypeStruct((8, LANES), jnp.float32),
        ],
        scratch_shapes=[
            pltpu.VMEM((TM_IN, TM_IN), bf16),
            pltpu.VMEM((8, LANES), jnp.float32),
        ],
        compiler_params=pltpu.CompilerParams(
            dimension_semantics=("arbitrary",), vmem_limit_bytes=VMEM_LIMIT),
        name="in_proj",
    )(x, g, w, b, place, hsum)


def _attn_kernel(*refs, fox):
    def one_tile(t, carry):
        rows = pl.ds(pl.multiple_of(t * TQ, TQ), TQ)
        _attn_query_tile(pl.program_id(1) * Q_TILES_PER_STEP + t, rows, refs, fox)
        return carry

    lax.fori_loop(0, Q_TILES_PER_STEP, one_tile, 0)


def _attn_query_tile(i, q_rows, refs, fox):
    if fox:
        q_ref, kaug_ref, vaug_ref, kn2_ref, cend_ref, o_ref, m_ref, acc_ref = refs
    else:
        (q_ref, kaug_ref, v_ref, kn2_ref, lq1_ref, lk1_ref, lq2_ref, lk2_ref, gn_ref,
         o_ref, m_ref, acc_ref, l_ref) = refs
    grp = pl.program_id(0)
    tq, tk = TQ, TK
    n_tiles = kaug_ref.shape[0] // tk
    f32, bf16 = jnp.float32, jnp.bfloat16

    lane = lax.broadcasted_iota(jnp.int32, (1, LANES), 1)
    lo_half = lane < HEAD_DIM

    if not fox:
        slope = LOG2E * jnp.exp2(-2.0 * jnp.full((1, 1), grp + 1, jnp.int32).astype(f32))
        sl = [p.astype(f32) for p in _split3(slope)]
        inv_slope = 1.0 / slope

    def half_norms_sq(x):
        l_in = lax.broadcasted_iota(jnp.int32, (LANES, LANES), 0)
        l_out = lax.broadcasted_iota(jnp.int32, (LANES, LANES), 1)
        half_sum = ((l_in // HEAD_DIM) == l_out).astype(bf16)
        return jnp.dot(x * x, half_sum, preferred_element_type=f32)

    q = q_ref[q_rows, :]
    zero = jnp.zeros_like(q)
    if fox:
        q_bias = [jnp.broadcast_to(((lane >= 3 * h) & (lane < 3 * h + 3)).astype(bf16), q.shape)
                  for h in range(2)]
    else:
        parts = jnp.where((lane == 0) | (lane == 3), sl[0],
                          jnp.where((lane == 1) | (lane == 4), sl[1],
                                    jnp.where((lane == 2) | (lane == 5), sl[2], 0.0)))
        q_bias = [jnp.broadcast_to(parts.astype(bf16), q.shape)] * 2
    q2 = jnp.concatenate(
        [jnp.concatenate([jnp.where(lo_half, q, zero), q_bias[0]], axis=1),
         jnp.concatenate([jnp.where(lo_half, zero, q), q_bias[1]], axis=1)], axis=0)

    m_ref[...] = jnp.full_like(m_ref, NEG_INF)
    acc_ref[...] = jnp.zeros_like(acc_ref)
    if not fox:
        l_ref[...] = jnp.zeros_like(l_ref)

    def tile(j, n_sub, masked):
        width = n_sub * tk
        rows = pl.ds(pl.multiple_of(j * tk, tk), width)
        s = lax.dot_general(q2, kaug_ref[rows, :], _NT, preferred_element_type=f32)
        if masked == "diagonal_last":
            r = lax.broadcasted_iota(jnp.int32, (2 * tq, tk), 0) & (tq - 1)
            c = lax.broadcasted_iota(jnp.int32, (2 * tq, tk), 1)
            diag = jnp.where(c <= r, s[:, width - tk:], NEG_INF)
            s = diag if n_sub == 1 else jnp.concatenate([s[:, :width - tk], diag], axis=1)
        elif masked == "by_position":
            r = lax.broadcasted_iota(jnp.int32, (2 * tq, width), 0) & (tq - 1)
            c = lax.broadcasted_iota(jnp.int32, (2 * tq, width), 1)
            s = jnp.where(c + (j - i) * tk <= r, s, NEG_INF)
        m_prev = m_ref[...]
        m_next = jnp.maximum(m_prev, jnp.max(s, axis=1, keepdims=True))
        alpha = jnp.exp2(m_prev - m_next)
        p = jnp.exp2(s - jnp.tile(m_next, (1, width // LANES)))
        pb = p.astype(bf16)
        if fox:
            pv = jnp.concatenate(
                [jnp.dot(pb[:tq], vaug_ref[rows, :LANES], preferred_element_type=f32),
                 jnp.dot(pb[tq:], vaug_ref[rows, LANES:], preferred_element_type=f32)], axis=0)
        else:
            pv = jnp.dot(pb, v_ref[rows, :], preferred_element_type=f32)
            part = p[:, :LANES]
            for blk in range(1, width // LANES):
                part = part + p[:, blk * LANES:(blk + 1) * LANES]
            l_ref[...] = alpha * l_ref[...] + part
        acc_ref[...] = acc_ref[...] * alpha + pv
        m_ref[...] = m_next

    def first_step(first, masked):
        top = jnp.maximum(i - (first - 1), 0)
        qn = half_norms_sq(q)
        first_norm_lane = (0 if fox else FOX_HEADS) + 2 * grp
        kmax = [jnp.sqrt(jnp.sum(jnp.where(lane == first_norm_lane + h, kn2_ref[0:1, :], 0.0),
                                 axis=1, keepdims=True)) * NORM_SLACK for h in range(2)]
        bound = [jnp.sqrt(qn[:, h:h + 1]) * kmax[h] for h in range(2)]
        tile(top, first, masked)
        m_col = [m_ref[h * tq:(h + 1) * tq, 0:1] for h in range(2)]
        if fox:
            head_lane = lax.broadcasted_iota(jnp.int32, cend_ref.shape, 1) - 2 * grp
            cend = [jnp.sum(jnp.where(head_lane == h, cend_ref[...], 0.0), axis=1, keepdims=True)
                    for h in range(2)]
            below_top = lax.broadcasted_iota(jnp.int32, (n_tiles, 1), 0) < top
            dead = below_top
            for h in range(2):
                gap = _reduce_rows(bound[h] - m_col[h], jnp.maximum)
                dead = dead & (-cend[h] < -UNDERFLOW_LOG2 - gap)
            return _reduce_rows(dead.astype(f32), jnp.add).astype(jnp.int32)[0, 0], top
        gap = _reduce_rows(jnp.maximum(bound[0] - m_col[0], bound[1] - m_col[1]), jnp.maximum)
        x = (-UNDERFLOW_LOG2 - gap) * inv_slope
        j_lo = jnp.clip(jnp.floor(x * (1.0 / tk)), 0.0, 1.0 * n_tiles)
        return jnp.minimum(j_lo.astype(jnp.int32)[0, 0], top), top

    sizes = FIRST_TILES_FOX if fox else FIRST_TILES_DIFF
    which = jnp.minimum(grp, len(sizes) - 1)
    branches = []
    for size in sizes:
        branches += [functools.partial(first_step, size, "diagonal_last"),
                     functools.partial(first_step, size, "by_position")]
    early = i < sum(jnp.where(which == n, size - 1, 0) for n, size in enumerate(sizes))
    j_lo, top = lax.switch(2 * which + early.astype(jnp.int32), branches)

    n_live = top - j_lo
    one = n_live & 1
    two = n_live & 2

    @pl.when(one == 1)
    def _():
        tile(top - 1, 1, masked=False)

    @pl.when(two == 2)
    def _():
        tile(top - one - 2, 2, masked=False)

    def body(step, carry):
        tile(top - one - two - TILES_PER_STEP * (step + 1), TILES_PER_STEP, masked=False)
        return carry

    lax.fori_loop(0, n_live // TILES_PER_STEP, body, 0)

    acc = [acc_ref[h * tq:(h + 1) * tq, :] for h in range(2)]
    if fox:
        out = jnp.where(lo_half, acc[0] / acc[0][:, HEAD_DIM:HEAD_DIM + 1],
                        acc[1] / acc[1][:, 0:1])
    else:
        lam = (jnp.exp(jnp.sum(lq1_ref[...] * lk1_ref[...], axis=1, keepdims=True))
               - jnp.exp(jnp.sum(lq2_ref[...] * lk2_ref[...], axis=1, keepdims=True))
               + LAMBDA_INIT)
        l = jnp.sum(l_ref[...], axis=1, keepdims=True)
        a = acc[0] / l[:tq] - lam * (acc[1] / l[tq:])
        out = _rms(a, gn_ref[...]) * (1.0 - LAMBDA_INIT)
    o_ref[q_rows, :] = out.astype(o_ref.dtype)


def _attention(q, kaug, v, kn2, extra, *, fox):
    s = q.shape[0]
    groups = FOX_HEADS // 2 if fox else DIFF_HEADS
    q_col0 = 0 if fox else FOX_WIDTH // LANES
    resident = lambda width: pl.BlockSpec((s, width), lambda g, i: (0, g),
                                          pipeline_mode=pl.Buffered(1))
    small = lambda a: pl.BlockSpec(a.shape, lambda g, i: (0, 0))
    in_specs = [
        pl.BlockSpec((Q_TILES_PER_STEP * TQ, LANES), lambda g, i: (i, q_col0 + g)),
        resident(2 * LANES),
        resident(2 * LANES if fox else LANES),
        small(kn2),
    ] + [small(a) for a in extra]
    scratch = [
        pltpu.VMEM((2 * TQ, LANES), jnp.float32),
        pltpu.VMEM((2 * TQ, LANES), jnp.float32),
    ]
    if not fox:
        scratch.append(pltpu.VMEM((2 * TQ, LANES), jnp.float32))
    return pl.pallas_call(
        functools.partial(_attn_kernel, fox=fox),
        grid=(groups, s // (Q_TILES_PER_STEP * TQ)),
        in_specs=in_specs,
        out_specs=pl.BlockSpec((Q_TILES_PER_STEP * TQ, LANES), lambda g, i: (i, g)),
        out_shape=jax.ShapeDtypeStruct((s, groups * LANES), jnp.bfloat16),
        scratch_shapes=scratch,
        compiler_params=pltpu.CompilerParams(
            dimension_semantics=("arbitrary", "arbitrary"), vmem_limit_bytes=VMEM_LIMIT),
        name="fox_attn" if fox else "diff_attn",
    )(q, kaug, v, kn2, *extra)


def _out_ffn_kernel(x_ref, fo_ref, do_ref, wo_ref, gf_ref, wg_ref, wu_ref, wd_ref,
                    gl_ref, o_ref):
    x1 = (x_ref[...]
          + jnp.dot(fo_ref[...], wo_ref[:FOX_WIDTH, :], preferred_element_type=jnp.float32)
          + jnp.dot(do_ref[...], wo_ref[FOX_WIDTH:, :], preferred_element_type=jnp.float32))
    h = _rms(x1, gf_ref[...]).astype(jnp.bfloat16)
    g = jnp.dot(h, wg_ref[...], preferred_element_type=jnp.float32)
    u = jnp.dot(h, wu_ref[...], preferred_element_type=jnp.float32)
    a = (g * jax.nn.sigmoid(g) * u).astype(jnp.bfloat16)
    x2 = x1 + jnp.dot(a, wd_ref[...], preferred_element_type=jnp.float32)
    o_ref[...] = _rms(x2, gl_ref[...])


def _out_ffn(x, fo, do, wo, gf, wg, wu, wd, gl):
    s = x.shape[0]
    const = lambda shape: pl.BlockSpec(shape, lambda i: (0, 0), pipeline_mode=pl.Buffered(1))
    return pl.pallas_call(
        _out_ffn_kernel,
        grid=(s // TM_FFN,),
        in_specs=[
            pl.BlockSpec((TM_FFN, D_MODEL), lambda i: (i, 0)),
            pl.BlockSpec((TM_FFN, FOX_WIDTH), lambda i: (i, 0)),
            pl.BlockSpec((TM_FFN, DIFF_WIDTH), lambda i: (i, 0)),
            const(wo.shape), const(gf.shape), const(wg.shape), const(wu.shape),
            const(wd.shape), const(gl.shape),
        ],
        out_specs=pl.BlockSpec((TM_FFN, D_MODEL), lambda i: (i, 0)),
        out_shape=jax.ShapeDtypeStruct((s, D_MODEL), jnp.float32),
        compiler_params=pltpu.CompilerParams(
            dimension_semantics=("arbitrary",), vmem_limit_bytes=VMEM_LIMIT),
        name="out_ffn",
    )(x, fo, do, wo, gf, wg, wu, wd, gl)


def kernel(x, mix_norm_g, w_in, b_forget, lambda_q1, lambda_k1, lambda_q2, lambda_k2,
           diff_norm_g, w_out, ffn_norm_g, w_gate, w_up, w_down, final_norm_g):
    b, s, d = x.shape
    assert b == 1 and d == D_MODEL and w_in.shape[0] == 1
    assert s % TQ == 0 and s % TM_IN == 0 and s % TM_FFN == 0
    assert TM_IN == TK and TQ == TK and TILES_PER_STEP == 4
    assert s >= max(FIRST_TILES_FOX + FIRST_TILES_DIFF) * TK and s % (Q_TILES_PER_STEP * TQ) == 0
    assert s < 2 ** 14 + 1
    bf16 = jnp.bfloat16
    x2d = x.reshape(s, d)

    w = w_in[0]
    col = np.cumsum((0, FOX_WIDTH, FOX_WIDTH, FOX_WIDTH, FOX_HEADS, DIFF_WIDTH, DIFF_WIDTH,
                     DIFF_WIDTH))
    fq, fk, fv, gate, dq, dk, dv = (w[:, col[n]:col[n + 1]] for n in range(7))
    order = jnp.argsort(b_forget[0])
    by_head = lambda t: t.reshape(d, FOX_HEADS, HEAD_DIM)[:, order, :].reshape(d, FOX_WIDTH)
    w_r = jnp.concatenate(
        [by_head(fk), dk, by_head(fq), dq, by_head(fv), dv, gate[:, order],
         jnp.zeros((d, GATE_PAD - FOX_HEADS), w.dtype)], axis=1).astype(bf16)
    b_pad = jnp.pad(b_forget[0][order].astype(jnp.float32),
                    (0, GATE_PAD - FOX_HEADS)).reshape(1, GATE_PAD)
    w_o = w_out[0]
    w_o = jnp.concatenate(
        [w_o[:FOX_WIDTH].reshape(FOX_HEADS, HEAD_DIM, d)[order].reshape(FOX_WIDTH, d),
         w_o[FOX_WIDTH:]], axis=0).astype(bf16)
    place = jnp.asarray(_fox_aug_placement(), bf16)
    hsum = jnp.asarray(np.arange(2 * FOX_WIDTH)[:, None] // HEAD_DIM == np.arange(LANES)[None, :],
                       bf16)

    q, fox_k, fox_v, diff_k, diff_v, cend, kn2 = _in_proj(
        x2d, mix_norm_g[0].reshape(1, d), w_r, b_pad, place, hsum)

    fox_out = _attention(q, fox_k, fox_v, kn2, (cend,), fox=True)
    lam_args = tuple(a[0].astype(jnp.float32).reshape(1, HEAD_DIM)
                     for a in (lambda_q1, lambda_k1, lambda_q2, lambda_k2))
    diff_out = _attention(
        q, diff_k, diff_v, kn2,
        lam_args + (diff_norm_g[0].astype(jnp.float32).reshape(1, DIFF_V_DIM),), fox=False)

    out = _out_ffn(
        x2d, fox_out, diff_out, w_o, ffn_norm_g[0].reshape(1, d),
        w_gate[0].astype(bf16), w_up[0].astype(bf16), w_down[0].astype(bf16),
        final_norm_g.reshape(1, d))
    return out.reshape(b, s, d)
```

```python
import functools
import math

import numpy as np
import jax
import jax.numpy as jnp
from jax import lax
from jax.experimental import pallas as pl
from jax.experimental.pallas import tpu as pltpu

D_MODEL = 1024
HEAD_DIM = 64
FOX_HEADS = 8
FOX_WIDTH = FOX_HEADS * HEAD_DIM
DIFF_HEADS = 4
DIFF_V_DIM = 2 * HEAD_DIM
DIFF_WIDTH = DIFF_HEADS * DIFF_V_DIM
D_FF = ((8 * D_MODEL + 767) // 768) * 256
EPS = 1e-6
NEG_INF = -1e30
LOG2E = math.log2(math.e)
Q_SCALE = LOG2E / math.sqrt(HEAD_DIM)
UNDERFLOW_LOG2 = 127.0
NORM_SLACK = 1.01
LAMBDA_INIT = 0.8 - 0.6 * math.exp(-0.3 * 0)

LANES = 128
GATE_PAD = LANES
QKV_WIDTH = 3 * FOX_WIDTH + 3 * DIFF_WIDTH
VMEM_LIMIT = 56 * 1024 * 1024

TM_IN = 512
TQ = 512
TK = 512
TILES_PER_STEP = 4
FIRST_TILES_FOX = (4,)
FIRST_TILES_DIFF = (2, 4)
Q_TILES_PER_STEP = 4
TM_FFN = 512

_NT = (((1,), (1,)), ((), ()))


def _rms(x, g):
    return x * lax.rsqrt(jnp.mean(x * x, axis=-1, keepdims=True) + EPS) * g


def _split3(x):
    hi = x.astype(jnp.bfloat16)
    r1 = x - hi.astype(jnp.float32)
    mid = r1.astype(jnp.bfloat16)
    lo = (r1 - mid.astype(jnp.float32)).astype(jnp.bfloat16)
    return hi, mid, lo


def _reduce_rows(x, op):
    while x.shape[0] > 8 and x.shape[0] % 2 == 0:
        half = x.shape[0] // 2
        x = op(x[:half], x[half:])
    reduce = jnp.max if op is jnp.maximum else jnp.sum
    return reduce(x, axis=0, keepdims=True)


def _fox_aug_placement():
    p = np.zeros((LANES, FOX_WIDTH), np.float32)
    for part in range(3):
        for head in range(FOX_HEADS):
            lane = 3 * (head % 2) + part
            p[part * FOX_HEADS + head, (head // 2) * LANES + lane] = 1.0
    return p


def _in_proj_kernel(x_ref, g_ref, wfk_ref, wdk_ref, wfq_ref, wdq_ref, wfv_ref, wdv_ref,
                    wgate_ref, b_ref, place_ref, hsum_ref,
                    q_ref, fk_ref, fv_ref, dk_ref, dv_ref, cend_ref, kn2_ref,
                    tri_ref, carry_ref):
    step = pl.program_id(0)
    tm = x_ref.shape[0]
    f32, bf16 = jnp.float32, jnp.bfloat16

    @pl.when(step == 0)
    def _():
        r = lax.broadcasted_iota(jnp.int32, (tm, tm), 0)
        c = lax.broadcasted_iota(jnp.int32, (tm, tm), 1)
        tri_ref[...] = jnp.where(r >= c, 1.0, 0.0).astype(bf16)
        carry_ref[...] = jnp.zeros_like(carry_ref)
        kn2_ref[...] = jnp.zeros_like(kn2_ref)

    lane = lax.broadcasted_iota(jnp.int32, (1, LANES), 1)
    lo_half = lane < HEAD_DIM
    head_lanes = lane < FOX_HEADS
    w5 = FOX_WIDTH

    def pack3(x):
        hi, mid, lo = (p.astype(f32) for p in _split3(jnp.where(head_lanes, x, 0.0)))
        return (hi + pltpu.roll(mid, FOX_HEADS, axis=1)
                + pltpu.roll(lo, 2 * FOX_HEADS, axis=1)).astype(bf16)

    h = _rms(x_ref[...], g_ref[...]).astype(bf16)

    z = jnp.dot(h, wgate_ref[...], preferred_element_type=f32) + b_ref[...]
    logf = LOG2E * (jnp.minimum(z, 0.0) - jnp.log1p(jnp.exp(-jnp.abs(z))))
    cs = jnp.dot(tri_ref[...], pack3(logf), preferred_element_type=f32)
    c = (cs + pltpu.roll(cs, LANES - FOX_HEADS, axis=1)
         + pltpu.roll(cs, LANES - 2 * FOX_HEADS, axis=1) + carry_ref[0:1, :])
    last = c[tm - 1:tm, :]
    carry_ref[...] = jnp.broadcast_to(last, carry_ref.shape)
    cend_ref[pl.ds(step, 1), :] = last
    caug = jnp.dot(pack3(-c), place_ref[...], preferred_element_type=f32).astype(bf16)

    project = lambda w_ref: jnp.dot(h, w_ref[...], preferred_element_type=f32)
    fk = project(wfk_ref).astype(bf16)
    dk = project(wdk_ref).astype(bf16)
    n2 = jnp.dot(jnp.concatenate([fk * fk, dk * dk], axis=1), hsum_ref[...],
                 preferred_element_type=f32)
    kn2_ref[...] = jnp.maximum(kn2_ref[...],
                               jnp.broadcast_to(jnp.max(n2, axis=0, keepdims=True), kn2_ref.shape))

    q_ref[:, :w5] = (project(wfq_ref) * Q_SCALE).astype(bf16)
    q_ref[:, w5:] = (project(wdq_ref) * Q_SCALE).astype(bf16)
    fv = project(wfv_ref).astype(bf16)
    dv_ref[...] = project(wdv_ref).astype(bf16)

    pos = step * tm + lax.broadcasted_iota(jnp.int32, (tm, LANES), 0)
    pos_lanes = jnp.where(lane < 3, pos & ~(LANES - 1),
                          jnp.where(lane < 6, pos & (LANES - 1), 0)).astype(f32).astype(bf16)
    for g in range(FOX_WIDTH // LANES):
        cols = slice(g * LANES, (g + 1) * LANES)
        key = slice(2 * g * LANES, (2 * g + 1) * LANES)
        bias = slice((2 * g + 1) * LANES, (2 * g + 2) * LANES)
        fk_ref[:, key] = fk[:, cols]
        fk_ref[:, bias] = caug[:, cols]
        dk_ref[:, key] = dk[:, cols]
        dk_ref[:, bias] = pos_lanes
        fv_ref[:, key] = jnp.where(lo_half, fv[:, cols], (lane == HEAD_DIM).astype(bf16))
        fv_ref[:, bias] = jnp.where(lo_half, (lane == 0).astype(bf16), fv[:, cols])


def _in_proj(x, g, weights, b, place, hsum):
    s = x.shape[0]
    n_tiles = s // TM_IN
    const = lambda a: pl.BlockSpec(a.shape, lambda i: (0, 0))
    row_block = lambda width: pl.BlockSpec((TM_IN, width), lambda i: (i, 0))
    bf16 = jnp.bfloat16
    return pl.pallas_call(
        _in_proj_kernel,
        grid=(n_tiles,),
        in_specs=([row_block(D_MODEL), const(g)] + [const(w) for w in weights]
                  + [const(b), const(place), const(hsum)]),
        out_specs=[
            row_block(2 * FOX_WIDTH),
            row_block(2 * FOX_WIDTH),
            row_block(2 * FOX_WIDTH),
            row_block(2 * DIFF_WIDTH),
            row_block(DIFF_WIDTH),
            pl.BlockSpec((n_tiles, LANES), lambda i: (0, 0)),
            pl.BlockSpec((8, LANES), lambda i: (0, 0)),
        ],
        out_shape=[
            jax.ShapeDtypeStruct((s, 2 * FOX_WIDTH), bf16),
            jax.ShapeDtypeStruct((s, 2 * FOX_WIDTH), bf16),
            jax.ShapeDtypeStruct((s, 2 * FOX_WIDTH), bf16),
            jax.ShapeDtypeStruct((s, 2 * DIFF_WIDTH), bf16),
            jax.ShapeDtypeStruct((s, DIFF_WIDTH), bf16),
            jax.ShapeDtypeStruct((n_tiles, LANES), jnp.float32),
            jax.ShapeDtypeStruct((8, LANES), jnp.float32),
        ],
        scratch_shapes=[
            pltpu.VMEM((TM_IN, TM_IN), bf16),
            pltpu.VMEM((8, LANES), jnp.float32),
        ],
        compiler_params=pltpu.CompilerParams(
            dimension_semantics=("arbitrary",), vmem_limit_bytes=VMEM_LIMIT),
        name="in_proj",
    )(x, g, *weights, b, place, hsum)


def _attn_kernel(*refs, fox):
    def one_tile(t, carry):
        rows = pl.ds(pl.multiple_of(t * TQ, TQ), TQ)
        _attn_query_tile(pl.program_id(1) * Q_TILES_PER_STEP + t, rows, refs, fox)
        return carry

    lax.fori_loop(0, Q_TILES_PER_STEP, one_tile, 0)


def _attn_query_tile(i, q_rows, refs, fox):
    if fox:
        q_ref, kaug_ref, vaug_ref, kn2_ref, cend_ref, o_ref, m_ref, acc_ref = refs
    else:
        (q_ref, kaug_ref, v_ref, kn2_ref, lq1_ref, lk1_ref, lq2_ref, lk2_ref, gn_ref,
         o_ref, m_ref, acc_ref, l_ref) = refs
    grp = pl.program_id(0)
    tq, tk = TQ, TK
    n_tiles = kaug_ref.shape[0] // tk
    f32, bf16 = jnp.float32, jnp.bfloat16

    lane = lax.broadcasted_iota(jnp.int32, (1, LANES), 1)
    lo_half = lane < HEAD_DIM

    if not fox:
        slope = LOG2E * jnp.exp2(-2.0 * jnp.full((1, 1), grp + 1, jnp.int32).astype(f32))
        sl = [p.astype(f32) for p in _split3(slope)]
        inv_slope = 1.0 / slope

    def half_norms_sq(x):
        l_in = lax.broadcasted_iota(jnp.int32, (LANES, LANES), 0)
        l_out = lax.broadcasted_iota(jnp.int32, (LANES, LANES), 1)
        half_sum = ((l_in // HEAD_DIM) == l_out).astype(bf16)
        return jnp.dot(x * x, half_sum, preferred_element_type=f32)

    q = q_ref[q_rows, :]
    zero = jnp.zeros_like(q)
    if fox:
        q_bias = [jnp.broadcast_to(((lane >= 3 * h) & (lane < 3 * h + 3)).astype(bf16), q.shape)
                  for h in range(2)]
    else:
        parts = jnp.where((lane == 0) | (lane == 3), sl[0],
                          jnp.where((lane == 1) | (lane == 4), sl[1],
                                    jnp.where((lane == 2) | (lane == 5), sl[2], 0.0)))
        q_bias = [jnp.broadcast_to(parts.astype(bf16), q.shape)] * 2
    q2 = jnp.concatenate(
        [jnp.concatenate([jnp.where(lo_half, q, zero), q_bias[0]], axis=1),
         jnp.concatenate([jnp.where(lo_half, zero, q), q_bias[1]], axis=1)], axis=0)

    m_ref[...] = jnp.full_like(m_ref, NEG_INF)
    acc_ref[...] = jnp.zeros_like(acc_ref)
    if not fox:
        l_ref[...] = jnp.zeros_like(l_ref)

    def tile(j, n_sub, masked):
        width = n_sub * tk
        rows = pl.ds(pl.multiple_of(j * tk, tk), width)
        s = lax.dot_general(q2, kaug_ref[rows, :], _NT, preferred_element_type=f32)
        if masked == "diagonal_last":
            r = lax.broadcasted_iota(jnp.int32, (2 * tq, tk), 0) & (tq - 1)
            c = lax.broadcasted_iota(jnp.int32, (2 * tq, tk), 1)
            diag = jnp.where(c <= r, s[:, width - tk:], NEG_INF)
            s = diag if n_sub == 1 else jnp.concatenate([s[:, :width - tk], diag], axis=1)
        elif masked == "by_position":
            r = lax.broadcasted_iota(jnp.int32, (2 * tq, width), 0) & (tq - 1)
            c = lax.broadcasted_iota(jnp.int32, (2 * tq, width), 1)
            s = jnp.where(c + (j - i) * tk <= r, s, NEG_INF)
        m_prev = m_ref[...]
        m_next = jnp.maximum(m_prev, jnp.max(s, axis=1, keepdims=True))
        alpha = jnp.exp2(m_prev - m_next)
        p = jnp.exp2(s - jnp.tile(m_next, (1, width // LANES)))
        pb = p.astype(bf16)
        if fox:
            pv = jnp.concatenate(
                [jnp.dot(pb[:tq], vaug_ref[rows, :LANES], preferred_element_type=f32),
                 jnp.dot(pb[tq:], vaug_ref[rows, LANES:], preferred_element_type=f32)], axis=0)
        else:
            pv = jnp.dot(pb, v_ref[rows, :], preferred_element_type=f32)
            part = p[:, :LANES]
            for blk in range(1, width // LANES):
                part = part + p[:, blk * LANES:(blk + 1) * LANES]
            l_ref[...] = alpha * l_ref[...] + part
        acc_ref[...] = acc_ref[...] * alpha + pv
        m_ref[...] = m_next

    def first_step(first, masked):
        top = jnp.maximum(i - (first - 1), 0)
        qn = half_norms_sq(q)
        first_norm_lane = (0 if fox else FOX_HEADS) + 2 * grp
        kmax = [jnp.sqrt(jnp.sum(jnp.where(lane == first_norm_lane + h, kn2_ref[0:1, :], 0.0),
                                 axis=1, keepdims=True)) * NORM_SLACK for h in range(2)]
        bound = [jnp.sqrt(qn[:, h:h + 1]) * kmax[h] for h in range(2)]
        tile(top, first, masked)
        m_col = [m_ref[h * tq:(h + 1) * tq, 0:1] for h in range(2)]
        if fox:
            head_lane = lax.broadcasted_iota(jnp.int32, cend_ref.shape, 1) - 2 * grp
            cend = [jnp.sum(jnp.where(head_lane == h, cend_ref[...], 0.0), axis=1, keepdims=True)
                    for h in range(2)]
            below_top = lax.broadcasted_iota(jnp.int32, (n_tiles, 1), 0) < top
            dead = below_top
            for h in range(2):
                gap = _reduce_rows(bound[h] - m_col[h], jnp.maximum)
                dead = dead & (-cend[h] < -UNDERFLOW_LOG2 - gap)
            return _reduce_rows(dead.astype(f32), jnp.add).astype(jnp.int32)[0, 0], top
        gap = _reduce_rows(jnp.maximum(bound[0] - m_col[0], bound[1] - m_col[1]), jnp.maximum)
        x = (-UNDERFLOW_LOG2 - gap) * inv_slope
        j_lo = jnp.clip(jnp.floor(x * (1.0 / tk)), 0.0, 1.0 * n_tiles)
        return jnp.minimum(j_lo.astype(jnp.int32)[0, 0], top), top

    sizes = FIRST_TILES_FOX if fox else FIRST_TILES_DIFF
    which = jnp.minimum(grp, len(sizes) - 1)
    branches = []
    for size in sizes:
        branches += [functools.partial(first_step, size, "diagonal_last"),
                     functools.partial(first_step, size, "by_position")]
    early = i < sum(jnp.where(which == n, size - 1, 0) for n, size in enumerate(sizes))
    j_lo, top = lax.switch(2 * which + early.astype(jnp.int32), branches)

    n_live = top - j_lo
    one = n_live & 1
    two = n_live & 2

    @pl.when(one == 1)
    def _():
        tile(top - 1, 1, masked=False)

    @pl.when(two == 2)
    def _():
        tile(top - one - 2, 2, masked=False)

    def body(step, carry):
        tile(top - one - two - TILES_PER_STEP * (step + 1), TILES_PER_STEP, masked=False)
        return carry

    lax.fori_loop(0, n_live // TILES_PER_STEP, body, 0)

    acc = [acc_ref[h * tq:(h + 1) * tq, :] for h in range(2)]
    if fox:
        out = jnp.where(lo_half, acc[0] / acc[0][:, HEAD_DIM:HEAD_DIM + 1],
                        acc[1] / acc[1][:, 0:1])
    else:
        lam = (jnp.exp(jnp.sum(lq1_ref[...] * lk1_ref[...], axis=1, keepdims=True))
               - jnp.exp(jnp.sum(lq2_ref[...] * lk2_ref[...], axis=1, keepdims=True))
               + LAMBDA_INIT)
        l = jnp.sum(l_ref[...], axis=1, keepdims=True)
        a = acc[0] / l[:tq] - lam * (acc[1] / l[tq:])
        out = _rms(a, gn_ref[...]) * (1.0 - LAMBDA_INIT)
    o_ref[q_rows, :] = out.astype(o_ref.dtype)


def _attention(q, kaug, v, kn2, extra, *, fox):
    s = q.shape[0]
    groups = FOX_HEADS // 2 if fox else DIFF_HEADS
    q_col0 = 0 if fox else FOX_WIDTH // LANES
    resident = lambda width: pl.BlockSpec((s, width), lambda g, i: (0, g),
                                          pipeline_mode=pl.Buffered(1))
    small = lambda a: pl.BlockSpec(a.shape, lambda g, i: (0, 0))
    in_specs = [
        pl.BlockSpec((Q_TILES_PER_STEP * TQ, LANES), lambda g, i: (i, q_col0 + g)),
        resident(2 * LANES),
        resident(2 * LANES if fox else LANES),
        small(kn2),
    ] + [small(a) for a in extra]
    scratch = [
        pltpu.VMEM((2 * TQ, LANES), jnp.float32),
        pltpu.VMEM((2 * TQ, LANES), jnp.float32),
    ]
    if not fox:
        scratch.append(pltpu.VMEM((2 * TQ, LANES), jnp.float32))
    return pl.pallas_call(
        functools.partial(_attn_kernel, fox=fox),
        grid=(groups, s // (Q_TILES_PER_STEP * TQ)),
        in_specs=in_specs,
        out_specs=pl.BlockSpec((Q_TILES_PER_STEP * TQ, LANES), lambda g, i: (i, g)),
        out_shape=jax.ShapeDtypeStruct((s, groups * LANES), jnp.bfloat16),
        scratch_shapes=scratch,
        compiler_params=pltpu.CompilerParams(
            dimension_semantics=("arbitrary", "arbitrary"), vmem_limit_bytes=VMEM_LIMIT),
        name="fox_attn" if fox else "diff_attn",
    )(q, kaug, v, kn2, *extra)


def _out_ffn_kernel(x_ref, fo_ref, do_ref, wo_ref, gf_ref, wg_ref, wu_ref, wd_ref,
                    gl_ref, o_ref):
    x1 = (x_ref[...]
          + jnp.dot(fo_ref[...], wo_ref[:FOX_WIDTH, :], preferred_element_type=jnp.float32)
          + jnp.dot(do_ref[...], wo_ref[FOX_WIDTH:, :], preferred_element_type=jnp.float32))
    h = _rms(x1, gf_ref[...]).astype(jnp.bfloat16)
    g = jnp.dot(h, wg_ref[...], preferred_element_type=jnp.float32)
    u = jnp.dot(h, wu_ref[...], preferred_element_type=jnp.float32)
    a = (g * jax.nn.sigmoid(g) * u).astype(jnp.bfloat16)
    x2 = x1 + jnp.dot(a, wd_ref[...], preferred_element_type=jnp.float32)
    o_ref[...] = _rms(x2, gl_ref[...])


def _out_ffn(x, fo, do, wo, gf, wg, wu, wd, gl):
    s = x.shape[0]
    const = lambda shape: pl.BlockSpec(shape, lambda i: (0, 0), pipeline_mode=pl.Buffered(1))
    return pl.pallas_call(
        _out_ffn_kernel,
        grid=(s // TM_FFN,),
        in_specs=[
            pl.BlockSpec((TM_FFN, D_MODEL), lambda i: (i, 0)),
            pl.BlockSpec((TM_FFN, FOX_WIDTH), lambda i: (i, 0)),
            pl.BlockSpec((TM_FFN, DIFF_WIDTH), lambda i: (i, 0)),
            const(wo.shape), const(gf.shape), const(wg.shape), const(wu.shape),
            const(wd.shape), const(gl.shape),
        ],
        out_specs=pl.BlockSpec((TM_FFN, D_MODEL), lambda i: (i, 0)),
        out_shape=jax.ShapeDtypeStruct((s, D_MODEL), jnp.float32),
        compiler_params=pltpu.CompilerParams(
            dimension_semantics=("arbitrary",), vmem_limit_bytes=VMEM_LIMIT),
        name="out_ffn",
    )(x, fo, do, wo, gf, wg, wu, wd, gl)


def kernel(x, mix_norm_g, w_in, b_forget, lambda_q1, lambda_k1, lambda_q2, lambda_k2,
           diff_norm_g, w_out, ffn_norm_g, w_gate, w_up, w_down, final_norm_g):
    b, s, d = x.shape
    assert b == 1 and d == D_MODEL and w_in.shape[0] == 1
    assert s % TQ == 0 and s % TM_IN == 0 and s % TM_FFN == 0
    assert TM_IN == TK and TQ == TK and TILES_PER_STEP == 4
    assert s >= max(FIRST_TILES_FOX + FIRST_TILES_DIFF) * TK and s % (Q_TILES_PER_STEP * TQ) == 0
    assert s < 2 ** 14 + 1
    bf16 = jnp.bfloat16
    x2d = x.reshape(s, d)

    w = w_in[0]
    col = np.cumsum((0, FOX_WIDTH, FOX_WIDTH, FOX_WIDTH, FOX_HEADS, DIFF_WIDTH, DIFF_WIDTH,
                     DIFF_WIDTH))
    fq, fk, fv, gate, dq, dk, dv = (w[:, col[n]:col[n + 1]] for n in range(7))
    order = jnp.argsort(b_forget[0])
    by_head = lambda t: t.reshape(d, FOX_HEADS, HEAD_DIM)[:, order, :].reshape(d, FOX_WIDTH)
    weights = [t.astype(bf16) for t in (
        by_head(fk), dk, by_head(fq), dq, by_head(fv), dv,
        jnp.pad(gate[:, order], ((0, 0), (0, GATE_PAD - FOX_HEADS))))]
    b_pad = jnp.pad(b_forget[0][order].astype(jnp.float32),
                    (0, GATE_PAD - FOX_HEADS)).reshape(1, GATE_PAD)
    w_o = w_out[0]
    w_o = jnp.concatenate(
        [w_o[:FOX_WIDTH].reshape(FOX_HEADS, HEAD_DIM, d)[order].reshape(FOX_WIDTH, d),
         w_o[FOX_WIDTH:]], axis=0).astype(bf16)
    place = jnp.asarray(_fox_aug_placement(), bf16)
    hsum = jnp.asarray(np.arange(2 * FOX_WIDTH)[:, None] // HEAD_DIM == np.arange(LANES)[None, :],
                       bf16)

    q, fox_k, fox_v, diff_k, diff_v, cend, kn2 = _in_proj(
        x2d, mix_norm_g[0].reshape(1, d), weights, b_pad, place, hsum)

    fox_out = _attention(q, fox_k, fox_v, kn2, (cend,), fox=True)
    lam_args = tuple(a[0].astype(jnp.float32).reshape(1, HEAD_DIM)
                     for a in (lambda_q1, lambda_k1, lambda_q2, lambda_k2))
    diff_out = _attention(
        q, diff_k, diff_v, kn2,
        lam_args + (diff_norm_g[0].astype(jnp.float32).reshape(1, DIFF_V_DIM),), fox=False)

    out = _out_ffn(
        x2d, fox_out, diff_out, w_o, ffn_norm_g[0].reshape(1, d),
        w_gate[0].astype(bf16), w_up[0].astype(bf16), w_down[0].astype(bf16),
        final_norm_g.reshape(1, d))
    return out.reshape(b, s, d)
```

```python
import functools
import math

import numpy as np
import jax
import jax.numpy as jnp
from jax import lax
from jax.experimental import pallas as pl
from jax.experimental.pallas import tpu as pltpu

D_MODEL = 1024
HEAD_DIM = 64
FOX_HEADS = 8
FOX_WIDTH = FOX_HEADS * HEAD_DIM
DIFF_HEADS = 4
DIFF_V_DIM = 2 * HEAD_DIM
DIFF_WIDTH = DIFF_HEADS * DIFF_V_DIM
EPS = 1e-6
NEG_INF = -1e30
LOG2E = math.log2(math.e)
Q_SCALE = LOG2E / math.sqrt(HEAD_DIM)
UNDERFLOW_LOG2 = 127.0
NORM_SLACK = 1.01
LAMBDA_INIT = 0.8 - 0.6 * math.exp(-0.3 * 0)

LANES = 128
SUBLANES = 8
GATE_PAD = LANES
VMEM_LIMIT = 56 * 1024 * 1024

TM_IN = 512
TQ = 512
TK = 512
TILES_PER_STEP = 4
FIRST_TILES_FOX = (4,)
FIRST_TILES_DIFF = (2, 4)
Q_TILES_PER_STEP = 4
TM_FFN = 512

_NT = (((1,), (1,)), ((), ()))


def _rms(x, g):
    return x * lax.rsqrt(jnp.mean(x * x, axis=-1, keepdims=True) + EPS) * g


def _split3(x):
    hi = x.astype(jnp.bfloat16)
    r1 = x - hi.astype(jnp.float32)
    mid = r1.astype(jnp.bfloat16)
    lo = (r1 - mid.astype(jnp.float32)).astype(jnp.bfloat16)
    return hi, mid, lo


def _reduce_rows(x, op):
    while x.shape[0] > 8 and x.shape[0] % 2 == 0:
        half = x.shape[0] // 2
        x = op(x[:half], x[half:])
    reduce = jnp.max if op is jnp.maximum else jnp.sum
    return reduce(x, axis=0, keepdims=True)


def _fox_aug_placement():
    p = np.zeros((LANES, FOX_WIDTH), np.float32)
    for part in range(3):
        for head in range(FOX_HEADS):
            lane = 3 * (head % 2) + part
            p[part * FOX_HEADS + head, (head // 2) * LANES + lane] = 1.0
    return p


def _in_proj_kernel(x_ref, g_ref, wfk_ref, wdk_ref, wfq_ref, wdq_ref, wfv_ref, wdv_ref,
                    wgate_ref, b_ref, place_ref, hsum_ref,
                    q_ref, fk_ref, fv_ref, dk_ref, dv_ref, cend_ref, kn2_ref,
                    tri_ref, carry_ref):
    step = pl.program_id(0)
    tm = x_ref.shape[0]
    f32, bf16 = jnp.float32, jnp.bfloat16

    @pl.when(step == 0)
    def _():
        r = lax.broadcasted_iota(jnp.int32, (tm, tm), 0)
        c = lax.broadcasted_iota(jnp.int32, (tm, tm), 1)
        tri_ref[...] = jnp.where(r >= c, 1.0, 0.0).astype(bf16)
        carry_ref[...] = jnp.zeros_like(carry_ref)
        kn2_ref[...] = jnp.zeros_like(kn2_ref)

    lane = lax.broadcasted_iota(jnp.int32, (1, LANES), 1)
    lo_half = lane < HEAD_DIM
    head_lanes = lane < FOX_HEADS
    w5 = FOX_WIDTH

    def pack3(x):
        hi, mid, lo = (p.astype(f32) for p in _split3(jnp.where(head_lanes, x, 0.0)))
        return (hi + pltpu.roll(mid, FOX_HEADS, axis=1)
                + pltpu.roll(lo, 2 * FOX_HEADS, axis=1)).astype(bf16)

    h = _rms(x_ref[...], g_ref[...]).astype(bf16)

    z = jnp.dot(h, wgate_ref[...], preferred_element_type=f32) + b_ref[...]
    logf = LOG2E * (jnp.minimum(z, 0.0) - jnp.log1p(jnp.exp(-jnp.abs(z))))
    cs = jnp.dot(tri_ref[...], pack3(logf), preferred_element_type=f32)
    c = (cs + pltpu.roll(cs, LANES - FOX_HEADS, axis=1)
         + pltpu.roll(cs, LANES - 2 * FOX_HEADS, axis=1) + carry_ref[0:1, :])
    last = c[tm - 1:tm, :]
    carry_ref[...] = jnp.broadcast_to(last, carry_ref.shape)
    cend_ref[pl.ds(step, 1), :] = last
    caug = jnp.dot(pack3(-c), place_ref[...], preferred_element_type=f32).astype(bf16)

    project = lambda w_ref: jnp.dot(h, w_ref[...], preferred_element_type=f32)
    fk = project(wfk_ref).astype(bf16)
    dk = project(wdk_ref).astype(bf16)
    n2 = jnp.dot(jnp.concatenate([fk * fk, dk * dk], axis=1), hsum_ref[...],
                 preferred_element_type=f32)
    kn2_ref[...] = jnp.maximum(kn2_ref[...],
                               jnp.broadcast_to(jnp.max(n2, axis=0, keepdims=True), kn2_ref.shape))

    q_ref[:, :w5] = (project(wfq_ref) * Q_SCALE).astype(bf16)
    q_ref[:, w5:] = (project(wdq_ref) * Q_SCALE).astype(bf16)
    fv = project(wfv_ref).astype(bf16)
    dv_ref[...] = project(wdv_ref).astype(bf16)

    pos = step * tm + lax.broadcasted_iota(jnp.int32, (tm, LANES), 0)
    pos_lanes = jnp.where(lane < 3, pos & ~(LANES - 1),
                          jnp.where(lane < 6, pos & (LANES - 1), 0)).astype(f32).astype(bf16)
    for g in range(FOX_WIDTH // LANES):
        cols = slice(g * LANES, (g + 1) * LANES)
        key = slice(2 * g * LANES, (2 * g + 1) * LANES)
        bias = slice((2 * g + 1) * LANES, (2 * g + 2) * LANES)
        fk_ref[:, key] = fk[:, cols]
        fk_ref[:, bias] = caug[:, cols]
        dk_ref[:, key] = dk[:, cols]
        dk_ref[:, bias] = pos_lanes
        fv_ref[:, key] = jnp.where(lo_half, fv[:, cols], (lane == HEAD_DIM).astype(bf16))
        fv_ref[:, bias] = jnp.where(lo_half, (lane == 0).astype(bf16), fv[:, cols])


def _in_proj(x, g, weights, b, place, hsum):
    s = x.shape[0]
    n_tiles = s // TM_IN
    const = lambda a: pl.BlockSpec(a.shape, lambda i: (0, 0))
    row_block = lambda width: pl.BlockSpec((TM_IN, width), lambda i: (i, 0))
    bf16 = jnp.bfloat16
    return pl.pallas_call(
        _in_proj_kernel,
        grid=(n_tiles,),
        in_specs=([row_block(D_MODEL), const(g)] + [const(w) for w in weights]
                  + [const(b), const(place), const(hsum)]),
        out_specs=[
            row_block(2 * FOX_WIDTH),
            row_block(2 * FOX_WIDTH),
            row_block(2 * FOX_WIDTH),
            row_block(2 * DIFF_WIDTH),
            row_block(DIFF_WIDTH),
            pl.BlockSpec((n_tiles, LANES), lambda i: (0, 0)),
            pl.BlockSpec((SUBLANES, LANES), lambda i: (0, 0)),
        ],
        out_shape=[
            jax.ShapeDtypeStruct((s, 2 * FOX_WIDTH), bf16),
            jax.ShapeDtypeStruct((s, 2 * FOX_WIDTH), bf16),
            jax.ShapeDtypeStruct((s, 2 * FOX_WIDTH), bf16),
            jax.ShapeDtypeStruct((s, 2 * DIFF_WIDTH), bf16),
            jax.ShapeDtypeStruct((s, DIFF_WIDTH), bf16),
            jax.ShapeDtypeStruct((n_tiles, LANES), jnp.float32),
            jax.ShapeDtypeStruct((SUBLANES, LANES), jnp.float32),
        ],
        scratch_shapes=[
            pltpu.VMEM((TM_IN, TM_IN), bf16),
            pltpu.VMEM((SUBLANES, LANES), jnp.float32),
        ],
        compiler_params=pltpu.CompilerParams(
            dimension_semantics=("arbitrary",), vmem_limit_bytes=VMEM_LIMIT),
        name="in_proj",
    )(x, g, *weights, b, place, hsum)


def _attn_kernel(*refs, fox):
    def one_tile(t, carry):
        rows = pl.ds(pl.multiple_of(t * TQ, TQ), TQ)
        _attn_query_tile(pl.program_id(1) * Q_TILES_PER_STEP + t, rows, refs, fox)
        return carry

    lax.fori_loop(0, Q_TILES_PER_STEP, one_tile, 0)


def _attn_query_tile(i, q_rows, refs, fox):
    if fox:
        q_ref, kaug_ref, vaug_ref, kn2_ref, cend_ref, o_ref, m_ref, acc_ref = refs
    else:
        (q_ref, kaug_ref, v_ref, kn2_ref, lq1_ref, lk1_ref, lq2_ref, lk2_ref, gn_ref,
         o_ref, m_ref, acc_ref, l_ref) = refs
    grp = pl.program_id(0)
    tq, tk = TQ, TK
    n_tiles = kaug_ref.shape[0] // tk
    f32, bf16 = jnp.float32, jnp.bfloat16

    lane = lax.broadcasted_iota(jnp.int32, (1, LANES), 1)
    lo_half = lane < HEAD_DIM

    if not fox:
        slope = LOG2E * jnp.exp2(-2.0 * jnp.full((1, 1), grp + 1, jnp.int32).astype(f32))
        sl = [p.astype(f32) for p in _split3(slope)]
        inv_slope = 1.0 / slope

    def half_norms_sq(x):
        l_in = lax.broadcasted_iota(jnp.int32, (LANES, LANES), 0)
        l_out = lax.broadcasted_iota(jnp.int32, (LANES, LANES), 1)
        half_sum = ((l_in // HEAD_DIM) == l_out).astype(bf16)
        return jnp.dot(x * x, half_sum, preferred_element_type=f32)

    q = q_ref[q_rows, :]
    zero = jnp.zeros_like(q)
    if fox:
        q_bias = [jnp.broadcast_to(((lane >= 3 * h) & (lane < 3 * h + 3)).astype(bf16), q.shape)
                  for h in range(2)]
    else:
        parts = jnp.where((lane == 0) | (lane == 3), sl[0],
                          jnp.where((lane == 1) | (lane == 4), sl[1],
                                    jnp.where((lane == 2) | (lane == 5), sl[2], 0.0)))
        q_bias = [jnp.broadcast_to(parts.astype(bf16), q.shape)] * 2
    q2 = jnp.concatenate(
        [jnp.concatenate([jnp.where(lo_half, q, zero), q_bias[0]], axis=1),
         jnp.concatenate([jnp.where(lo_half, zero, q), q_bias[1]], axis=1)], axis=0)

    m_ref[...] = jnp.full_like(m_ref, NEG_INF)
    acc_ref[...] = jnp.zeros_like(acc_ref)
    if not fox:
        l_ref[...] = jnp.zeros_like(l_ref)

    def tile(j, n_sub, masked):
        width = n_sub * tk
        rows = pl.ds(pl.multiple_of(j * tk, tk), width)
        s = lax.dot_general(q2, kaug_ref[rows, :], _NT, preferred_element_type=f32)
        if masked == "diagonal_last":
            r = lax.broadcasted_iota(jnp.int32, (2 * tq, tk), 0) & (tq - 1)
            c = lax.broadcasted_iota(jnp.int32, (2 * tq, tk), 1)
            diag = jnp.where(c <= r, s[:, width - tk:], NEG_INF)
            s = diag if n_sub == 1 else jnp.concatenate([s[:, :width - tk], diag], axis=1)
        elif masked == "by_position":
            r = lax.broadcasted_iota(jnp.int32, (2 * tq, width), 0) & (tq - 1)
            c = lax.broadcasted_iota(jnp.int32, (2 * tq, width), 1)
            s = jnp.where(c + (j - i) * tk <= r, s, NEG_INF)
        m_prev = m_ref[...]
        m_next = jnp.maximum(m_prev, jnp.max(s, axis=1, keepdims=True))
        alpha = jnp.exp2(m_prev - m_next)
        p = jnp.exp2(s - jnp.tile(m_next, (1, width // LANES)))
        pb = p.astype(bf16)
        if fox:
            pv = jnp.concatenate(
                [jnp.dot(pb[:tq], vaug_ref[rows, :LANES], preferred_element_type=f32),
                 jnp.dot(pb[tq:], vaug_ref[rows, LANES:], preferred_element_type=f32)], axis=0)
        else:
            pv = jnp.dot(pb, v_ref[rows, :], preferred_element_type=f32)
            part = p[:, :LANES]
            for blk in range(1, width // LANES):
                part = part + p[:, blk * LANES:(blk + 1) * LANES]
            l_ref[...] = alpha * l_ref[...] + part
        acc_ref[...] = acc_ref[...] * alpha + pv
        m_ref[...] = m_next

    def first_step(first, masked):
        top = jnp.maximum(i - (first - 1), 0)
        qn = half_norms_sq(q)
        first_norm_lane = (0 if fox else FOX_HEADS) + 2 * grp
        kmax = [jnp.sqrt(jnp.sum(jnp.where(lane == first_norm_lane + h, kn2_ref[0:1, :], 0.0),
                                 axis=1, keepdims=True)) * NORM_SLACK for h in range(2)]
        bound = [jnp.sqrt(qn[:, h:h + 1]) * kmax[h] for h in range(2)]
        tile(top, first, masked)
        m_col = [m_ref[h * tq:(h + 1) * tq, 0:1] for h in range(2)]
        if fox:
            head_lane = lax.broadcasted_iota(jnp.int32, cend_ref.shape, 1) - 2 * grp
            cend = [jnp.sum(jnp.where(head_lane == h, cend_ref[...], 0.0), axis=1, keepdims=True)
                    for h in range(2)]
            below_top = lax.broadcasted_iota(jnp.int32, (n_tiles, 1), 0) < top
            dead = below_top
            for h in range(2):
                gap = _reduce_rows(bound[h] - m_col[h], jnp.maximum)
                dead = dead & (-cend[h] < -UNDERFLOW_LOG2 - gap)
            return _reduce_rows(dead.astype(f32), jnp.add).astype(jnp.int32)[0, 0], top
        gap = _reduce_rows(jnp.maximum(bound[0] - m_col[0], bound[1] - m_col[1]), jnp.maximum)
        x = (-UNDERFLOW_LOG2 - gap) * inv_slope
        j_lo = jnp.clip(jnp.floor(x * (1.0 / tk)), 0.0, 1.0 * n_tiles)
        return jnp.minimum(j_lo.astype(jnp.int32)[0, 0], top), top

    sizes = FIRST_TILES_FOX if fox else FIRST_TILES_DIFF
    which = jnp.minimum(grp, len(sizes) - 1)
    branches = []
    for size in sizes:
        branches += [functools.partial(first_step, size, "diagonal_last"),
                     functools.partial(first_step, size, "by_position")]
    early = i < sum(jnp.where(which == n, size - 1, 0) for n, size in enumerate(sizes))
    j_lo, top = lax.switch(2 * which + early.astype(jnp.int32), branches)

    n_live = top - j_lo
    one = n_live & 1
    two = n_live & 2

    @pl.when(one == 1)
    def _():
        tile(top - 1, 1, masked=False)

    @pl.when(two == 2)
    def _():
        tile(top - one - 2, 2, masked=False)

    def body(step, carry):
        tile(top - one - two - TILES_PER_STEP * (step + 1), TILES_PER_STEP, masked=False)
        return carry

    lax.fori_loop(0, n_live // TILES_PER_STEP, body, 0)

    acc = [acc_ref[h * tq:(h + 1) * tq, :] for h in range(2)]
    if fox:
        out = jnp.where(lo_half, acc[0] / acc[0][:, HEAD_DIM:HEAD_DIM + 1],
                        acc[1] / acc[1][:, 0:1])
    else:
        lam = (jnp.exp(jnp.sum(lq1_ref[...] * lk1_ref[...], axis=1, keepdims=True))
               - jnp.exp(jnp.sum(lq2_ref[...] * lk2_ref[...], axis=1, keepdims=True))
               + LAMBDA_INIT)
        l = jnp.sum(l_ref[...], axis=1, keepdims=True)
        a = acc[0] / l[:tq] - lam * (acc[1] / l[tq:])
        out = _rms(a, gn_ref[...]) * (1.0 - LAMBDA_INIT)
    o_ref[q_rows, :] = out.astype(o_ref.dtype)


def _attention(q, kaug, v, kn2, extra, *, fox):
    s = q.shape[0]
    groups = FOX_HEADS // 2 if fox else DIFF_HEADS
    q_col0 = 0 if fox else FOX_WIDTH // LANES
    resident = lambda width: pl.BlockSpec((s, width), lambda g, i: (0, g),
                                          pipeline_mode=pl.Buffered(1))
    small = lambda a: pl.BlockSpec(a.shape, lambda g, i: (0, 0))
    in_specs = [
        pl.BlockSpec((Q_TILES_PER_STEP * TQ, LANES), lambda g, i: (i, q_col0 + g)),
        resident(2 * LANES),
        resident(2 * LANES if fox else LANES),
        small(kn2),
    ] + [small(a) for a in extra]
    scratch = [
        pltpu.VMEM((2 * TQ, LANES), jnp.float32),
        pltpu.VMEM((2 * TQ, LANES), jnp.float32),
    ]
    if not fox:
        scratch.append(pltpu.VMEM((2 * TQ, LANES), jnp.float32))
    return pl.pallas_call(
        functools.partial(_attn_kernel, fox=fox),
        grid=(groups, s // (Q_TILES_PER_STEP * TQ)),
        in_specs=in_specs,
        out_specs=pl.BlockSpec((Q_TILES_PER_STEP * TQ, LANES), lambda g, i: (i, g)),
        out_shape=jax.ShapeDtypeStruct((s, groups * LANES), jnp.bfloat16),
        scratch_shapes=scratch,
        compiler_params=pltpu.CompilerParams(
            dimension_semantics=("arbitrary", "arbitrary"), vmem_limit_bytes=VMEM_LIMIT),
        name="fox_attn" if fox else "diff_attn",
    )(q, kaug, v, kn2, *extra)


def _out_ffn_kernel(x_ref, fo_ref, do_ref, wo_ref, gf_ref, wg_ref, wu_ref, wd_ref,
                    gl_ref, o_ref):
    x1 = (x_ref[...]
          + jnp.dot(fo_ref[...], wo_ref[:FOX_WIDTH, :], preferred_element_type=jnp.float32)
          + jnp.dot(do_ref[...], wo_ref[FOX_WIDTH:, :], preferred_element_type=jnp.float32))
    h = _rms(x1, gf_ref[...]).astype(jnp.bfloat16)
    g = jnp.dot(h, wg_ref[...], preferred_element_type=jnp.float32)
    u = jnp.dot(h, wu_ref[...], preferred_element_type=jnp.float32)
    a = (g * jax.nn.sigmoid(g) * u).astype(jnp.bfloat16)
    x2 = x1 + jnp.dot(a, wd_ref[...], preferred_element_type=jnp.float32)
    o_ref[...] = _rms(x2, gl_ref[...])


def _out_ffn(x, fo, do, wo, gf, wg, wu, wd, gl):
    s = x.shape[0]
    const = lambda shape: pl.BlockSpec(shape, lambda i: (0, 0), pipeline_mode=pl.Buffered(1))
    return pl.pallas_call(
        _out_ffn_kernel,
        grid=(s // TM_FFN,),
        in_specs=[
            pl.BlockSpec((TM_FFN, D_MODEL), lambda i: (i, 0)),
            pl.BlockSpec((TM_FFN, FOX_WIDTH), lambda i: (i, 0)),
            pl.BlockSpec((TM_FFN, DIFF_WIDTH), lambda i: (i, 0)),
            const(wo.shape), const(gf.shape), const(wg.shape), const(wu.shape),
            const(wd.shape), const(gl.shape),
        ],
        out_specs=pl.BlockSpec((TM_FFN, D_MODEL), lambda i: (i, 0)),
        out_shape=jax.ShapeDtypeStruct((s, D_MODEL), jnp.float32),
        compiler_params=pltpu.CompilerParams(
            dimension_semantics=("arbitrary",), vmem_limit_bytes=VMEM_LIMIT),
        name="out_ffn",
    )(x, fo, do, wo, gf, wg, wu, wd, gl)


def kernel(x, mix_norm_g, w_in, b_forget, lambda_q1, lambda_k1, lambda_q2, lambda_k2,
           diff_norm_g, w_out, ffn_norm_g, w_gate, w_up, w_down, final_norm_g):
    b, s, d = x.shape
    assert b == 1 and d == D_MODEL and w_in.shape[0] == 1
    assert s % TQ == 0 and s % TM_IN == 0 and s % TM_FFN == 0
    assert TM_IN == TK and TQ == TK and TILES_PER_STEP == 4
    assert s >= max(FIRST_TILES_FOX + FIRST_TILES_DIFF) * TK and s % (Q_TILES_PER_STEP * TQ) == 0
    assert s < 2 ** 14 + 1
    bf16 = jnp.bfloat16
    x2d = x.reshape(s, d)

    w = w_in[0]
    col = np.cumsum((0, FOX_WIDTH, FOX_WIDTH, FOX_WIDTH, FOX_HEADS, DIFF_WIDTH, DIFF_WIDTH,
                     DIFF_WIDTH))
    fq, fk, fv, gate, dq, dk, dv = (w[:, col[n]:col[n + 1]] for n in range(7))
    order = jnp.argsort(b_forget[0])
    by_head = lambda t: t.reshape(d, FOX_HEADS, HEAD_DIM)[:, order, :].reshape(d, FOX_WIDTH)
    weights = [t.astype(bf16) for t in (
        by_head(fk), dk, by_head(fq), dq, by_head(fv), dv,
        jnp.pad(gate[:, order], ((0, 0), (0, GATE_PAD - FOX_HEADS))))]
    b_pad = jnp.pad(b_forget[0][order].astype(jnp.float32),
                    (0, GATE_PAD - FOX_HEADS)).reshape(1, GATE_PAD)
    w_o = w_out[0]
    w_o = jnp.concatenate(
        [w_o[:FOX_WIDTH].reshape(FOX_HEADS, HEAD_DIM, d)[order].reshape(FOX_WIDTH, d),
         w_o[FOX_WIDTH:]], axis=0).astype(bf16)
    place = jnp.asarray(_fox_aug_placement(), bf16)
    hsum = jnp.asarray(np.arange(2 * FOX_WIDTH)[:, None] // HEAD_DIM == np.arange(LANES)[None, :],
                       bf16)

    q, fox_k, fox_v, diff_k, diff_v, cend, kn2 = _in_proj(
        x2d, mix_norm_g[0].reshape(1, d), weights, b_pad, place, hsum)

    fox_out = _attention(q, fox_k, fox_v, kn2, (cend,), fox=True)
    lam_args = tuple(a[0].astype(jnp.float32).reshape(1, HEAD_DIM)
                     for a in (lambda_q1, lambda_k1, lambda_q2, lambda_k2))
    diff_out = _attention(
        q, diff_k, diff_v, kn2,
        lam_args + (diff_norm_g[0].astype(jnp.float32).reshape(1, DIFF_V_DIM),), fox=False)

    out = _out_ffn(
        x2d, fox_out, diff_out, w_o, ffn_norm_g[0].reshape(1, d),
        w_gate[0].astype(bf16), w_up[0].astype(bf16), w_down[0].astype(bf16),
        final_norm_g.reshape(1, d))
    return out.reshape(b, s, d)
```

```python
import functools
import math

import numpy as np
import jax
import jax.numpy as jnp
from jax import lax
from jax.experimental import pallas as pl
from jax.experimental.pallas import tpu as pltpu

D_MODEL = 1024
HEAD_DIM = 64
FOX_HEADS = 8
FOX_WIDTH = FOX_HEADS * HEAD_DIM
DIFF_HEADS = 4
DIFF_V_DIM = 2 * HEAD_DIM
DIFF_WIDTH = DIFF_HEADS * DIFF_V_DIM
EPS = 1e-6
NEG_INF = -1e30
LOG2E = math.log2(math.e)
Q_SCALE = LOG2E / math.sqrt(HEAD_DIM)
UNDERFLOW_LOG2 = 127.0
NORM_SLACK = 1.01
LAMBDA_INIT = 0.8 - 0.6 * math.exp(-0.3 * 0)

LANES = 128
SUBLANES = 8
GATE_PAD = LANES
VMEM_LIMIT = 56 * 1024 * 1024

TM_IN = 512
TQ = 512
TK = 512
TILES_PER_STEP = 4
FIRST_TILES_FOX = (3, 3, 4, 4)
FIRST_TILES_DIFF = (2, 4, 4, 4)
Q_TILES_PER_STEP = 4
TM_FFN = 512

_NT = (((1,), (1,)), ((), ()))


def _rms(x, g):
    return x * lax.rsqrt(jnp.mean(x * x, axis=-1, keepdims=True) + EPS) * g


def _split3(x):
    hi = x.astype(jnp.bfloat16)
    r1 = x - hi.astype(jnp.float32)
    mid = r1.astype(jnp.bfloat16)
    lo = (r1 - mid.astype(jnp.float32)).astype(jnp.bfloat16)
    return hi, mid, lo


def _reduce_rows(x, op):
    while x.shape[0] > 8 and x.shape[0] % 2 == 0:
        half = x.shape[0] // 2
        x = op(x[:half], x[half:])
    reduce = jnp.max if op is jnp.maximum else jnp.sum
    return reduce(x, axis=0, keepdims=True)


def _fox_aug_placement():
    p = np.zeros((LANES, FOX_WIDTH), np.float32)
    for part in range(3):
        for head in range(FOX_HEADS):
            lane = 3 * (head % 2) + part
            p[part * FOX_HEADS + head, (head // 2) * LANES + lane] = 1.0
    return p


def _in_proj_kernel(x_ref, g_ref, wfk_ref, wdk_ref, wfq_ref, wdq_ref, wfv_ref, wdv_ref,
                    wgate_ref, b_ref, place_ref, hsum_ref,
                    q_ref, fk_ref, fv_ref, dk_ref, dv_ref, cend_ref, kn2_ref,
                    tri_ref, carry_ref):
    step = pl.program_id(0)
    tm = x_ref.shape[0]
    f32, bf16 = jnp.float32, jnp.bfloat16

    @pl.when(step == 0)
    def _():
        r = lax.broadcasted_iota(jnp.int32, (tm, tm), 0)
        c = lax.broadcasted_iota(jnp.int32, (tm, tm), 1)
        tri_ref[...] = jnp.where(r >= c, 1.0, 0.0).astype(bf16)
        carry_ref[...] = jnp.zeros_like(carry_ref)
        kn2_ref[...] = jnp.zeros_like(kn2_ref)

    lane = lax.broadcasted_iota(jnp.int32, (1, LANES), 1)
    lo_half = lane < HEAD_DIM
    head_lanes = lane < FOX_HEADS
    w5 = FOX_WIDTH

    def pack3(x):
        hi, mid, lo = (p.astype(f32) for p in _split3(jnp.where(head_lanes, x, 0.0)))
        return (hi + pltpu.roll(mid, FOX_HEADS, axis=1)
                + pltpu.roll(lo, 2 * FOX_HEADS, axis=1)).astype(bf16)

    h = _rms(x_ref[...], g_ref[...]).astype(bf16)

    z = jnp.dot(h, wgate_ref[...], preferred_element_type=f32) + b_ref[...]
    logf = LOG2E * (jnp.minimum(z, 0.0) - jnp.log1p(jnp.exp(-jnp.abs(z))))
    cs = jnp.dot(tri_ref[...], pack3(logf), preferred_element_type=f32)
    c = (cs + pltpu.roll(cs, LANES - FOX_HEADS, axis=1)
         + pltpu.roll(cs, LANES - 2 * FOX_HEADS, axis=1) + carry_ref[0:1, :])
    last = c[tm - 1:tm, :]
    carry_ref[...] = jnp.broadcast_to(last, carry_ref.shape)
    cend_ref[pl.ds(step, 1), :] = last
    caug = jnp.dot(pack3(-c), place_ref[...], preferred_element_type=f32).astype(bf16)

    project = lambda w_ref: jnp.dot(h, w_ref[...], preferred_element_type=f32)
    fk = project(wfk_ref).astype(bf16)
    dk = project(wdk_ref).astype(bf16)
    n2 = jnp.dot(jnp.concatenate([fk * fk, dk * dk], axis=1), hsum_ref[...],
                 preferred_element_type=f32)
    kn2_ref[...] = jnp.maximum(kn2_ref[...],
                               jnp.broadcast_to(jnp.max(n2, axis=0, keepdims=True), kn2_ref.shape))

    q_ref[:, :w5] = (project(wfq_ref) * Q_SCALE).astype(bf16)
    q_ref[:, w5:] = (project(wdq_ref) * Q_SCALE).astype(bf16)
    fv = project(wfv_ref).astype(bf16)
    dv_ref[...] = project(wdv_ref).astype(bf16)

    pos = step * tm + lax.broadcasted_iota(jnp.int32, (tm, LANES), 0)
    pos_lanes = jnp.where(lane < 3, pos & ~(LANES - 1),
                          jnp.where(lane < 6, pos & (LANES - 1), 0)).astype(f32).astype(bf16)
    for g in range(FOX_WIDTH // LANES):
        cols = slice(g * LANES, (g + 1) * LANES)
        key = slice(2 * g * LANES, (2 * g + 1) * LANES)
        bias = slice((2 * g + 1) * LANES, (2 * g + 2) * LANES)
        fk_ref[:, key] = fk[:, cols]
        fk_ref[:, bias] = caug[:, cols]
        dk_ref[:, key] = dk[:, cols]
        dk_ref[:, bias] = pos_lanes
        fv_ref[:, key] = jnp.where(lo_half, fv[:, cols], (lane == HEAD_DIM).astype(bf16))
        fv_ref[:, bias] = jnp.where(lo_half, (lane == 0).astype(bf16), fv[:, cols])


def _in_proj(x, g, weights, b, place, hsum):
    s = x.shape[0]
    n_tiles = s // TM_IN
    const = lambda a: pl.BlockSpec(a.shape, lambda i: (0, 0))
    row_block = lambda width: pl.BlockSpec((TM_IN, width), lambda i: (i, 0))
    bf16 = jnp.bfloat16
    return pl.pallas_call(
        _in_proj_kernel,
        grid=(n_tiles,),
        in_specs=([row_block(D_MODEL), const(g)] + [const(w) for w in weights]
                  + [const(b), const(place), const(hsum)]),
        out_specs=[
            row_block(2 * FOX_WIDTH),
            row_block(2 * FOX_WIDTH),
            row_block(2 * FOX_WIDTH),
            row_block(2 * DIFF_WIDTH),
            row_block(DIFF_WIDTH),
            pl.BlockSpec((n_tiles, LANES), lambda i: (0, 0)),
            pl.BlockSpec((SUBLANES, LANES), lambda i: (0, 0)),
        ],
        out_shape=[
            jax.ShapeDtypeStruct((s, 2 * FOX_WIDTH), bf16),
            jax.ShapeDtypeStruct((s, 2 * FOX_WIDTH), bf16),
            jax.ShapeDtypeStruct((s, 2 * FOX_WIDTH), bf16),
            jax.ShapeDtypeStruct((s, 2 * DIFF_WIDTH), bf16),
            jax.ShapeDtypeStruct((s, DIFF_WIDTH), bf16),
            jax.ShapeDtypeStruct((n_tiles, LANES), jnp.float32),
            jax.ShapeDtypeStruct((SUBLANES, LANES), jnp.float32),
        ],
        scratch_shapes=[
            pltpu.VMEM((TM_IN, TM_IN), bf16),
            pltpu.VMEM((SUBLANES, LANES), jnp.float32),
        ],
        compiler_params=pltpu.CompilerParams(
            dimension_semantics=("arbitrary",), vmem_limit_bytes=VMEM_LIMIT),
        name="in_proj",
    )(x, g, *weights, b, place, hsum)


def _attn_kernel(*refs, fox):
    def one_tile(t, carry):
        rows = pl.ds(pl.multiple_of(t * TQ, TQ), TQ)
        _attn_query_tile(pl.program_id(1) * Q_TILES_PER_STEP + t, rows, refs, fox)
        return carry

    lax.fori_loop(0, Q_TILES_PER_STEP, one_tile, 0)


def _attn_query_tile(i, q_rows, refs, fox):
    if fox:
        q_ref, kaug_ref, vaug_ref, kn2_ref, cend_ref, o_ref, m_ref, acc_ref = refs
    else:
        (q_ref, kaug_ref, v_ref, kn2_ref, lq1_ref, lk1_ref, lq2_ref, lk2_ref, gn_ref,
         o_ref, m_ref, acc_ref, l_ref) = refs
    grp = pl.program_id(0)
    tq, tk = TQ, TK
    n_tiles = kaug_ref.shape[0] // tk
    f32, bf16 = jnp.float32, jnp.bfloat16

    lane = lax.broadcasted_iota(jnp.int32, (1, LANES), 1)
    lo_half = lane < HEAD_DIM

    if not fox:
        slope = LOG2E * jnp.exp2(-2.0 * jnp.full((1, 1), grp + 1, jnp.int32).astype(f32))
        sl = [p.astype(f32) for p in _split3(slope)]
        inv_slope = 1.0 / slope

    def half_norms_sq(x):
        l_in = lax.broadcasted_iota(jnp.int32, (LANES, LANES), 0)
        l_out = lax.broadcasted_iota(jnp.int32, (LANES, LANES), 1)
        half_sum = ((l_in // HEAD_DIM) == l_out).astype(bf16)
        return jnp.dot(x * x, half_sum, preferred_element_type=f32)

    q = q_ref[q_rows, :]
    zero = jnp.zeros_like(q)
    if fox:
        q_bias = [jnp.broadcast_to(((lane >= 3 * h) & (lane < 3 * h + 3)).astype(bf16), q.shape)
                  for h in range(2)]
    else:
        parts = jnp.where((lane == 0) | (lane == 3), sl[0],
                          jnp.where((lane == 1) | (lane == 4), sl[1],
                                    jnp.where((lane == 2) | (lane == 5), sl[2], 0.0)))
        q_bias = [jnp.broadcast_to(parts.astype(bf16), q.shape)] * 2
    q2 = jnp.concatenate(
        [jnp.concatenate([jnp.where(lo_half, q, zero), q_bias[0]], axis=1),
         jnp.concatenate([jnp.where(lo_half, zero, q), q_bias[1]], axis=1)], axis=0)

    m_ref[...] = jnp.full_like(m_ref, NEG_INF)
    acc_ref[...] = jnp.zeros_like(acc_ref)
    if not fox:
        l_ref[...] = jnp.zeros_like(l_ref)

    def tile(j, n_sub, masked):
        width = n_sub * tk
        rows = pl.ds(pl.multiple_of(j * tk, tk), width)
        s = lax.dot_general(q2, kaug_ref[rows, :], _NT, preferred_element_type=f32)
        if masked == "diagonal_last":
            r = lax.broadcasted_iota(jnp.int32, (2 * tq, tk), 0) & (tq - 1)
            c = lax.broadcasted_iota(jnp.int32, (2 * tq, tk), 1)
            diag = jnp.where(c <= r, s[:, width - tk:], NEG_INF)
            s = diag if n_sub == 1 else jnp.concatenate([s[:, :width - tk], diag], axis=1)
        elif masked == "by_position":
            r = lax.broadcasted_iota(jnp.int32, (2 * tq, width), 0) & (tq - 1)
            c = lax.broadcasted_iota(jnp.int32, (2 * tq, width), 1)
            s = jnp.where(c + (j - i) * tk <= r, s, NEG_INF)
        m_prev = m_ref[...]
        m_next = jnp.maximum(m_prev, jnp.max(s, axis=1, keepdims=True))
        alpha = jnp.exp2(m_prev - m_next)
        p = jnp.exp2(s - jnp.tile(m_next, (1, width // LANES)))
        pb = p.astype(bf16)
        if fox:
            pv = jnp.concatenate(
                [jnp.dot(pb[:tq], vaug_ref[rows, :LANES], preferred_element_type=f32),
                 jnp.dot(pb[tq:], vaug_ref[rows, LANES:], preferred_element_type=f32)], axis=0)
        else:
            pv = jnp.dot(pb, v_ref[rows, :], preferred_element_type=f32)
            part = p[:, :LANES]
            for blk in range(1, width // LANES):
                part = part + p[:, blk * LANES:(blk + 1) * LANES]
            l_ref[...] = alpha * l_ref[...] + part
        acc_ref[...] = acc_ref[...] * alpha + pv
        m_ref[...] = m_next

    def first_step(first, masked):
        top = jnp.maximum(i - (first - 1), 0)
        qn = half_norms_sq(q)
        first_norm_lane = (0 if fox else FOX_HEADS) + 2 * grp
        kmax = [jnp.sqrt(jnp.sum(jnp.where(lane == first_norm_lane + h, kn2_ref[0:1, :], 0.0),
                                 axis=1, keepdims=True)) * NORM_SLACK for h in range(2)]
        bound = [jnp.sqrt(qn[:, h:h + 1]) * kmax[h] for h in range(2)]
        tile(top, first, masked)
        m_col = [m_ref[h * tq:(h + 1) * tq, 0:1] for h in range(2)]
        if fox:
            head_lane = lax.broadcasted_iota(jnp.int32, cend_ref.shape, 1) - 2 * grp
            cend = [jnp.sum(jnp.where(head_lane == h, cend_ref[...], 0.0), axis=1, keepdims=True)
                    for h in range(2)]
            below_top = lax.broadcasted_iota(jnp.int32, (n_tiles, 1), 0) < top
            dead = below_top
            for h in range(2):
                gap = _reduce_rows(bound[h] - m_col[h], jnp.maximum)
                dead = dead & (-cend[h] < -UNDERFLOW_LOG2 - gap)
            return _reduce_rows(dead.astype(f32), jnp.add).astype(jnp.int32)[0, 0], top
        gap = _reduce_rows(jnp.maximum(bound[0] - m_col[0], bound[1] - m_col[1]), jnp.maximum)
        x = (-UNDERFLOW_LOG2 - gap) * inv_slope
        j_lo = jnp.clip(jnp.floor(x * (1.0 / tk)), 0.0, 1.0 * n_tiles)
        return jnp.minimum(j_lo.astype(jnp.int32)[0, 0], top), top

    per_group = FIRST_TILES_FOX if fox else FIRST_TILES_DIFF
    sizes = sorted(set(per_group))
    which = sum(jnp.where(grp == g, sizes.index(size), 0) for g, size in enumerate(per_group))
    branches = []
    for size in sizes:
        branches += [functools.partial(first_step, size, "diagonal_last"),
                     functools.partial(first_step, size, "by_position")]
    early = i < sum(jnp.where(which == n, size - 1, 0) for n, size in enumerate(sizes))
    j_lo, top = lax.switch(2 * which + early.astype(jnp.int32), branches)

    n_live = top - j_lo
    one = n_live & 1
    two = n_live & 2

    @pl.when(one == 1)
    def _():
        tile(top - 1, 1, masked=False)

    @pl.when(two == 2)
    def _():
        tile(top - one - 2, 2, masked=False)

    def body(step, carry):
        tile(top - one - two - TILES_PER_STEP * (step + 1), TILES_PER_STEP, masked=False)
        return carry

    lax.fori_loop(0, n_live // TILES_PER_STEP, body, 0)

    acc = [acc_ref[h * tq:(h + 1) * tq, :] for h in range(2)]
    if fox:
        out = jnp.where(lo_half, acc[0] / acc[0][:, HEAD_DIM:HEAD_DIM + 1],
                        acc[1] / acc[1][:, 0:1])
    else:
        lam = (jnp.exp(jnp.sum(lq1_ref[...] * lk1_ref[...], axis=1, keepdims=True))
               - jnp.exp(jnp.sum(lq2_ref[...] * lk2_ref[...], axis=1, keepdims=True))
               + LAMBDA_INIT)
        l = jnp.sum(l_ref[...], axis=1, keepdims=True)
        a = acc[0] / l[:tq] - lam * (acc[1] / l[tq:])
        out = _rms(a, gn_ref[...]) * (1.0 - LAMBDA_INIT)
    o_ref[q_rows, :] = out.astype(o_ref.dtype)


def _attention(q, kaug, v, kn2, extra, *, fox):
    s = q.shape[0]
    groups = FOX_HEADS // 2 if fox else DIFF_HEADS
    q_col0 = 0 if fox else FOX_WIDTH // LANES
    resident = lambda width: pl.BlockSpec((s, width), lambda g, i: (0, g),
                                          pipeline_mode=pl.Buffered(1))
    small = lambda a: pl.BlockSpec(a.shape, lambda g, i: (0, 0))
    in_specs = [
        pl.BlockSpec((Q_TILES_PER_STEP * TQ, LANES), lambda g, i: (i, q_col0 + g)),
        resident(2 * LANES),
        resident(2 * LANES if fox else LANES),
        small(kn2),
    ] + [small(a) for a in extra]
    scratch = [
        pltpu.VMEM((2 * TQ, LANES), jnp.float32),
        pltpu.VMEM((2 * TQ, LANES), jnp.float32),
    ]
    if not fox:
        scratch.append(pltpu.VMEM((2 * TQ, LANES), jnp.float32))
    return pl.pallas_call(
        functools.partial(_attn_kernel, fox=fox),
        grid=(groups, s // (Q_TILES_PER_STEP * TQ)),
        in_specs=in_specs,
        out_specs=pl.BlockSpec((Q_TILES_PER_STEP * TQ, LANES), lambda g, i: (i, g)),
        out_shape=jax.ShapeDtypeStruct((s, groups * LANES), jnp.bfloat16),
        scratch_shapes=scratch,
        compiler_params=pltpu.CompilerParams(
            dimension_semantics=("arbitrary", "arbitrary"), vmem_limit_bytes=VMEM_LIMIT),
        name="fox_attn" if fox else "diff_attn",
    )(q, kaug, v, kn2, *extra)


def _out_ffn_kernel(x_ref, fo_ref, do_ref, wo_ref, gf_ref, wg_ref, wu_ref, wd_ref,
                    gl_ref, o_ref):
    x1 = (x_ref[...]
          + jnp.dot(fo_ref[...], wo_ref[:FOX_WIDTH, :], preferred_element_type=jnp.float32)
          + jnp.dot(do_ref[...], wo_ref[FOX_WIDTH:, :], preferred_element_type=jnp.float32))
    h = _rms(x1, gf_ref[...]).astype(jnp.bfloat16)
    g = jnp.dot(h, wg_ref[...], preferred_element_type=jnp.float32)
    u = jnp.dot(h, wu_ref[...], preferred_element_type=jnp.float32)
    a = (g * jax.nn.sigmoid(g) * u).astype(jnp.bfloat16)
    x2 = x1 + jnp.dot(a, wd_ref[...], preferred_element_type=jnp.float32)
    o_ref[...] = _rms(x2, gl_ref[...])


def _out_ffn(x, fo, do, wo, gf, wg, wu, wd, gl):
    s = x.shape[0]
    const = lambda shape: pl.BlockSpec(shape, lambda i: (0, 0), pipeline_mode=pl.Buffered(1))
    return pl.pallas_call(
        _out_ffn_kernel,
        grid=(s // TM_FFN,),
        in_specs=[
            pl.BlockSpec((TM_FFN, D_MODEL), lambda i: (i, 0)),
            pl.BlockSpec((TM_FFN, FOX_WIDTH), lambda i: (i, 0)),
            pl.BlockSpec((TM_FFN, DIFF_WIDTH), lambda i: (i, 0)),
            const(wo.shape), const(gf.shape), const(wg.shape), const(wu.shape),
            const(wd.shape), const(gl.shape),
        ],
        out_specs=pl.BlockSpec((TM_FFN, D_MODEL), lambda i: (i, 0)),
        out_shape=jax.ShapeDtypeStruct((s, D_MODEL), jnp.float32),
        compiler_params=pltpu.CompilerParams(
            dimension_semantics=("arbitrary",), vmem_limit_bytes=VMEM_LIMIT),
        name="out_ffn",
    )(x, fo, do, wo, gf, wg, wu, wd, gl)


def kernel(x, mix_norm_g, w_in, b_forget, lambda_q1, lambda_k1, lambda_q2, lambda_k2,
           diff_norm_g, w_out, ffn_norm_g, w_gate, w_up, w_down, final_norm_g):
    b, s, d = x.shape
    assert b == 1 and d == D_MODEL and w_in.shape[0] == 1
    assert s % TQ == 0 and s % TM_IN == 0 and s % TM_FFN == 0
    assert TM_IN == TK and TQ == TK and TILES_PER_STEP == 4
    assert s >= max(FIRST_TILES_FOX + FIRST_TILES_DIFF) * TK and s % (Q_TILES_PER_STEP * TQ) == 0
    assert s < 2 ** 14 + 1
    bf16 = jnp.bfloat16
    x2d = x.reshape(s, d)

    w = w_in[0]
    col = np.cumsum((0, FOX_WIDTH, FOX_WIDTH, FOX_WIDTH, FOX_HEADS, DIFF_WIDTH, DIFF_WIDTH,
                     DIFF_WIDTH))
    fq, fk, fv, gate, dq, dk, dv = (w[:, col[n]:col[n + 1]] for n in range(7))
    order = jnp.argsort(b_forget[0])
    by_head = lambda t: t.reshape(d, FOX_HEADS, HEAD_DIM)[:, order, :].reshape(d, FOX_WIDTH)
    weights = [t.astype(bf16) for t in (
        by_head(fk), dk, by_head(fq), dq, by_head(fv), dv,
        jnp.pad(gate[:, order], ((0, 0), (0, GATE_PAD - FOX_HEADS))))]
    b_pad = jnp.pad(b_forget[0][order].astype(jnp.float32),
                    (0, GATE_PAD - FOX_HEADS)).reshape(1, GATE_PAD)
    w_o = w_out[0]
    w_o = jnp.concatenate(
        [w_o[:FOX_WIDTH].reshape(FOX_HEADS, HEAD_DIM, d)[order].reshape(FOX_WIDTH, d),
         w_o[FOX_WIDTH:]], axis=0).astype(bf16)
    place = jnp.asarray(_fox_aug_placement(), bf16)
    hsum = jnp.asarray(np.arange(2 * FOX_WIDTH)[:, None] // HEAD_DIM == np.arange(LANES)[None, :],
                       bf16)

    q, fox_k, fox_v, diff_k, diff_v, cend, kn2 = _in_proj(
        x2d, mix_norm_g[0].reshape(1, d), weights, b_pad, place, hsum)

    fox_out = _attention(q, fox_k, fox_v, kn2, (cend,), fox=True)
    lam_args = tuple(a[0].astype(jnp.float32).reshape(1, HEAD_DIM)
                     for a in (lambda_q1, lambda_k1, lambda_q2, lambda_k2))
    diff_out = _attention(
        q, diff_k, diff_v, kn2,
        lam_args + (diff_norm_g[0].astype(jnp.float32).reshape(1, DIFF_V_DIM),), fox=False)

    out = _out_ffn(
        x2d, fox_out, diff_out, w_o, ffn_norm_g[0].reshape(1, d),
        w_gate[0].astype(bf16), w_up[0].astype(bf16), w_down[0].astype(bf16),
        final_norm_g.reshape(1, d))
    return out.reshape(b, s, d)
```

```python
import functools
import math

import numpy as np
import jax
import jax.numpy as jnp
from jax import lax
from jax.experimental import pallas as pl
from jax.experimental.pallas import tpu as pltpu

D_MODEL = 1024
HEAD_DIM = 64
FOX_HEADS = 8
FOX_WIDTH = FOX_HEADS * HEAD_DIM
DIFF_HEADS = 4
DIFF_V_DIM = 2 * HEAD_DIM
DIFF_WIDTH = DIFF_HEADS * DIFF_V_DIM
EPS = 1e-6
NEG_INF = -1e30
LOG2E = math.log2(math.e)
Q_SCALE = LOG2E / math.sqrt(HEAD_DIM)
UNDERFLOW_LOG2 = 127.0
NORM_SLACK = 1.01
LAMBDA_INIT = 0.8 - 0.6 * math.exp(-0.3 * 0)

LANES = 128
SUBLANES = 8
GATE_PAD = LANES
VMEM_LIMIT = 56 * 1024 * 1024

TM_IN = 512
TQ = 512
TK = 512
TILES_PER_STEP = 4
FIRST_TILE_CHOICES = (2, 3, 4)
Q_TILES_PER_STEP = 4
TM_FFN = 512

_NT = (((1,), (1,)), ((), ()))


def _rms(x, g):
    return x * lax.rsqrt(jnp.mean(x * x, axis=-1, keepdims=True) + EPS) * g


def _split3(x):
    hi = x.astype(jnp.bfloat16)
    r1 = x - hi.astype(jnp.float32)
    mid = r1.astype(jnp.bfloat16)
    lo = (r1 - mid.astype(jnp.float32)).astype(jnp.bfloat16)
    return hi, mid, lo


def _reduce_rows(x, op):
    while x.shape[0] > 8 and x.shape[0] % 2 == 0:
        half = x.shape[0] // 2
        x = op(x[:half], x[half:])
    reduce = jnp.max if op is jnp.maximum else jnp.sum
    return reduce(x, axis=0, keepdims=True)


def _fox_aug_placement():
    p = np.zeros((LANES, FOX_WIDTH), np.float32)
    for part in range(3):
        for head in range(FOX_HEADS):
            lane = 3 * (head % 2) + part
            p[part * FOX_HEADS + head, (head // 2) * LANES + lane] = 1.0
    return p


def _in_proj_kernel(x_ref, g_ref, wfk_ref, wdk_ref, wfq_ref, wdq_ref, wfv_ref, wdv_ref,
                    wgate_ref, b_ref, place_ref, hsum_ref,
                    q_ref, fk_ref, fv_ref, dk_ref, dv_ref, cend_ref, kn2_ref,
                    tri_ref, carry_ref):
    step = pl.program_id(0)
    tm = x_ref.shape[0]
    f32, bf16 = jnp.float32, jnp.bfloat16

    @pl.when(step == 0)
    def _():
        r = lax.broadcasted_iota(jnp.int32, (tm, tm), 0)
        c = lax.broadcasted_iota(jnp.int32, (tm, tm), 1)
        tri_ref[...] = jnp.where(r >= c, 1.0, 0.0).astype(bf16)
        carry_ref[...] = jnp.zeros_like(carry_ref)
        kn2_ref[...] = jnp.zeros_like(kn2_ref)

    lane = lax.broadcasted_iota(jnp.int32, (1, LANES), 1)
    lo_half = lane < HEAD_DIM
    head_lanes = lane < FOX_HEADS
    w5 = FOX_WIDTH

    def pack3(x):
        hi, mid, lo = (p.astype(f32) for p in _split3(jnp.where(head_lanes, x, 0.0)))
        return (hi + pltpu.roll(mid, FOX_HEADS, axis=1)
                + pltpu.roll(lo, 2 * FOX_HEADS, axis=1)).astype(bf16)

    h = _rms(x_ref[...], g_ref[...]).astype(bf16)

    z = jnp.dot(h, wgate_ref[...], preferred_element_type=f32) + b_ref[...]
    logf = LOG2E * (jnp.minimum(z, 0.0) - jnp.log1p(jnp.exp(-jnp.abs(z))))
    cs = jnp.dot(tri_ref[...], pack3(logf), preferred_element_type=f32)
    c = (cs + pltpu.roll(cs, LANES - FOX_HEADS, axis=1)
         + pltpu.roll(cs, LANES - 2 * FOX_HEADS, axis=1) + carry_ref[0:1, :])
    last = c[tm - 1:tm, :]
    carry_ref[...] = jnp.broadcast_to(last, carry_ref.shape)
    cend_ref[pl.ds(step, 1), :] = last
    caug = jnp.dot(pack3(-c), place_ref[...], preferred_element_type=f32).astype(bf16)

    project = lambda w_ref: jnp.dot(h, w_ref[...], preferred_element_type=f32)
    fk = project(wfk_ref).astype(bf16)
    dk = project(wdk_ref).astype(bf16)
    n2 = jnp.dot(jnp.concatenate([fk * fk, dk * dk], axis=1), hsum_ref[...],
                 preferred_element_type=f32)
    kn2_ref[...] = jnp.maximum(kn2_ref[...],
                               jnp.broadcast_to(jnp.max(n2, axis=0, keepdims=True), kn2_ref.shape))

    q_ref[:, :w5] = (project(wfq_ref) * Q_SCALE).astype(bf16)
    q_ref[:, w5:] = (project(wdq_ref) * Q_SCALE).astype(bf16)
    fv = project(wfv_ref).astype(bf16)
    dv_ref[...] = project(wdv_ref).astype(bf16)

    pos = step * tm + lax.broadcasted_iota(jnp.int32, (tm, LANES), 0)
    pos_lanes = jnp.where(lane < 3, pos & ~(LANES - 1),
                          jnp.where(lane < 6, pos & (LANES - 1), 0)).astype(f32).astype(bf16)
    for g in range(FOX_WIDTH // LANES):
        cols = slice(g * LANES, (g + 1) * LANES)
        key = slice(2 * g * LANES, (2 * g + 1) * LANES)
        bias = slice((2 * g + 1) * LANES, (2 * g + 2) * LANES)
        fk_ref[:, key] = fk[:, cols]
        fk_ref[:, bias] = caug[:, cols]
        dk_ref[:, key] = dk[:, cols]
        dk_ref[:, bias] = pos_lanes
        fv_ref[:, key] = jnp.where(lo_half, fv[:, cols], (lane == HEAD_DIM).astype(bf16))
        fv_ref[:, bias] = jnp.where(lo_half, (lane == 0).astype(bf16), fv[:, cols])


def _in_proj(x, g, weights, b, place, hsum):
    s = x.shape[0]
    n_tiles = s // TM_IN
    const = lambda a: pl.BlockSpec(a.shape, lambda i: (0, 0))
    row_block = lambda width: pl.BlockSpec((TM_IN, width), lambda i: (i, 0))
    bf16 = jnp.bfloat16
    return pl.pallas_call(
        _in_proj_kernel,
        grid=(n_tiles,),
        in_specs=([row_block(D_MODEL), const(g)] + [const(w) for w in weights]
                  + [const(b), const(place), const(hsum)]),
        out_specs=[
            row_block(2 * FOX_WIDTH),
            row_block(2 * FOX_WIDTH),
            row_block(2 * FOX_WIDTH),
            row_block(2 * DIFF_WIDTH),
            row_block(DIFF_WIDTH),
            pl.BlockSpec((n_tiles, LANES), lambda i: (0, 0)),
            pl.BlockSpec((SUBLANES, LANES), lambda i: (0, 0)),
        ],
        out_shape=[
            jax.ShapeDtypeStruct((s, 2 * FOX_WIDTH), bf16),
            jax.ShapeDtypeStruct((s, 2 * FOX_WIDTH), bf16),
            jax.ShapeDtypeStruct((s, 2 * FOX_WIDTH), bf16),
            jax.ShapeDtypeStruct((s, 2 * DIFF_WIDTH), bf16),
            jax.ShapeDtypeStruct((s, DIFF_WIDTH), bf16),
            jax.ShapeDtypeStruct((n_tiles, LANES), jnp.float32),
            jax.ShapeDtypeStruct((SUBLANES, LANES), jnp.float32),
        ],
        scratch_shapes=[
            pltpu.VMEM((TM_IN, TM_IN), bf16),
            pltpu.VMEM((SUBLANES, LANES), jnp.float32),
        ],
        compiler_params=pltpu.CompilerParams(
            dimension_semantics=("arbitrary",), vmem_limit_bytes=VMEM_LIMIT),
        name="in_proj",
    )(x, g, *weights, b, place, hsum)


def _attn_kernel(*refs, fox):
    def one_tile(t, carry):
        rows = pl.ds(pl.multiple_of(t * TQ, TQ), TQ)
        _attn_query_tile(pl.program_id(1) * Q_TILES_PER_STEP + t, rows, refs, fox)
        return carry

    lax.fori_loop(0, Q_TILES_PER_STEP, one_tile, 0)


def _attn_query_tile(i, q_rows, refs, fox):
    if fox:
        q_ref, kaug_ref, vaug_ref, kn2_ref, cend_ref, o_ref, reach_ref, m_ref, acc_ref = refs
    else:
        (q_ref, kaug_ref, v_ref, kn2_ref, lq1_ref, lk1_ref, lq2_ref, lk2_ref, gn_ref,
         o_ref, reach_ref, m_ref, acc_ref, l_ref) = refs
    grp = pl.program_id(0)
    tq, tk = TQ, TK
    n_tiles = kaug_ref.shape[0] // tk
    f32, bf16 = jnp.float32, jnp.bfloat16

    lane = lax.broadcasted_iota(jnp.int32, (1, LANES), 1)
    lo_half = lane < HEAD_DIM

    if not fox:
        slope = LOG2E * jnp.exp2(-2.0 * jnp.full((1, 1), grp + 1, jnp.int32).astype(f32))
        sl = [p.astype(f32) for p in _split3(slope)]
        inv_slope = 1.0 / slope

    def half_norms_sq(x):
        l_in = lax.broadcasted_iota(jnp.int32, (LANES, LANES), 0)
        l_out = lax.broadcasted_iota(jnp.int32, (LANES, LANES), 1)
        half_sum = ((l_in // HEAD_DIM) == l_out).astype(bf16)
        return jnp.dot(x * x, half_sum, preferred_element_type=f32)

    q = q_ref[q_rows, :]
    zero = jnp.zeros_like(q)
    if fox:
        q_bias = [jnp.broadcast_to(((lane >= 3 * h) & (lane < 3 * h + 3)).astype(bf16), q.shape)
                  for h in range(2)]
    else:
        parts = jnp.where((lane == 0) | (lane == 3), sl[0],
                          jnp.where((lane == 1) | (lane == 4), sl[1],
                                    jnp.where((lane == 2) | (lane == 5), sl[2], 0.0)))
        q_bias = [jnp.broadcast_to(parts.astype(bf16), q.shape)] * 2
    q2 = jnp.concatenate(
        [jnp.concatenate([jnp.where(lo_half, q, zero), q_bias[0]], axis=1),
         jnp.concatenate([jnp.where(lo_half, zero, q), q_bias[1]], axis=1)], axis=0)

    m_ref[...] = jnp.full_like(m_ref, NEG_INF)
    acc_ref[...] = jnp.zeros_like(acc_ref)
    if not fox:
        l_ref[...] = jnp.zeros_like(l_ref)

    def tile(j, n_sub, masked):
        width = n_sub * tk
        rows = pl.ds(pl.multiple_of(j * tk, tk), width)
        s = lax.dot_general(q2, kaug_ref[rows, :], _NT, preferred_element_type=f32)
        if masked == "diagonal_last":
            r = lax.broadcasted_iota(jnp.int32, (2 * tq, tk), 0) & (tq - 1)
            c = lax.broadcasted_iota(jnp.int32, (2 * tq, tk), 1)
            diag = jnp.where(c <= r, s[:, width - tk:], NEG_INF)
            s = diag if n_sub == 1 else jnp.concatenate([s[:, :width - tk], diag], axis=1)
        elif masked == "by_position":
            r = lax.broadcasted_iota(jnp.int32, (2 * tq, width), 0) & (tq - 1)
            c = lax.broadcasted_iota(jnp.int32, (2 * tq, width), 1)
            s = jnp.where(c + (j - i) * tk <= r, s, NEG_INF)
        m_prev = m_ref[...]
        m_next = jnp.maximum(m_prev, jnp.max(s, axis=1, keepdims=True))
        alpha = jnp.exp2(m_prev - m_next)
        p = jnp.exp2(s - jnp.tile(m_next, (1, width // LANES)))
        pb = p.astype(bf16)
        if fox:
            pv = jnp.concatenate(
                [jnp.dot(pb[:tq], vaug_ref[rows, :LANES], preferred_element_type=f32),
                 jnp.dot(pb[tq:], vaug_ref[rows, LANES:], preferred_element_type=f32)], axis=0)
        else:
            pv = jnp.dot(pb, v_ref[rows, :], preferred_element_type=f32)
            part = p[:, :LANES]
            for blk in range(1, width // LANES):
                part = part + p[:, blk * LANES:(blk + 1) * LANES]
            l_ref[...] = alpha * l_ref[...] + part
        acc_ref[...] = acc_ref[...] * alpha + pv
        m_ref[...] = m_next

    def first_step(first, masked):
        top = jnp.maximum(i - (first - 1), 0)
        qn = half_norms_sq(q)
        first_norm_lane = (0 if fox else FOX_HEADS) + 2 * grp
        kmax = [jnp.sqrt(jnp.sum(jnp.where(lane == first_norm_lane + h, kn2_ref[0:1, :], 0.0),
                                 axis=1, keepdims=True)) * NORM_SLACK for h in range(2)]
        bound = [jnp.sqrt(qn[:, h:h + 1]) * kmax[h] for h in range(2)]
        tile(top, first, masked)
        m_col = [m_ref[h * tq:(h + 1) * tq, 0:1] for h in range(2)]
        if fox:
            head_lane = lax.broadcasted_iota(jnp.int32, cend_ref.shape, 1) - 2 * grp
            cend = [jnp.sum(jnp.where(head_lane == h, cend_ref[...], 0.0), axis=1, keepdims=True)
                    for h in range(2)]
            below_top = lax.broadcasted_iota(jnp.int32, (n_tiles, 1), 0) < top
            dead = below_top
            for h in range(2):
                gap = _reduce_rows(bound[h] - m_col[h], jnp.maximum)
                dead = dead & (-cend[h] < -UNDERFLOW_LOG2 - gap)
            return _reduce_rows(dead.astype(f32), jnp.add).astype(jnp.int32)[0, 0], top
        gap = _reduce_rows(jnp.maximum(bound[0] - m_col[0], bound[1] - m_col[1]), jnp.maximum)
        x = (-UNDERFLOW_LOG2 - gap) * inv_slope
        j_lo = jnp.clip(jnp.floor(x * (1.0 / tk)), 0.0, 1.0 * n_tiles)
        return jnp.minimum(j_lo.astype(jnp.int32)[0, 0], top), top

    sizes = FIRST_TILE_CHOICES
    @pl.when(i == 0)
    def _():
        reach_ref[0] = sizes[0]

    which = jnp.clip(reach_ref[0], sizes[0], sizes[-1]) - sizes[0]
    branches = []
    for size in sizes:
        branches += [functools.partial(first_step, size, "diagonal_last"),
                     functools.partial(first_step, size, "by_position")]
    early = i < which + sizes[0] - 1
    j_lo, top = lax.switch(2 * which + early.astype(jnp.int32), branches)
    reach_ref[0] = i + 1 - j_lo

    n_live = top - j_lo
    one = n_live & 1
    two = n_live & 2

    @pl.when(one == 1)
    def _():
        tile(top - 1, 1, masked=False)

    @pl.when(two == 2)
    def _():
        tile(top - one - 2, 2, masked=False)

    def body(step, carry):
        tile(top - one - two - TILES_PER_STEP * (step + 1), TILES_PER_STEP, masked=False)
        return carry

    lax.fori_loop(0, n_live // TILES_PER_STEP, body, 0)

    acc = [acc_ref[h * tq:(h + 1) * tq, :] for h in range(2)]
    if fox:
        out = jnp.where(lo_half, acc[0] / acc[0][:, HEAD_DIM:HEAD_DIM + 1],
                        acc[1] / acc[1][:, 0:1])
    else:
        lam = (jnp.exp(jnp.sum(lq1_ref[...] * lk1_ref[...], axis=1, keepdims=True))
               - jnp.exp(jnp.sum(lq2_ref[...] * lk2_ref[...], axis=1, keepdims=True))
               + LAMBDA_INIT)
        l = jnp.sum(l_ref[...], axis=1, keepdims=True)
        a = acc[0] / l[:tq] - lam * (acc[1] / l[tq:])
        out = _rms(a, gn_ref[...]) * (1.0 - LAMBDA_INIT)
    o_ref[q_rows, :] = out.astype(o_ref.dtype)


def _attention(q, kaug, v, kn2, extra, *, fox):
    s = q.shape[0]
    groups = FOX_HEADS // 2 if fox else DIFF_HEADS
    q_col0 = 0 if fox else FOX_WIDTH // LANES
    resident = lambda width: pl.BlockSpec((s, width), lambda g, i: (0, g),
                                          pipeline_mode=pl.Buffered(1))
    small = lambda a: pl.BlockSpec(a.shape, lambda g, i: (0, 0))
    in_specs = [
        pl.BlockSpec((Q_TILES_PER_STEP * TQ, LANES), lambda g, i: (i, q_col0 + g)),
        resident(2 * LANES),
        resident(2 * LANES if fox else LANES),
        small(kn2),
    ] + [small(a) for a in extra]
    scratch = [
        pltpu.SMEM((1,), jnp.int32),
        pltpu.VMEM((2 * TQ, LANES), jnp.float32),
        pltpu.VMEM((2 * TQ, LANES), jnp.float32),
    ]
    if not fox:
        scratch.append(pltpu.VMEM((2 * TQ, LANES), jnp.float32))
    return pl.pallas_call(
        functools.partial(_attn_kernel, fox=fox),
        grid=(groups, s // (Q_TILES_PER_STEP * TQ)),
        in_specs=in_specs,
        out_specs=pl.BlockSpec((Q_TILES_PER_STEP * TQ, LANES), lambda g, i: (i, g)),
        out_shape=jax.ShapeDtypeStruct((s, groups * LANES), jnp.bfloat16),
        scratch_shapes=scratch,
        compiler_params=pltpu.CompilerParams(
            dimension_semantics=("arbitrary", "arbitrary"), vmem_limit_bytes=VMEM_LIMIT),
        name="fox_attn" if fox else "diff_attn",
    )(q, kaug, v, kn2, *extra)


def _out_ffn_kernel(x_ref, fo_ref, do_ref, wo_ref, gf_ref, wg_ref, wu_ref, wd_ref,
                    gl_ref, o_ref):
    x1 = (x_ref[...]
          + jnp.dot(fo_ref[...], wo_ref[:FOX_WIDTH, :], preferred_element_type=jnp.float32)
          + jnp.dot(do_ref[...], wo_ref[FOX_WIDTH:, :], preferred_element_type=jnp.float32))
    h = _rms(x1, gf_ref[...]).astype(jnp.bfloat16)
    g = jnp.dot(h, wg_ref[...], preferred_element_type=jnp.float32)
    u = jnp.dot(h, wu_ref[...], preferred_element_type=jnp.float32)
    a = (g * jax.nn.sigmoid(g) * u).astype(jnp.bfloat16)
    x2 = x1 + jnp.dot(a, wd_ref[...], preferred_element_type=jnp.float32)
    o_ref[...] = _rms(x2, gl_ref[...])


def _out_ffn(x, fo, do, wo, gf, wg, wu, wd, gl):
    s = x.shape[0]
    const = lambda shape: pl.BlockSpec(shape, lambda i: (0, 0), pipeline_mode=pl.Buffered(1))
    return pl.pallas_call(
        _out_ffn_kernel,
        grid=(s // TM_FFN,),
        in_specs=[
            pl.BlockSpec((TM_FFN, D_MODEL), lambda i: (i, 0)),
            pl.BlockSpec((TM_FFN, FOX_WIDTH), lambda i: (i, 0)),
            pl.BlockSpec((TM_FFN, DIFF_WIDTH), lambda i: (i, 0)),
            const(wo.shape), const(gf.shape), const(wg.shape), const(wu.shape),
            const(wd.shape), const(gl.shape),
        ],
        out_specs=pl.BlockSpec((TM_FFN, D_MODEL), lambda i: (i, 0)),
        out_shape=jax.ShapeDtypeStruct((s, D_MODEL), jnp.float32),
        compiler_params=pltpu.CompilerParams(
            dimension_semantics=("arbitrary",), vmem_limit_bytes=VMEM_LIMIT),
        name="out_ffn",
    )(x, fo, do, wo, gf, wg, wu, wd, gl)


def kernel(x, mix_norm_g, w_in, b_forget, lambda_q1, lambda_k1, lambda_q2, lambda_k2,
           diff_norm_g, w_out, ffn_norm_g, w_gate, w_up, w_down, final_norm_g):
    b, s, d = x.shape
    assert b == 1 and d == D_MODEL and w_in.shape[0] == 1
    assert s % TQ == 0 and s % TM_IN == 0 and s % TM_FFN == 0
    assert TM_IN == TK and TQ == TK and TILES_PER_STEP == 4
    assert s >= max(FIRST_TILE_CHOICES) * TK and s % (Q_TILES_PER_STEP * TQ) == 0
    assert FIRST_TILE_CHOICES == tuple(range(FIRST_TILE_CHOICES[0], FIRST_TILE_CHOICES[-1] + 1))
    assert s < 2 ** 14 + 1
    bf16 = jnp.bfloat16
    x2d = x.reshape(s, d)

    w = w_in[0]
    col = np.cumsum((0, FOX_WIDTH, FOX_WIDTH, FOX_WIDTH, FOX_HEADS, DIFF_WIDTH, DIFF_WIDTH,
                     DIFF_WIDTH))
    fq, fk, fv, gate, dq, dk, dv = (w[:, col[n]:col[n + 1]] for n in range(7))
    order = jnp.argsort(b_forget[0])
    by_head = lambda t: t.reshape(d, FOX_HEADS, HEAD_DIM)[:, order, :].reshape(d, FOX_WIDTH)
    weights = [t.astype(bf16) for t in (
        by_head(fk), dk, by_head(fq), dq, by_head(fv), dv,
        jnp.pad(gate[:, order], ((0, 0), (0, GATE_PAD - FOX_HEADS))))]
    b_pad = jnp.pad(b_forget[0][order].astype(jnp.float32),
                    (0, GATE_PAD - FOX_HEADS)).reshape(1, GATE_PAD)
    w_o = w_out[0]
    w_o = jnp.concatenate(
        [w_o[:FOX_WIDTH].reshape(FOX_HEADS, HEAD_DIM, d)[order].reshape(FOX_WIDTH, d),
         w_o[FOX_WIDTH:]], axis=0).astype(bf16)
    place = jnp.asarray(_fox_aug_placement(), bf16)
    hsum = jnp.asarray(np.arange(2 * FOX_WIDTH)[:, None] // HEAD_DIM == np.arange(LANES)[None, :],
                       bf16)

    q, fox_k, fox_v, diff_k, diff_v, cend, kn2 = _in_proj(
        x2d, mix_norm_g[0].reshape(1, d), weights, b_pad, place, hsum)

    fox_out = _attention(q, fox_k, fox_v, kn2, (cend,), fox=True)
    lam_args = tuple(a[0].astype(jnp.float32).reshape(1, HEAD_DIM)
                     for a in (lambda_q1, lambda_k1, lambda_q2, lambda_k2))
    diff_out = _attention(
        q, diff_k, diff_v, kn2,
        lam_args + (diff_norm_g[0].astype(jnp.float32).reshape(1, DIFF_V_DIM),), fox=False)

    out = _out_ffn(
        x2d, fox_out, diff_out, w_o, ffn_norm_g[0].reshape(1, d),
        w_gate[0].astype(bf16), w_up[0].astype(bf16), w_down[0].astype(bf16),
        final_norm_g.reshape(1, d))
    return out.reshape(b, s, d)
```

```python
import functools
import math

import numpy as np
import jax
import jax.numpy as jnp
from jax import lax
from jax.experimental import pallas as pl
from jax.experimental.pallas import tpu as pltpu

D_MODEL = 1024
HEAD_DIM = 64
FOX_HEADS = 8
FOX_WIDTH = FOX_HEADS * HEAD_DIM
DIFF_HEADS = 4
DIFF_V_DIM = 2 * HEAD_DIM
DIFF_WIDTH = DIFF_HEADS * DIFF_V_DIM
EPS = 1e-6
NEG_INF = -1e30
LOG2E = math.log2(math.e)
Q_SCALE = LOG2E / math.sqrt(HEAD_DIM)
UNDERFLOW_LOG2 = 127.0
NORM_SLACK = 1.01
LAMBDA_INIT = 0.8 - 0.6 * math.exp(-0.3 * 0)

LANES = 128
SUBLANES = 8
GATE_PAD = LANES
VMEM_LIMIT = 56 * 1024 * 1024

TM_IN = 512
TQ = 512
TK = 512
TILES_PER_STEP = 4
FIRST_TILE_CHOICES = (2, 3, 4, 5, 6)
Q_TILES_PER_STEP = 4
TM_FFN = 512

_NT = (((1,), (1,)), ((), ()))


def _rms(x, g):
    return x * lax.rsqrt(jnp.mean(x * x, axis=-1, keepdims=True) + EPS) * g


def _split3(x):
    hi = x.astype(jnp.bfloat16)
    r1 = x - hi.astype(jnp.float32)
    mid = r1.astype(jnp.bfloat16)
    lo = (r1 - mid.astype(jnp.float32)).astype(jnp.bfloat16)
    return hi, mid, lo


def _reduce_rows(x, op):
    while x.shape[0] > 8 and x.shape[0] % 2 == 0:
        half = x.shape[0] // 2
        x = op(x[:half], x[half:])
    reduce = jnp.max if op is jnp.maximum else jnp.sum
    return reduce(x, axis=0, keepdims=True)


def _fox_aug_placement():
    p = np.zeros((LANES, FOX_WIDTH), np.float32)
    for part in range(3):
        for head in range(FOX_HEADS):
            lane = 3 * (head % 2) + part
            p[part * FOX_HEADS + head, (head // 2) * LANES + lane] = 1.0
    return p


def _in_proj_kernel(x_ref, g_ref, wfk_ref, wdk_ref, wfq_ref, wdq_ref, wfv_ref, wdv_ref,
                    wgate_ref, b_ref, place_ref, hsum_ref,
                    q_ref, fk_ref, fv_ref, dk_ref, dv_ref, cend_ref, kn2_ref,
                    tri_ref, carry_ref):
    step = pl.program_id(0)
    tm = x_ref.shape[0]
    f32, bf16 = jnp.float32, jnp.bfloat16

    @pl.when(step == 0)
    def _():
        r = lax.broadcasted_iota(jnp.int32, (tm, tm), 0)
        c = lax.broadcasted_iota(jnp.int32, (tm, tm), 1)
        tri_ref[...] = jnp.where(r >= c, 1.0, 0.0).astype(bf16)
        carry_ref[...] = jnp.zeros_like(carry_ref)
        kn2_ref[...] = jnp.zeros_like(kn2_ref)

    lane = lax.broadcasted_iota(jnp.int32, (1, LANES), 1)
    lo_half = lane < HEAD_DIM
    head_lanes = lane < FOX_HEADS
    w5 = FOX_WIDTH

    def pack3(x):
        hi, mid, lo = (p.astype(f32) for p in _split3(jnp.where(head_lanes, x, 0.0)))
        return (hi + pltpu.roll(mid, FOX_HEADS, axis=1)
                + pltpu.roll(lo, 2 * FOX_HEADS, axis=1)).astype(bf16)

    h = _rms(x_ref[...], g_ref[...]).astype(bf16)

    z = jnp.dot(h, wgate_ref[...], preferred_element_type=f32) + b_ref[...]
    logf = LOG2E * (jnp.minimum(z, 0.0) - jnp.log1p(jnp.exp(-jnp.abs(z))))
    cs = jnp.dot(tri_ref[...], pack3(logf), preferred_element_type=f32)
    c = (cs + pltpu.roll(cs, LANES - FOX_HEADS, axis=1)
         + pltpu.roll(cs, LANES - 2 * FOX_HEADS, axis=1) + carry_ref[0:1, :])
    last = c[tm - 1:tm, :]
    carry_ref[...] = jnp.broadcast_to(last, carry_ref.shape)
    cend_ref[pl.ds(step, 1), :] = last
    caug = jnp.dot(pack3(-c), place_ref[...], preferred_element_type=f32).astype(bf16)

    project = lambda w_ref: jnp.dot(h, w_ref[...], preferred_element_type=f32)
    fk = project(wfk_ref).astype(bf16)
    dk = project(wdk_ref).astype(bf16)
    n2 = jnp.dot(jnp.concatenate([fk * fk, dk * dk], axis=1), hsum_ref[...],
                 preferred_element_type=f32)
    kn2_ref[...] = jnp.maximum(kn2_ref[...],
                               jnp.broadcast_to(jnp.max(n2, axis=0, keepdims=True), kn2_ref.shape))

    q_ref[:, :w5] = (project(wfq_ref) * Q_SCALE).astype(bf16)
    q_ref[:, w5:] = (project(wdq_ref) * Q_SCALE).astype(bf16)
    fv = project(wfv_ref).astype(bf16)
    dv_ref[...] = project(wdv_ref).astype(bf16)

    pos = step * tm + lax.broadcasted_iota(jnp.int32, (tm, LANES), 0)
    pos_lanes = jnp.where(lane < 3, pos & ~(LANES - 1),
                          jnp.where(lane < 6, pos & (LANES - 1), 0)).astype(f32).astype(bf16)
    for g in range(FOX_WIDTH // LANES):
        cols = slice(g * LANES, (g + 1) * LANES)
        key = slice(2 * g * LANES, (2 * g + 1) * LANES)
        bias = slice((2 * g + 1) * LANES, (2 * g + 2) * LANES)
        fk_ref[:, key] = fk[:, cols]
        fk_ref[:, bias] = caug[:, cols]
        dk_ref[:, key] = dk[:, cols]
        dk_ref[:, bias] = pos_lanes
        fv_ref[:, key] = jnp.where(lo_half, fv[:, cols], (lane == HEAD_DIM).astype(bf16))
        fv_ref[:, bias] = jnp.where(lo_half, (lane == 0).astype(bf16), fv[:, cols])


def _in_proj(x, g, weights, b, place, hsum):
    s = x.shape[0]
    n_tiles = s // TM_IN
    const = lambda a: pl.BlockSpec(a.shape, lambda i: (0, 0))
    row_block = lambda width: pl.BlockSpec((TM_IN, width), lambda i: (i, 0))
    bf16 = jnp.bfloat16
    return pl.pallas_call(
        _in_proj_kernel,
        grid=(n_tiles,),
        in_specs=([row_block(D_MODEL), const(g)] + [const(w) for w in weights]
                  + [const(b), const(place), const(hsum)]),
        out_specs=[
            row_block(2 * FOX_WIDTH),
            row_block(2 * FOX_WIDTH),
            row_block(2 * FOX_WIDTH),
            row_block(2 * DIFF_WIDTH),
            row_block(DIFF_WIDTH),
            pl.BlockSpec((n_tiles, LANES), lambda i: (0, 0)),
            pl.BlockSpec((SUBLANES, LANES), lambda i: (0, 0)),
        ],
        out_shape=[
            jax.ShapeDtypeStruct((s, 2 * FOX_WIDTH), bf16),
            jax.ShapeDtypeStruct((s, 2 * FOX_WIDTH), bf16),
            jax.ShapeDtypeStruct((s, 2 * FOX_WIDTH), bf16),
            jax.ShapeDtypeStruct((s, 2 * DIFF_WIDTH), bf16),
            jax.ShapeDtypeStruct((s, DIFF_WIDTH), bf16),
            jax.ShapeDtypeStruct((n_tiles, LANES), jnp.float32),
            jax.ShapeDtypeStruct((SUBLANES, LANES), jnp.float32),
        ],
        scratch_shapes=[
            pltpu.VMEM((TM_IN, TM_IN), bf16),
            pltpu.VMEM((SUBLANES, LANES), jnp.float32),
        ],
        compiler_params=pltpu.CompilerParams(
            dimension_semantics=("arbitrary",), vmem_limit_bytes=VMEM_LIMIT),
        name="in_proj",
    )(x, g, *weights, b, place, hsum)


def _attn_kernel(*refs, fox):
    def one_tile(t, carry):
        rows = pl.ds(pl.multiple_of(t * TQ, TQ), TQ)
        _attn_query_tile(pl.program_id(1) * Q_TILES_PER_STEP + t, rows, refs, fox)
        return carry

    lax.fori_loop(0, Q_TILES_PER_STEP, one_tile, 0)


def _attn_query_tile(i, q_rows, refs, fox):
    if fox:
        q_ref, kaug_ref, vaug_ref, kn2_ref, cend_ref, o_ref, reach_ref, m_ref, acc_ref = refs
    else:
        (q_ref, kaug_ref, v_ref, kn2_ref, lq1_ref, lk1_ref, lq2_ref, lk2_ref, gn_ref,
         o_ref, reach_ref, m_ref, acc_ref, l_ref) = refs
    grp = pl.program_id(0)
    tq, tk = TQ, TK
    n_tiles = kaug_ref.shape[0] // tk
    f32, bf16 = jnp.float32, jnp.bfloat16

    lane = lax.broadcasted_iota(jnp.int32, (1, LANES), 1)
    lo_half = lane < HEAD_DIM

    if not fox:
        slope = LOG2E * jnp.exp2(-2.0 * jnp.full((1, 1), grp + 1, jnp.int32).astype(f32))
        sl = [p.astype(f32) for p in _split3(slope)]
        inv_slope = 1.0 / slope

    def half_norms_sq(x):
        l_in = lax.broadcasted_iota(jnp.int32, (LANES, LANES), 0)
        l_out = lax.broadcasted_iota(jnp.int32, (LANES, LANES), 1)
        half_sum = ((l_in // HEAD_DIM) == l_out).astype(bf16)
        return jnp.dot(x * x, half_sum, preferred_element_type=f32)

    q = q_ref[q_rows, :]
    zero = jnp.zeros_like(q)
    if fox:
        q_bias = [jnp.broadcast_to(((lane >= 3 * h) & (lane < 3 * h + 3)).astype(bf16), q.shape)
                  for h in range(2)]
    else:
        parts = jnp.where((lane == 0) | (lane == 3), sl[0],
                          jnp.where((lane == 1) | (lane == 4), sl[1],
                                    jnp.where((lane == 2) | (lane == 5), sl[2], 0.0)))
        q_bias = [jnp.broadcast_to(parts.astype(bf16), q.shape)] * 2
    q2 = jnp.concatenate(
        [jnp.concatenate([jnp.where(lo_half, q, zero), q_bias[0]], axis=1),
         jnp.concatenate([jnp.where(lo_half, zero, q), q_bias[1]], axis=1)], axis=0)

    m_ref[...] = jnp.full_like(m_ref, NEG_INF)
    acc_ref[...] = jnp.zeros_like(acc_ref)
    if not fox:
        l_ref[...] = jnp.zeros_like(l_ref)

    def tile(j, n_sub, masked):
        width = n_sub * tk
        rows = pl.ds(pl.multiple_of(j * tk, tk), width)
        s = lax.dot_general(q2, kaug_ref[rows, :], _NT, preferred_element_type=f32)
        if masked == "diagonal_last":
            r = lax.broadcasted_iota(jnp.int32, (2 * tq, tk), 0) & (tq - 1)
            c = lax.broadcasted_iota(jnp.int32, (2 * tq, tk), 1)
            diag = jnp.where(c <= r, s[:, width - tk:], NEG_INF)
            s = diag if n_sub == 1 else jnp.concatenate([s[:, :width - tk], diag], axis=1)
        elif masked == "by_position":
            r = lax.broadcasted_iota(jnp.int32, (2 * tq, width), 0) & (tq - 1)
            c = lax.broadcasted_iota(jnp.int32, (2 * tq, width), 1)
            s = jnp.where(c + (j - i) * tk <= r, s, NEG_INF)
        m_prev = m_ref[...]
        m_next = jnp.maximum(m_prev, jnp.max(s, axis=1, keepdims=True))
        alpha = jnp.exp2(m_prev - m_next)
        p = jnp.exp2(s - jnp.tile(m_next, (1, width // LANES)))
        pb = p.astype(bf16)
        if fox:
            pv = jnp.concatenate(
                [jnp.dot(pb[:tq], vaug_ref[rows, :LANES], preferred_element_type=f32),
                 jnp.dot(pb[tq:], vaug_ref[rows, LANES:], preferred_element_type=f32)], axis=0)
        else:
            pv = jnp.dot(pb, v_ref[rows, :], preferred_element_type=f32)
            part = p[:, :LANES]
            for blk in range(1, width // LANES):
                part = part + p[:, blk * LANES:(blk + 1) * LANES]
            l_ref[...] = alpha * l_ref[...] + part
        acc_ref[...] = acc_ref[...] * alpha + pv
        m_ref[...] = m_next

    def first_step(first, masked):
        top = jnp.maximum(i - (first - 1), 0)
        qn = half_norms_sq(q)
        first_norm_lane = (0 if fox else FOX_HEADS) + 2 * grp
        kmax = [jnp.sqrt(jnp.sum(jnp.where(lane == first_norm_lane + h, kn2_ref[0:1, :], 0.0),
                                 axis=1, keepdims=True)) * NORM_SLACK for h in range(2)]
        bound = [jnp.sqrt(qn[:, h:h + 1]) * kmax[h] for h in range(2)]
        tile(top, first, masked)
        m_col = [m_ref[h * tq:(h + 1) * tq, 0:1] for h in range(2)]
        if fox:
            head_lane = lax.broadcasted_iota(jnp.int32, cend_ref.shape, 1) - 2 * grp
            cend = [jnp.sum(jnp.where(head_lane == h, cend_ref[...], 0.0), axis=1, keepdims=True)
                    for h in range(2)]
            below_top = lax.broadcasted_iota(jnp.int32, (n_tiles, 1), 0) < top
            dead = below_top
            for h in range(2):
                gap = _reduce_rows(bound[h] - m_col[h], jnp.maximum)
                dead = dead & (-cend[h] < -UNDERFLOW_LOG2 - gap)
            return _reduce_rows(dead.astype(f32), jnp.add).astype(jnp.int32)[0, 0], top
        gap = _reduce_rows(jnp.maximum(bound[0] - m_col[0], bound[1] - m_col[1]), jnp.maximum)
        x = (-UNDERFLOW_LOG2 - gap) * inv_slope
        j_lo = jnp.clip(jnp.floor(x * (1.0 / tk)), 0.0, 1.0 * n_tiles)
        return jnp.minimum(j_lo.astype(jnp.int32)[0, 0], top), top

    sizes = FIRST_TILE_CHOICES
    @pl.when(i == 0)
    def _():
        reach_ref[0] = sizes[0]

    which = jnp.clip(reach_ref[0], sizes[0], sizes[-1]) - sizes[0]
    branches = []
    for size in sizes:
        branches += [functools.partial(first_step, size, "diagonal_last"),
                     functools.partial(first_step, size, "by_position")]
    early = i < which + sizes[0] - 1
    j_lo, top = lax.switch(2 * which + early.astype(jnp.int32), branches)
    reach_ref[0] = i + 1 - j_lo

    n_live = top - j_lo
    one = n_live & 1
    two = n_live & 2

    @pl.when(one == 1)
    def _():
        tile(top - 1, 1, masked=False)

    @pl.when(two == 2)
    def _():
        tile(top - one - 2, 2, masked=False)

    def body(step, carry):
        tile(top - one - two - TILES_PER_STEP * (step + 1), TILES_PER_STEP, masked=False)
        return carry

    lax.fori_loop(0, n_live // TILES_PER_STEP, body, 0)

    acc = [acc_ref[h * tq:(h + 1) * tq, :] for h in range(2)]
    if fox:
        out = jnp.where(lo_half, acc[0] / acc[0][:, HEAD_DIM:HEAD_DIM + 1],
                        acc[1] / acc[1][:, 0:1])
    else:
        lam = (jnp.exp(jnp.sum(lq1_ref[...] * lk1_ref[...], axis=1, keepdims=True))
               - jnp.exp(jnp.sum(lq2_ref[...] * lk2_ref[...], axis=1, keepdims=True))
               + LAMBDA_INIT)
        l = jnp.sum(l_ref[...], axis=1, keepdims=True)
        a = acc[0] / l[:tq] - lam * (acc[1] / l[tq:])
        out = _rms(a, gn_ref[...]) * (1.0 - LAMBDA_INIT)
    o_ref[q_rows, :] = out.astype(o_ref.dtype)


def _attention(q, kaug, v, kn2, extra, *, fox):
    s = q.shape[0]
    groups = FOX_HEADS // 2 if fox else DIFF_HEADS
    q_col0 = 0 if fox else FOX_WIDTH // LANES
    resident = lambda width: pl.BlockSpec((s, width), lambda g, i: (0, g),
                                          pipeline_mode=pl.Buffered(1))
    small = lambda a: pl.BlockSpec(a.shape, lambda g, i: (0, 0))
    in_specs = [
        pl.BlockSpec((Q_TILES_PER_STEP * TQ, LANES), lambda g, i: (i, q_col0 + g)),
        resident(2 * LANES),
        resident(2 * LANES if fox else LANES),
        small(kn2),
    ] + [small(a) for a in extra]
    scratch = [
        pltpu.SMEM((1,), jnp.int32),
        pltpu.VMEM((2 * TQ, LANES), jnp.float32),
        pltpu.VMEM((2 * TQ, LANES), jnp.float32),
    ]
    if not fox:
        scratch.append(pltpu.VMEM((2 * TQ, LANES), jnp.float32))
    return pl.pallas_call(
        functools.partial(_attn_kernel, fox=fox),
        grid=(groups, s // (Q_TILES_PER_STEP * TQ)),
        in_specs=in_specs,
        out_specs=pl.BlockSpec((Q_TILES_PER_STEP * TQ, LANES), lambda g, i: (i, g)),
        out_shape=jax.ShapeDtypeStruct((s, groups * LANES), jnp.bfloat16),
        scratch_shapes=scratch,
        compiler_params=pltpu.CompilerParams(
            dimension_semantics=("arbitrary", "arbitrary"), vmem_limit_bytes=VMEM_LIMIT),
        name="fox_attn" if fox else "diff_attn",
    )(q, kaug, v, kn2, *extra)


def _out_ffn_kernel(x_ref, fo_ref, do_ref, wo_ref, gf_ref, wg_ref, wu_ref, wd_ref,
                    gl_ref, o_ref):
    x1 = (x_ref[...]
          + jnp.dot(fo_ref[...], wo_ref[:FOX_WIDTH, :], preferred_element_type=jnp.float32)
          + jnp.dot(do_ref[...], wo_ref[FOX_WIDTH:, :], preferred_element_type=jnp.float32))
    h = _rms(x1, gf_ref[...]).astype(jnp.bfloat16)
    g = jnp.dot(h, wg_ref[...], preferred_element_type=jnp.float32)
    u = jnp.dot(h, wu_ref[...], preferred_element_type=jnp.float32)
    a = (g * jax.nn.sigmoid(g) * u).astype(jnp.bfloat16)
    x2 = x1 + jnp.dot(a, wd_ref[...], preferred_element_type=jnp.float32)
    o_ref[...] = _rms(x2, gl_ref[...])


def _out_ffn(x, fo, do, wo, gf, wg, wu, wd, gl):
    s = x.shape[0]
    const = lambda shape: pl.BlockSpec(shape, lambda i: (0, 0), pipeline_mode=pl.Buffered(1))
    return pl.pallas_call(
        _out_ffn_kernel,
        grid=(s // TM_FFN,),
        in_specs=[
            pl.BlockSpec((TM_FFN, D_MODEL), lambda i: (i, 0)),
            pl.BlockSpec((TM_FFN, FOX_WIDTH), lambda i: (i, 0)),
            pl.BlockSpec((TM_FFN, DIFF_WIDTH), lambda i: (i, 0)),
            const(wo.shape), const(gf.shape), const(wg.shape), const(wu.shape),
            const(wd.shape), const(gl.shape),
        ],
        out_specs=pl.BlockSpec((TM_FFN, D_MODEL), lambda i: (i, 0)),
        out_shape=jax.ShapeDtypeStruct((s, D_MODEL), jnp.float32),
        compiler_params=pltpu.CompilerParams(
            dimension_semantics=("arbitrary",), vmem_limit_bytes=VMEM_LIMIT),
        name="out_ffn",
    )(x, fo, do, wo, gf, wg, wu, wd, gl)


def kernel(x, mix_norm_g, w_in, b_forget, lambda_q1, lambda_k1, lambda_q2, lambda_k2,
           diff_norm_g, w_out, ffn_norm_g, w_gate, w_up, w_down, final_norm_g):
    b, s, d = x.shape
    assert b == 1 and d == D_MODEL and w_in.shape[0] == 1
    assert s % TQ == 0 and s % TM_IN == 0 and s % TM_FFN == 0
    assert TM_IN == TK and TQ == TK and TILES_PER_STEP == 4
    assert s >= max(FIRST_TILE_CHOICES) * TK and s % (Q_TILES_PER_STEP * TQ) == 0
    assert FIRST_TILE_CHOICES == tuple(range(FIRST_TILE_CHOICES[0], FIRST_TILE_CHOICES[-1] + 1))
    assert s < 2 ** 14 + 1
    bf16 = jnp.bfloat16
    x2d = x.reshape(s, d)

    w = w_in[0]
    col = np.cumsum((0, FOX_WIDTH, FOX_WIDTH, FOX_WIDTH, FOX_HEADS, DIFF_WIDTH, DIFF_WIDTH,
                     DIFF_WIDTH))
    fq, fk, fv, gate, dq, dk, dv = (w[:, col[n]:col[n + 1]] for n in range(7))
    order = jnp.argsort(b_forget[0])
    by_head = lambda t: t.reshape(d, FOX_HEADS, HEAD_DIM)[:, order, :].reshape(d, FOX_WIDTH)
    weights = [t.astype(bf16) for t in (
        by_head(fk), dk, by_head(fq), dq, by_head(fv), dv,
        jnp.pad(gate[:, order], ((0, 0), (0, GATE_PAD - FOX_HEADS))))]
    b_pad = jnp.pad(b_forget[0][order].astype(jnp.float32),
                    (0, GATE_PAD - FOX_HEADS)).reshape(1, GATE_PAD)
    w_o = w_out[0]
    w_o = jnp.concatenate(
        [w_o[:FOX_WIDTH].reshape(FOX_HEADS, HEAD_DIM, d)[order].reshape(FOX_WIDTH, d),
         w_o[FOX_WIDTH:]], axis=0).astype(bf16)
    place = jnp.asarray(_fox_aug_placement(), bf16)
    hsum = jnp.asarray(np.arange(2 * FOX_WIDTH)[:, None] // HEAD_DIM == np.arange(LANES)[None, :],
                       bf16)

    q, fox_k, fox_v, diff_k, diff_v, cend, kn2 = _in_proj(
        x2d, mix_norm_g[0].reshape(1, d), weights, b_pad, place, hsum)

    fox_out = _attention(q, fox_k, fox_v, kn2, (cend,), fox=True)
    lam_args = tuple(a[0].astype(jnp.float32).reshape(1, HEAD_DIM)
                     for a in (lambda_q1, lambda_k1, lambda_q2, lambda_k2))
    diff_out = _attention(
        q, diff_k, diff_v, kn2,
        lam_args + (diff_norm_g[0].astype(jnp.float32).reshape(1, DIFF_V_DIM),), fox=False)

    out = _out_ffn(
        x2d, fox_out, diff_out, w_o, ffn_norm_g[0].reshape(1, d),
        w_gate[0].astype(bf16), w_up[0].astype(bf16), w_down[0].astype(bf16),
        final_norm_g.reshape(1, d))
    return out.reshape(b, s, d)
```

```python
import functools
import math

import numpy as np
import jax
import jax.numpy as jnp
from jax import lax
from jax.experimental import pallas as pl
from jax.experimental.pallas import tpu as pltpu

D_MODEL = 1024
HEAD_DIM = 64
FOX_HEADS = 8
FOX_WIDTH = FOX_HEADS * HEAD_DIM
DIFF_HEADS = 4
DIFF_V_DIM = 2 * HEAD_DIM
DIFF_WIDTH = DIFF_HEADS * DIFF_V_DIM
EPS = 1e-6
NEG_INF = -1e30
LOG2E = math.log2(math.e)
Q_SCALE = LOG2E / math.sqrt(HEAD_DIM)
UNDERFLOW_LOG2 = 127.0
NORM_SLACK = 1.01
LAMBDA_INIT = 0.8 - 0.6 * math.exp(-0.3 * 0)

LANES = 128
SUBLANES = 8
GATE_PAD = LANES
VMEM_LIMIT = 56 * 1024 * 1024

TM_IN = 512
TQ = 512
TK = 512
TILES_PER_STEP = 4
FIRST_TILES_FOX = (3, 3, 4, 4)
FIRST_TILES_DIFF = (2, 4, 4, 4)
Q_TILES_PER_STEP = 4
TM_FFN = 512

_NT = (((1,), (1,)), ((), ()))


def _rms(x, g):
    return x * lax.rsqrt(jnp.mean(x * x, axis=-1, keepdims=True) + EPS) * g


def _split3(x):
    hi = x.astype(jnp.bfloat16)
    r1 = x - hi.astype(jnp.float32)
    mid = r1.astype(jnp.bfloat16)
    lo = (r1 - mid.astype(jnp.float32)).astype(jnp.bfloat16)
    return hi, mid, lo


def _reduce_rows(x, op):
    while x.shape[0] > 8 and x.shape[0] % 2 == 0:
        half = x.shape[0] // 2
        x = op(x[:half], x[half:])
    reduce = jnp.max if op is jnp.maximum else jnp.sum
    return reduce(x, axis=0, keepdims=True)


def _fox_aug_placement():
    p = np.zeros((LANES, FOX_WIDTH), np.float32)
    for part in range(3):
        for head in range(FOX_HEADS):
            lane = 3 * (head % 2) + part
            p[part * FOX_HEADS + head, (head // 2) * LANES + lane] = 1.0
    return p


def _in_proj_kernel(x_ref, g_ref, wfk_ref, wdk_ref, wfq_ref, wdq_ref, wfv_ref, wdv_ref,
                    wgate_ref, b_ref, place_ref, hsum_ref,
                    q_ref, fk_ref, fv_ref, dk_ref, dv_ref, cend_ref, kn2_ref,
                    tri_ref, carry_ref):
    step = pl.program_id(0)
    tm = x_ref.shape[0]
    f32, bf16 = jnp.float32, jnp.bfloat16

    @pl.when(step == 0)
    def _():
        r = lax.broadcasted_iota(jnp.int32, (tm, tm), 0)
        c = lax.broadcasted_iota(jnp.int32, (tm, tm), 1)
        tri_ref[...] = jnp.where(r >= c, 1.0, 0.0).astype(bf16)
        carry_ref[...] = jnp.zeros_like(carry_ref)
        kn2_ref[...] = jnp.zeros_like(kn2_ref)

    lane = lax.broadcasted_iota(jnp.int32, (1, LANES), 1)
    lo_half = lane < HEAD_DIM
    head_lanes = lane < FOX_HEADS
    w5 = FOX_WIDTH

    def pack3(x):
        hi, mid, lo = (p.astype(f32) for p in _split3(jnp.where(head_lanes, x, 0.0)))
        return (hi + pltpu.roll(mid, FOX_HEADS, axis=1)
                + pltpu.roll(lo, 2 * FOX_HEADS, axis=1)).astype(bf16)

    h = _rms(x_ref[...], g_ref[...]).astype(bf16)

    z = jnp.dot(h, wgate_ref[...], preferred_element_type=f32) + b_ref[...]
    logf = LOG2E * (jnp.minimum(z, 0.0) - jnp.log1p(jnp.exp(-jnp.abs(z))))
    cs = jnp.dot(tri_ref[...], pack3(logf), preferred_element_type=f32)
    c = (cs + pltpu.roll(cs, LANES - FOX_HEADS, axis=1)
         + pltpu.roll(cs, LANES - 2 * FOX_HEADS, axis=1) + carry_ref[0:1, :])
    last = c[tm - 1:tm, :]
    carry_ref[...] = jnp.broadcast_to(last, carry_ref.shape)
    cend_ref[pl.ds(step, 1), :] = last
    caug = jnp.dot(pack3(-c), place_ref[...], preferred_element_type=f32).astype(bf16)

    project = lambda w_ref: jnp.dot(h, w_ref[...], preferred_element_type=f32)
    fk = project(wfk_ref).astype(bf16)
    dk = project(wdk_ref).astype(bf16)
    n2 = jnp.dot(jnp.concatenate([fk * fk, dk * dk], axis=1), hsum_ref[...],
                 preferred_element_type=f32)
    kn2_ref[...] = jnp.maximum(kn2_ref[...],
                               jnp.broadcast_to(jnp.max(n2, axis=0, keepdims=True), kn2_ref.shape))

    q_ref[:, :w5] = (project(wfq_ref) * Q_SCALE).astype(bf16)
    q_ref[:, w5:] = (project(wdq_ref) * Q_SCALE).astype(bf16)
    fv = project(wfv_ref).astype(bf16)
    dv_ref[...] = project(wdv_ref).astype(bf16)

    pos = step * tm + lax.broadcasted_iota(jnp.int32, (tm, LANES), 0)
    pos_lanes = jnp.where(lane < 3, pos & ~(LANES - 1),
                          jnp.where(lane < 6, pos & (LANES - 1), 0)).astype(f32).astype(bf16)
    for g in range(FOX_WIDTH // LANES):
        cols = slice(g * LANES, (g + 1) * LANES)
        key = slice(2 * g * LANES, (2 * g + 1) * LANES)
        bias = slice((2 * g + 1) * LANES, (2 * g + 2) * LANES)
        fk_ref[:, key] = fk[:, cols]
        fk_ref[:, bias] = caug[:, cols]
        dk_ref[:, key] = dk[:, cols]
        dk_ref[:, bias] = pos_lanes
        fv_ref[:, key] = jnp.where(lo_half, fv[:, cols], (lane == HEAD_DIM).astype(bf16))
        fv_ref[:, bias] = jnp.where(lo_half, (lane == 0).astype(bf16), fv[:, cols])


def _in_proj(x, g, weights, b, place, hsum):
    s = x.shape[0]
    n_tiles = s // TM_IN
    const = lambda a: pl.BlockSpec(a.shape, lambda i: (0, 0))
    row_block = lambda width: pl.BlockSpec((TM_IN, width), lambda i: (i, 0))
    bf16 = jnp.bfloat16
    return pl.pallas_call(
        _in_proj_kernel,
        grid=(n_tiles,),
        in_specs=([row_block(D_MODEL), const(g)] + [const(w) for w in weights]
                  + [const(b), const(place), const(hsum)]),
        out_specs=[
            row_block(2 * FOX_WIDTH),
            row_block(2 * FOX_WIDTH),
            row_block(2 * FOX_WIDTH),
            row_block(2 * DIFF_WIDTH),
            row_block(DIFF_WIDTH),
            pl.BlockSpec((n_tiles, LANES), lambda i: (0, 0)),
            pl.BlockSpec((SUBLANES, LANES), lambda i: (0, 0)),
        ],
        out_shape=[
            jax.ShapeDtypeStruct((s, 2 * FOX_WIDTH), bf16),
            jax.ShapeDtypeStruct((s, 2 * FOX_WIDTH), bf16),
            jax.ShapeDtypeStruct((s, 2 * FOX_WIDTH), bf16),
            jax.ShapeDtypeStruct((s, 2 * DIFF_WIDTH), bf16),
            jax.ShapeDtypeStruct((s, DIFF_WIDTH), bf16),
            jax.ShapeDtypeStruct((n_tiles, LANES), jnp.float32),
            jax.ShapeDtypeStruct((SUBLANES, LANES), jnp.float32),
        ],
        scratch_shapes=[
            pltpu.VMEM((TM_IN, TM_IN), bf16),
            pltpu.VMEM((SUBLANES, LANES), jnp.float32),
        ],
        compiler_params=pltpu.CompilerParams(
            dimension_semantics=("arbitrary",), vmem_limit_bytes=VMEM_LIMIT),
        name="in_proj",
    )(x, g, *weights, b, place, hsum)


def _attn_kernel(*refs, fox):
    def one_tile(t, carry):
        rows = pl.ds(pl.multiple_of(t * TQ, TQ), TQ)
        _attn_query_tile(pl.program_id(1) * Q_TILES_PER_STEP + t, rows, refs, fox)
        return carry

    lax.fori_loop(0, Q_TILES_PER_STEP, one_tile, 0)


def _attn_query_tile(i, q_rows, refs, fox):
    if fox:
        q_ref, kaug_ref, vaug_ref, kn2_ref, cend_ref, o_ref, m_ref, acc_ref = refs
    else:
        (q_ref, kaug_ref, v_ref, kn2_ref, lq1_ref, lk1_ref, lq2_ref, lk2_ref, gn_ref,
         o_ref, m_ref, acc_ref, l_ref) = refs
    grp = pl.program_id(0)
    tq, tk = TQ, TK
    n_tiles = kaug_ref.shape[0] // tk
    f32, bf16 = jnp.float32, jnp.bfloat16

    lane = lax.broadcasted_iota(jnp.int32, (1, LANES), 1)
    lo_half = lane < HEAD_DIM

    if not fox:
        slope = LOG2E * jnp.exp2(-2.0 * jnp.full((1, 1), grp + 1, jnp.int32).astype(f32))
        sl = [p.astype(f32) for p in _split3(slope)]
        inv_slope = 1.0 / slope

    def half_norms_sq(x):
        l_in = lax.broadcasted_iota(jnp.int32, (LANES, LANES), 0)
        l_out = lax.broadcasted_iota(jnp.int32, (LANES, LANES), 1)
        half_sum = ((l_in // HEAD_DIM) == l_out).astype(bf16)
        return jnp.dot(x * x, half_sum, preferred_element_type=f32)

    q = q_ref[q_rows, :]
    zero = jnp.zeros_like(q)
    if fox:
        q_bias = [jnp.broadcast_to(((lane >= 3 * h) & (lane < 3 * h + 3)).astype(bf16), q.shape)
                  for h in range(2)]
    else:
        parts = jnp.where((lane == 0) | (lane == 3), sl[0],
                          jnp.where((lane == 1) | (lane == 4), sl[1],
                                    jnp.where((lane == 2) | (lane == 5), sl[2], 0.0)))
        q_bias = [jnp.broadcast_to(parts.astype(bf16), q.shape)] * 2
    q2 = jnp.concatenate(
        [jnp.concatenate([jnp.where(lo_half, q, zero), q_bias[0]], axis=1),
         jnp.concatenate([jnp.where(lo_half, zero, q), q_bias[1]], axis=1)], axis=0)

    m_ref[...] = jnp.full_like(m_ref, NEG_INF)
    acc_ref[...] = jnp.zeros_like(acc_ref)
    if not fox:
        l_ref[...] = jnp.zeros_like(l_ref)

    def tile(j, n_sub, masked):
        width = n_sub * tk
        rows = pl.ds(pl.multiple_of(j * tk, tk), width)
        s = lax.dot_general(q2, kaug_ref[rows, :], _NT, preferred_element_type=f32)
        if masked == "diagonal_last":
            r = lax.broadcasted_iota(jnp.int32, (2 * tq, tk), 0) & (tq - 1)
            c = lax.broadcasted_iota(jnp.int32, (2 * tq, tk), 1)
            diag = jnp.where(c <= r, s[:, width - tk:], NEG_INF)
            s = diag if n_sub == 1 else jnp.concatenate([s[:, :width - tk], diag], axis=1)
        elif masked == "by_position":
            r = lax.broadcasted_iota(jnp.int32, (2 * tq, width), 0) & (tq - 1)
            c = lax.broadcasted_iota(jnp.int32, (2 * tq, width), 1)
            s = jnp.where(c + (j - i) * tk <= r, s, NEG_INF)
        m_prev = m_ref[...]
        m_next = jnp.maximum(m_prev, jnp.max(s, axis=1, keepdims=True))
        alpha = jnp.exp2(m_prev - m_next)
        p = jnp.exp2(s - jnp.tile(m_next, (1, width // LANES)))
        pb = p.astype(bf16)
        if fox:
            pv = jnp.concatenate(
                [jnp.dot(pb[:tq], vaug_ref[rows, :LANES], preferred_element_type=f32),
                 jnp.dot(pb[tq:], vaug_ref[rows, LANES:], preferred_element_type=f32)], axis=0)
        else:
            v = v_ref[rows, :]
            pv = jnp.concatenate([jnp.dot(pb[:tq], v, preferred_element_type=f32),
                                  jnp.dot(pb[tq:], v, preferred_element_type=f32)], axis=0)
            part = p[:, :LANES]
            for blk in range(1, width // LANES):
                part = part + p[:, blk * LANES:(blk + 1) * LANES]
            l_ref[...] = alpha * l_ref[...] + part
        acc_ref[...] = acc_ref[...] * alpha + pv
        m_ref[...] = m_next

    def first_step(first, masked):
        top = jnp.maximum(i - (first - 1), 0)
        qn = half_norms_sq(q)
        first_norm_lane = (0 if fox else FOX_HEADS) + 2 * grp
        kmax = [jnp.sqrt(jnp.sum(jnp.where(lane == first_norm_lane + h, kn2_ref[0:1, :], 0.0),
                                 axis=1, keepdims=True)) * NORM_SLACK for h in range(2)]
        bound = [jnp.sqrt(qn[:, h:h + 1]) * kmax[h] for h in range(2)]
        tile(top, first, masked)
        m_col = [m_ref[h * tq:(h + 1) * tq, 0:1] for h in range(2)]
        if fox:
            head_lane = lax.broadcasted_iota(jnp.int32, cend_ref.shape, 1) - 2 * grp
            cend = [jnp.sum(jnp.where(head_lane == h, cend_ref[...], 0.0), axis=1, keepdims=True)
                    for h in range(2)]
            below_top = lax.broadcasted_iota(jnp.int32, (n_tiles, 1), 0) < top
            dead = below_top
            for h in range(2):
                gap = _reduce_rows(bound[h] - m_col[h], jnp.maximum)
                dead = dead & (-cend[h] < -UNDERFLOW_LOG2 - gap)
            return _reduce_rows(dead.astype(f32), jnp.add).astype(jnp.int32)[0, 0], top
        gap = _reduce_rows(jnp.maximum(bound[0] - m_col[0], bound[1] - m_col[1]), jnp.maximum)
        x = (-UNDERFLOW_LOG2 - gap) * inv_slope
        j_lo = jnp.clip(jnp.floor(x * (1.0 / tk)), 0.0, 1.0 * n_tiles)
        return jnp.minimum(j_lo.astype(jnp.int32)[0, 0], top), top

    per_group = FIRST_TILES_FOX if fox else FIRST_TILES_DIFF
    sizes = sorted(set(per_group))
    which = sum(jnp.where(grp == g, sizes.index(size), 0) for g, size in enumerate(per_group))
    branches = []
    for size in sizes:
        branches += [functools.partial(first_step, size, "diagonal_last"),
                     functools.partial(first_step, size, "by_position")]
    early = i < sum(jnp.where(which == n, size - 1, 0) for n, size in enumerate(sizes))
    j_lo, top = lax.switch(2 * which + early.astype(jnp.int32), branches)

    n_live = top - j_lo
    one = n_live & 1
    two = n_live & 2

    @pl.when(one == 1)
    def _():
        tile(top - 1, 1, masked=False)

    @pl.when(two == 2)
    def _():
        tile(top - one - 2, 2, masked=False)

    def body(step, carry):
        tile(top - one - two - TILES_PER_STEP * (step + 1), TILES_PER_STEP, masked=False)
        return carry

    lax.fori_loop(0, n_live // TILES_PER_STEP, body, 0)

    acc = [acc_ref[h * tq:(h + 1) * tq, :] for h in range(2)]
    if fox:
        out = jnp.where(lo_half, acc[0] / acc[0][:, HEAD_DIM:HEAD_DIM + 1],
                        acc[1] / acc[1][:, 0:1])
    else:
        lam = (jnp.exp(jnp.sum(lq1_ref[...] * lk1_ref[...], axis=1, keepdims=True))
               - jnp.exp(jnp.sum(lq2_ref[...] * lk2_ref[...], axis=1, keepdims=True))
               + LAMBDA_INIT)
        l = jnp.sum(l_ref[...], axis=1, keepdims=True)
        a = acc[0] / l[:tq] - lam * (acc[1] / l[tq:])
        out = _rms(a, gn_ref[...]) * (1.0 - LAMBDA_INIT)
    o_ref[q_rows, :] = out.astype(o_ref.dtype)


def _attention(q, kaug, v, kn2, extra, *, fox):
    s = q.shape[0]
    groups = FOX_HEADS // 2 if fox else DIFF_HEADS
    q_col0 = 0 if fox else FOX_WIDTH // LANES
    resident = lambda width: pl.BlockSpec((s, width), lambda g, i: (0, g),
                                          pipeline_mode=pl.Buffered(1))
    small = lambda a: pl.BlockSpec(a.shape, lambda g, i: (0, 0))
    in_specs = [
        pl.BlockSpec((Q_TILES_PER_STEP * TQ, LANES), lambda g, i: (i, q_col0 + g)),
        resident(2 * LANES),
        resident(2 * LANES if fox else LANES),
        small(kn2),
    ] + [small(a) for a in extra]
    scratch = [
        pltpu.VMEM((2 * TQ, LANES), jnp.float32),
        pltpu.VMEM((2 * TQ, LANES), jnp.float32),
    ]
    if not fox:
        scratch.append(pltpu.VMEM((2 * TQ, LANES), jnp.float32))
    return pl.pallas_call(
        functools.partial(_attn_kernel, fox=fox),
        grid=(groups, s // (Q_TILES_PER_STEP * TQ)),
        in_specs=in_specs,
        out_specs=pl.BlockSpec((Q_TILES_PER_STEP * TQ, LANES), lambda g, i: (i, g)),
        out_shape=jax.ShapeDtypeStruct((s, groups * LANES), jnp.bfloat16),
        scratch_shapes=scratch,
        compiler_params=pltpu.CompilerParams(
            dimension_semantics=("arbitrary", "arbitrary"), vmem_limit_bytes=VMEM_LIMIT),
        name="fox_attn" if fox else "diff_attn",
    )(q, kaug, v, kn2, *extra)


def _out_ffn_kernel(x_ref, fo_ref, do_ref, wo_ref, gf_ref, wg_ref, wu_ref, wd_ref,
                    gl_ref, o_ref):
    x1 = (x_ref[...]
          + jnp.dot(fo_ref[...], wo_ref[:FOX_WIDTH, :], preferred_element_type=jnp.float32)
          + jnp.dot(do_ref[...], wo_ref[FOX_WIDTH:, :], preferred_element_type=jnp.float32))
    h = _rms(x1, gf_ref[...]).astype(jnp.bfloat16)
    g = jnp.dot(h, wg_ref[...], preferred_element_type=jnp.float32)
    u = jnp.dot(h, wu_ref[...], preferred_element_type=jnp.float32)
    a = (g * jax.nn.sigmoid(g) * u).astype(jnp.bfloat16)
    x2 = x1 + jnp.dot(a, wd_ref[...], preferred_element_type=jnp.float32)
    o_ref[...] = _rms(x2, gl_ref[...])


def _out_ffn(x, fo, do, wo, gf, wg, wu, wd, gl):
    s = x.shape[0]
    const = lambda shape: pl.BlockSpec(shape, lambda i: (0, 0), pipeline_mode=pl.Buffered(1))
    return pl.pallas_call(
        _out_ffn_kernel,
        grid=(s // TM_FFN,),
        in_specs=[
            pl.BlockSpec((TM_FFN, D_MODEL), lambda i: (i, 0)),
            pl.BlockSpec((TM_FFN, FOX_WIDTH), lambda i: (i, 0)),
            pl.BlockSpec((TM_FFN, DIFF_WIDTH), lambda i: (i, 0)),
            const(wo.shape), const(gf.shape), const(wg.shape), const(wu.shape),
            const(wd.shape), const(gl.shape),
        ],
        out_specs=pl.BlockSpec((TM_FFN, D_MODEL), lambda i: (i, 0)),
        out_shape=jax.ShapeDtypeStruct((s, D_MODEL), jnp.float32),
        compiler_params=pltpu.CompilerParams(
            dimension_semantics=("arbitrary",), vmem_limit_bytes=VMEM_LIMIT),
        name="out_ffn",
    )(x, fo, do, wo, gf, wg, wu, wd, gl)


def kernel(x, mix_norm_g, w_in, b_forget, lambda_q1, lambda_k1, lambda_q2, lambda_k2,
           diff_norm_g, w_out, ffn_norm_g, w_gate, w_up, w_down, final_norm_g):
    b, s, d = x.shape
    assert b == 1 and d == D_MODEL and w_in.shape[0] == 1
    assert s % TQ == 0 and s % TM_IN == 0 and s % TM_FFN == 0
    assert TM_IN == TK and TQ == TK and TILES_PER_STEP == 4
    assert s >= max(FIRST_TILES_FOX + FIRST_TILES_DIFF) * TK and s % (Q_TILES_PER_STEP * TQ) == 0
    assert s < 2 ** 14 + 1
    bf16 = jnp.bfloat16
    x2d = x.reshape(s, d)

    w = w_in[0]
    col = np.cumsum((0, FOX_WIDTH, FOX_WIDTH, FOX_WIDTH, FOX_HEADS, DIFF_WIDTH, DIFF_WIDTH,
                     DIFF_WIDTH))
    fq, fk, fv, gate, dq, dk, dv = (w[:, col[n]:col[n + 1]] for n in range(7))
    order = jnp.argsort(b_forget[0])
    by_head = lambda t: t.reshape(d, FOX_HEADS, HEAD_DIM)[:, order, :].reshape(d, FOX_WIDTH)
    weights = [t.astype(bf16) for t in (
        by_head(fk), dk, by_head(fq), dq, by_head(fv), dv,
        jnp.pad(gate[:, order], ((0, 0), (0, GATE_PAD - FOX_HEADS))))]
    b_pad = jnp.pad(b_forget[0][order].astype(jnp.float32),
                    (0, GATE_PAD - FOX_HEADS)).reshape(1, GATE_PAD)
    w_o = w_out[0]
    w_o = jnp.concatenate(
        [w_o[:FOX_WIDTH].reshape(FOX_HEADS, HEAD_DIM, d)[order].reshape(FOX_WIDTH, d),
         w_o[FOX_WIDTH:]], axis=0).astype(bf16)
    place = jnp.asarray(_fox_aug_placement(), bf16)
    hsum = jnp.asarray(np.arange(2 * FOX_WIDTH)[:, None] // HEAD_DIM == np.arange(LANES)[None, :],
                       bf16)

    q, fox_k, fox_v, diff_k, diff_v, cend, kn2 = _in_proj(
        x2d, mix_norm_g[0].reshape(1, d), weights, b_pad, place, hsum)

    fox_out = _attention(q, fox_k, fox_v, kn2, (cend,), fox=True)
    lam_args = tuple(a[0].astype(jnp.float32).reshape(1, HEAD_DIM)
                     for a in (lambda_q1, lambda_k1, lambda_q2, lambda_k2))
    diff_out = _attention(
        q, diff_k, diff_v, kn2,
        lam_args + (diff_norm_g[0].astype(jnp.float32).reshape(1, DIFF_V_DIM),), fox=False)

    out = _out_ffn(
        x2d, fox_out, diff_out, w_o, ffn_norm_g[0].reshape(1, d),
        w_gate[0].astype(bf16), w_up[0].astype(bf16), w_down[0].astype(bf16),
        final_norm_g.reshape(1, d))
    return out.reshape(b, s, d)
```

```python
import functools
import math

import numpy as np
import jax
import jax.numpy as jnp
from jax import lax
from jax.experimental import pallas as pl
from jax.experimental.pallas import tpu as pltpu

D_MODEL = 1024
HEAD_DIM = 64
FOX_HEADS = 8
FOX_WIDTH = FOX_HEADS * HEAD_DIM
DIFF_HEADS = 4
DIFF_V_DIM = 2 * HEAD_DIM
DIFF_WIDTH = DIFF_HEADS * DIFF_V_DIM
EPS = 1e-6
NEG_INF = -1e30
LOG2E = math.log2(math.e)
Q_SCALE = LOG2E / math.sqrt(HEAD_DIM)
UNDERFLOW_LOG2 = 127.0
NORM_SLACK = 1.01
LAMBDA_INIT = 0.8 - 0.6 * math.exp(-0.3 * 0)

LANES = 128
SUBLANES = 8
GATE_PAD = LANES
VMEM_LIMIT = 56 * 1024 * 1024

TM_IN = 512
TQ = 512
TK = 512
TILES_PER_STEP = 4
FIRST_TILES_FOX = (3, 3, 4, 4)
FIRST_TILES_DIFF = (2, 4, 4, 4)
Q_TILES_PER_STEP = 4
TM_FFN = 512

_NT = (((1,), (1,)), ((), ()))


def _rms(x, g):
    return x * lax.rsqrt(jnp.mean(x * x, axis=-1, keepdims=True) + EPS) * g


def _split3(x):
    hi = x.astype(jnp.bfloat16)
    r1 = x - hi.astype(jnp.float32)
    mid = r1.astype(jnp.bfloat16)
    lo = (r1 - mid.astype(jnp.float32)).astype(jnp.bfloat16)
    return hi, mid, lo


def _reduce_rows(x, op):
    while x.shape[0] > 8 and x.shape[0] % 2 == 0:
        half = x.shape[0] // 2
        x = op(x[:half], x[half:])
    reduce = jnp.max if op is jnp.maximum else jnp.sum
    return reduce(x, axis=0, keepdims=True)


def _fox_aug_placement():
    p = np.zeros((LANES, FOX_WIDTH), np.float32)
    for part in range(3):
        for head in range(FOX_HEADS):
            lane = 3 * (head % 2) + part
            p[part * FOX_HEADS + head, (head // 2) * LANES + lane] = 1.0
    return p


def _in_proj_kernel(x_ref, g_ref, wfk_ref, wdk_ref, wfq_ref, wdq_ref, wfv_ref, wdv_ref,
                    wgate_ref, b_ref, place_ref, hsum_ref,
                    q_ref, fk_ref, fv_ref, dk_ref, dv_ref, cend_ref, kn2_ref,
                    tri_ref, carry_ref):
    step = pl.program_id(0)
    tm = x_ref.shape[0]
    f32, bf16 = jnp.float32, jnp.bfloat16

    @pl.when(step == 0)
    def _():
        r = lax.broadcasted_iota(jnp.int32, (tm, tm), 0)
        c = lax.broadcasted_iota(jnp.int32, (tm, tm), 1)
        tri_ref[...] = jnp.where(r >= c, 1.0, 0.0).astype(bf16)
        carry_ref[...] = jnp.zeros_like(carry_ref)
        kn2_ref[...] = jnp.zeros_like(kn2_ref)

    lane = lax.broadcasted_iota(jnp.int32, (1, LANES), 1)
    lo_half = lane < HEAD_DIM
    head_lanes = lane < FOX_HEADS
    w5 = FOX_WIDTH

    def pack3(x):
        hi, mid, lo = (p.astype(f32) for p in _split3(jnp.where(head_lanes, x, 0.0)))
        return (hi + pltpu.roll(mid, FOX_HEADS, axis=1)
                + pltpu.roll(lo, 2 * FOX_HEADS, axis=1)).astype(bf16)

    h = _rms(x_ref[...], g_ref[...]).astype(bf16)

    z = jnp.dot(h, wgate_ref[...], preferred_element_type=f32) + b_ref[...]
    logf = LOG2E * (jnp.minimum(z, 0.0) - jnp.log1p(jnp.exp(-jnp.abs(z))))
    project = lambda w_ref: jnp.dot(h, w_ref[...], preferred_element_type=f32)
    fk = project(wfk_ref).astype(bf16)
    dk = project(wdk_ref).astype(bf16)
    cs = jnp.dot(tri_ref[...], pack3(logf), preferred_element_type=f32)
    c = (cs + pltpu.roll(cs, LANES - FOX_HEADS, axis=1)
         + pltpu.roll(cs, LANES - 2 * FOX_HEADS, axis=1) + carry_ref[0:1, :])
    last = c[tm - 1:tm, :]
    carry_ref[...] = jnp.broadcast_to(last, carry_ref.shape)
    cend_ref[pl.ds(step, 1), :] = last

    q_ref[:, :w5] = (project(wfq_ref) * Q_SCALE).astype(bf16)
    q_ref[:, w5:] = (project(wdq_ref) * Q_SCALE).astype(bf16)
    caug = jnp.dot(pack3(-c), place_ref[...], preferred_element_type=f32).astype(bf16)
    fv = project(wfv_ref).astype(bf16)
    n2 = jnp.dot(jnp.concatenate([fk * fk, dk * dk], axis=1), hsum_ref[...],
                 preferred_element_type=f32)
    kn2_ref[...] = jnp.maximum(kn2_ref[...],
                               jnp.broadcast_to(jnp.max(n2, axis=0, keepdims=True), kn2_ref.shape))
    dv_ref[...] = project(wdv_ref).astype(bf16)

    pos = step * tm + lax.broadcasted_iota(jnp.int32, (tm, LANES), 0)
    pos_lanes = jnp.where(lane < 3, pos & ~(LANES - 1),
                          jnp.where(lane < 6, pos & (LANES - 1), 0)).astype(f32).astype(bf16)
    for g in range(FOX_WIDTH // LANES):
        cols = slice(g * LANES, (g + 1) * LANES)
        key = slice(2 * g * LANES, (2 * g + 1) * LANES)
        bias = slice((2 * g + 1) * LANES, (2 * g + 2) * LANES)
        fk_ref[:, key] = fk[:, cols]
        fk_ref[:, bias] = caug[:, cols]
        dk_ref[:, key] = dk[:, cols]
        dk_ref[:, bias] = pos_lanes
        fv_ref[:, key] = jnp.where(lo_half, fv[:, cols], (lane == HEAD_DIM).astype(bf16))
        fv_ref[:, bias] = jnp.where(lo_half, (lane == 0).astype(bf16), fv[:, cols])


def _in_proj(x, g, weights, b, place, hsum):
    s = x.shape[0]
    n_tiles = s // TM_IN
    const = lambda a: pl.BlockSpec(a.shape, lambda i: (0, 0))
    row_block = lambda width: pl.BlockSpec((TM_IN, width), lambda i: (i, 0))
    bf16 = jnp.bfloat16
    return pl.pallas_call(
        _in_proj_kernel,
        grid=(n_tiles,),
        in_specs=([row_block(D_MODEL), const(g)] + [const(w) for w in weights]
                  + [const(b), const(place), const(hsum)]),
        out_specs=[
            row_block(2 * FOX_WIDTH),
            row_block(2 * FOX_WIDTH),
            row_block(2 * FOX_WIDTH),
            row_block(2 * DIFF_WIDTH),
            row_block(DIFF_WIDTH),
            pl.BlockSpec((n_tiles, LANES), lambda i: (0, 0)),
            pl.BlockSpec((SUBLANES, LANES), lambda i: (0, 0)),
        ],
        out_shape=[
            jax.ShapeDtypeStruct((s, 2 * FOX_WIDTH), bf16),
            jax.ShapeDtypeStruct((s, 2 * FOX_WIDTH), bf16),
            jax.ShapeDtypeStruct((s, 2 * FOX_WIDTH), bf16),
            jax.ShapeDtypeStruct((s, 2 * DIFF_WIDTH), bf16),
            jax.ShapeDtypeStruct((s, DIFF_WIDTH), bf16),
            jax.ShapeDtypeStruct((n_tiles, LANES), jnp.float32),
            jax.ShapeDtypeStruct((SUBLANES, LANES), jnp.float32),
        ],
        scratch_shapes=[
            pltpu.VMEM((TM_IN, TM_IN), bf16),
            pltpu.VMEM((SUBLANES, LANES), jnp.float32),
        ],
        compiler_params=pltpu.CompilerParams(
            dimension_semantics=("arbitrary",), vmem_limit_bytes=VMEM_LIMIT),
        name="in_proj",
    )(x, g, *weights, b, place, hsum)


def _attn_kernel(q_ref, fk_ref, fv_ref, dk_ref, dv_ref, kn2_ref, cend_ref,
                 lq1_ref, lk1_ref, lq2_ref, lk2_ref, gn_ref, o_ref, m_ref, acc_ref, l_ref):
    group_refs = {
        True: (q_ref, fk_ref, fv_ref, kn2_ref, cend_ref, o_ref, m_ref, acc_ref),
        False: (q_ref, dk_ref, dv_ref, kn2_ref, lq1_ref, lk1_ref, lq2_ref, lk2_ref, gn_ref,
                o_ref, m_ref, acc_ref, l_ref),
    }
    is_fox = pl.program_id(0) < FOX_HEADS // 2
    for fox, refs in group_refs.items():
        @pl.when(is_fox == fox)
        def _(fox=fox, refs=refs):
            def one_tile(t, carry):
                rows = pl.ds(pl.multiple_of(t * TQ, TQ), TQ)
                _attn_query_tile(pl.program_id(1) * Q_TILES_PER_STEP + t, rows, refs, fox)
                return carry

            lax.fori_loop(0, Q_TILES_PER_STEP, one_tile, 0)


def _attn_query_tile(i, q_rows, refs, fox):
    if fox:
        q_ref, kaug_ref, vaug_ref, kn2_ref, cend_ref, o_ref, m_ref, acc_ref = refs
    else:
        (q_ref, kaug_ref, v_ref, kn2_ref, lq1_ref, lk1_ref, lq2_ref, lk2_ref, gn_ref,
         o_ref, m_ref, acc_ref, l_ref) = refs
    grp = pl.program_id(0) - (0 if fox else FOX_HEADS // 2)
    tq, tk = TQ, TK
    n_tiles = kaug_ref.shape[0] // tk
    f32, bf16 = jnp.float32, jnp.bfloat16

    lane = lax.broadcasted_iota(jnp.int32, (1, LANES), 1)
    lo_half = lane < HEAD_DIM

    if not fox:
        slope = LOG2E * jnp.exp2(-2.0 * jnp.full((1, 1), grp + 1, jnp.int32).astype(f32))
        sl = [p.astype(f32) for p in _split3(slope)]
        inv_slope = 1.0 / slope

    def half_norms_sq(x):
        l_in = lax.broadcasted_iota(jnp.int32, (LANES, LANES), 0)
        l_out = lax.broadcasted_iota(jnp.int32, (LANES, LANES), 1)
        half_sum = ((l_in // HEAD_DIM) == l_out).astype(bf16)
        return jnp.dot(x * x, half_sum, preferred_element_type=f32)

    q = q_ref[q_rows, :]
    zero = jnp.zeros_like(q)
    if fox:
        q_bias = [jnp.broadcast_to(((lane >= 3 * h) & (lane < 3 * h + 3)).astype(bf16), q.shape)
                  for h in range(2)]
    else:
        parts = jnp.where((lane == 0) | (lane == 3), sl[0],
                          jnp.where((lane == 1) | (lane == 4), sl[1],
                                    jnp.where((lane == 2) | (lane == 5), sl[2], 0.0)))
        q_bias = [jnp.broadcast_to(parts.astype(bf16), q.shape)] * 2
    q2 = jnp.concatenate(
        [jnp.concatenate([jnp.where(lo_half, q, zero), q_bias[0]], axis=1),
         jnp.concatenate([jnp.where(lo_half, zero, q), q_bias[1]], axis=1)], axis=0)

    m_ref[...] = jnp.full_like(m_ref, NEG_INF)
    acc_ref[...] = jnp.zeros_like(acc_ref)
    if not fox:
        l_ref[...] = jnp.zeros_like(l_ref)

    def tile(j, n_sub, masked):
        width = n_sub * tk
        rows = pl.ds(pl.multiple_of(j * tk, tk), width)
        s = lax.dot_general(q2, kaug_ref[rows, :], _NT, preferred_element_type=f32)
        if masked == "diagonal_last":
            r = lax.broadcasted_iota(jnp.int32, (2 * tq, tk), 0) & (tq - 1)
            c = lax.broadcasted_iota(jnp.int32, (2 * tq, tk), 1)
            diag = jnp.where(c <= r, s[:, width - tk:], NEG_INF)
            s = diag if n_sub == 1 else jnp.concatenate([s[:, :width - tk], diag], axis=1)
        elif masked == "by_position":
            r = lax.broadcasted_iota(jnp.int32, (2 * tq, width), 0) & (tq - 1)
            c = lax.broadcasted_iota(jnp.int32, (2 * tq, width), 1)
            s = jnp.where(c + (j - i) * tk <= r, s, NEG_INF)
        m_prev = m_ref[...]
        m_next = jnp.maximum(m_prev, jnp.max(s, axis=1, keepdims=True))
        alpha = jnp.exp2(m_prev - m_next)
        p = jnp.exp2(s - jnp.tile(m_next, (1, width // LANES)))
        pb = p.astype(bf16)
        if fox:
            pv = jnp.concatenate(
                [jnp.dot(pb[:tq], vaug_ref[rows, :LANES], preferred_element_type=f32),
                 jnp.dot(pb[tq:], vaug_ref[rows, LANES:], preferred_element_type=f32)], axis=0)
        else:
            v = v_ref[rows, :]
            pv = jnp.concatenate([jnp.dot(pb[:tq], v, preferred_element_type=f32),
                                  jnp.dot(pb[tq:], v, preferred_element_type=f32)], axis=0)
            part = p[:, :LANES]
            for blk in range(1, width // LANES):
                part = part + p[:, blk * LANES:(blk + 1) * LANES]
            l_ref[...] = alpha * l_ref[...] + part
        acc_ref[...] = acc_ref[...] * alpha + pv
        m_ref[...] = m_next

    def first_step(first, masked):
        top = jnp.maximum(i - (first - 1), 0)
        qn = half_norms_sq(q)
        first_norm_lane = (0 if fox else FOX_HEADS) + 2 * grp
        kmax = [jnp.sqrt(jnp.sum(jnp.where(lane == first_norm_lane + h, kn2_ref[0:1, :], 0.0),
                                 axis=1, keepdims=True)) * NORM_SLACK for h in range(2)]
        bound = [jnp.sqrt(qn[:, h:h + 1]) * kmax[h] for h in range(2)]
        tile(top, first, masked)
        m_col = [m_ref[h * tq:(h + 1) * tq, 0:1] for h in range(2)]
        if fox:
            head_lane = lax.broadcasted_iota(jnp.int32, cend_ref.shape, 1) - 2 * grp
            cend = [jnp.sum(jnp.where(head_lane == h, cend_ref[...], 0.0), axis=1, keepdims=True)
                    for h in range(2)]
            below_top = lax.broadcasted_iota(jnp.int32, (n_tiles, 1), 0) < top
            dead = below_top
            for h in range(2):
                gap = _reduce_rows(bound[h] - m_col[h], jnp.maximum)
                dead = dead & (-cend[h] < -UNDERFLOW_LOG2 - gap)
            return _reduce_rows(dead.astype(f32), jnp.add).astype(jnp.int32)[0, 0], top
        gap = _reduce_rows(jnp.maximum(bound[0] - m_col[0], bound[1] - m_col[1]), jnp.maximum)
        x = (-UNDERFLOW_LOG2 - gap) * inv_slope
        j_lo = jnp.clip(jnp.floor(x * (1.0 / tk)), 0.0, 1.0 * n_tiles)
        return jnp.minimum(j_lo.astype(jnp.int32)[0, 0], top), top

    per_group = FIRST_TILES_FOX if fox else FIRST_TILES_DIFF
    sizes = sorted(set(per_group))
    which = sum(jnp.where(grp == g, sizes.index(size), 0) for g, size in enumerate(per_group))
    branches = []
    for size in sizes:
        branches += [functools.partial(first_step, size, "diagonal_last"),
                     functools.partial(first_step, size, "by_position")]
    early = i < sum(jnp.where(which == n, size - 1, 0) for n, size in enumerate(sizes))
    j_lo, top = lax.switch(2 * which + early.astype(jnp.int32), branches)

    n_live = top - j_lo
    one = n_live & 1
    two = n_live & 2

    @pl.when(one == 1)
    def _():
        tile(top - 1, 1, masked=False)

    @pl.when(two == 2)
    def _():
        tile(top - one - 2, 2, masked=False)

    def body(step, carry):
        tile(top - one - two - TILES_PER_STEP * (step + 1), TILES_PER_STEP, masked=False)
        return carry

    lax.fori_loop(0, n_live // TILES_PER_STEP, body, 0)

    acc = [acc_ref[h * tq:(h + 1) * tq, :] for h in range(2)]
    if fox:
        out = jnp.where(lo_half, acc[0] / acc[0][:, HEAD_DIM:HEAD_DIM + 1],
                        acc[1] / acc[1][:, 0:1])
    else:
        lam = (jnp.exp(jnp.sum(lq1_ref[...] * lk1_ref[...], axis=1, keepdims=True))
               - jnp.exp(jnp.sum(lq2_ref[...] * lk2_ref[...], axis=1, keepdims=True))
               + LAMBDA_INIT)
        l = jnp.sum(l_ref[...], axis=1, keepdims=True)
        a = acc[0] / l[:tq] - lam * (acc[1] / l[tq:])
        out = _rms(a, gn_ref[...]) * (1.0 - LAMBDA_INIT)
    o_ref[q_rows, :] = out.astype(o_ref.dtype)


def _attention(q, fox_k, fox_v, diff_k, diff_v, kn2, cend, diff_params):
    s = q.shape[0]
    n_fox = FOX_HEADS // 2
    groups = n_fox + DIFF_HEADS
    fox_resident = lambda width: pl.BlockSpec(
        (s, width), lambda g, i: (0, jnp.minimum(g, n_fox - 1)), pipeline_mode=pl.Buffered(1))
    diff_resident = lambda width: pl.BlockSpec(
        (s, width), lambda g, i: (0, jnp.maximum(g - n_fox, 0)), pipeline_mode=pl.Buffered(1))
    small = lambda a: pl.BlockSpec(a.shape, lambda g, i: (0, 0))
    q_block = pl.BlockSpec((Q_TILES_PER_STEP * TQ, LANES), lambda g, i: (i, g))
    stats = pltpu.VMEM((2 * TQ, LANES), jnp.float32)
    return pl.pallas_call(
        _attn_kernel,
        grid=(groups, s // (Q_TILES_PER_STEP * TQ)),
        in_specs=[q_block,
                  fox_resident(2 * LANES), fox_resident(2 * LANES),
                  diff_resident(2 * LANES), diff_resident(LANES),
                  small(kn2), small(cend)] + [small(a) for a in diff_params],
        out_specs=q_block,
        out_shape=jax.ShapeDtypeStruct((s, groups * LANES), jnp.bfloat16),
        scratch_shapes=[stats, stats, stats],
        compiler_params=pltpu.CompilerParams(
            dimension_semantics=("arbitrary", "arbitrary"), vmem_limit_bytes=VMEM_LIMIT),
        name="attention",
    )(q, fox_k, fox_v, diff_k, diff_v, kn2, cend, *diff_params)


def _out_ffn_kernel(x_ref, fo_ref, do_ref, wo_ref, gf_ref, wg_ref, wu_ref, wd_ref,
                    gl_ref, o_ref):
    x1 = (x_ref[...]
          + jnp.dot(fo_ref[...], wo_ref[:FOX_WIDTH, :], preferred_element_type=jnp.float32)
          + jnp.dot(do_ref[...], wo_ref[FOX_WIDTH:, :], preferred_element_type=jnp.float32))
    h = _rms(x1, gf_ref[...]).astype(jnp.bfloat16)
    g = jnp.dot(h, wg_ref[...], preferred_element_type=jnp.float32)
    u = jnp.dot(h, wu_ref[...], preferred_element_type=jnp.float32)
    a = (g * jax.nn.sigmoid(g) * u).astype(jnp.bfloat16)
    x2 = x1 + jnp.dot(a, wd_ref[...], preferred_element_type=jnp.float32)
    o_ref[...] = _rms(x2, gl_ref[...])


def _out_ffn(x, fo, do, wo, gf, wg, wu, wd, gl):
    s = x.shape[0]
    const = lambda shape: pl.BlockSpec(shape, lambda i: (0, 0), pipeline_mode=pl.Buffered(1))
    return pl.pallas_call(
        _out_ffn_kernel,
        grid=(s // TM_FFN,),
        in_specs=[
            pl.BlockSpec((TM_FFN, D_MODEL), lambda i: (i, 0)),
            pl.BlockSpec((TM_FFN, FOX_WIDTH), lambda i: (i, 0)),
            pl.BlockSpec((TM_FFN, DIFF_WIDTH), lambda i: (i, 1)),
            const(wo.shape), const(gf.shape), const(wg.shape), const(wu.shape),
            const(wd.shape), const(gl.shape),
        ],
        out_specs=pl.BlockSpec((TM_FFN, D_MODEL), lambda i: (i, 0)),
        out_shape=jax.ShapeDtypeStruct((s, D_MODEL), jnp.float32),
        compiler_params=pltpu.CompilerParams(
            dimension_semantics=("arbitrary",), vmem_limit_bytes=VMEM_LIMIT),
        name="out_ffn",
    )(x, fo, do, wo, gf, wg, wu, wd, gl)


def kernel(x, mix_norm_g, w_in, b_forget, lambda_q1, lambda_k1, lambda_q2, lambda_k2,
           diff_norm_g, w_out, ffn_norm_g, w_gate, w_up, w_down, final_norm_g):
    b, s, d = x.shape
    assert b == 1 and d == D_MODEL and w_in.shape[0] == 1
    assert s % TQ == 0 and s % TM_IN == 0 and s % TM_FFN == 0
    assert TM_IN == TK and TQ == TK and TILES_PER_STEP == 4
    assert s >= max(FIRST_TILES_FOX + FIRST_TILES_DIFF) * TK and s % (Q_TILES_PER_STEP * TQ) == 0
    assert s < 2 ** 14 + 1
    bf16 = jnp.bfloat16
    x2d = x.reshape(s, d)

    w = w_in[0]
    col = np.cumsum((0, FOX_WIDTH, FOX_WIDTH, FOX_WIDTH, FOX_HEADS, DIFF_WIDTH, DIFF_WIDTH,
                     DIFF_WIDTH))
    fq, fk, fv, gate, dq, dk, dv = (w[:, col[n]:col[n + 1]] for n in range(7))
    order = jnp.argsort(b_forget[0])
    by_head = lambda t: t.reshape(d, FOX_HEADS, HEAD_DIM)[:, order, :].reshape(d, FOX_WIDTH)
    weights = [t.astype(bf16) for t in (
        by_head(fk), dk, by_head(fq), dq, by_head(fv), dv,
        jnp.pad(gate[:, order], ((0, 0), (0, GATE_PAD - FOX_HEADS))))]
    b_pad = jnp.pad(b_forget[0][order].astype(jnp.float32),
                    (0, GATE_PAD - FOX_HEADS)).reshape(1, GATE_PAD)
    w_o = w_out[0]
    w_o = jnp.concatenate(
        [w_o[:FOX_WIDTH].reshape(FOX_HEADS, HEAD_DIM, d)[order].reshape(FOX_WIDTH, d),
         w_o[FOX_WIDTH:]], axis=0).astype(bf16)
    place = jnp.asarray(_fox_aug_placement(), bf16)
    hsum = jnp.asarray(np.arange(2 * FOX_WIDTH)[:, None] // HEAD_DIM == np.arange(LANES)[None, :],
                       bf16)

    q, fox_k, fox_v, diff_k, diff_v, cend, kn2 = _in_proj(
        x2d, mix_norm_g[0].reshape(1, d), weights, b_pad, place, hsum)

    lam_args = tuple(a[0].astype(jnp.float32).reshape(1, HEAD_DIM)
                     for a in (lambda_q1, lambda_k1, lambda_q2, lambda_k2))
    mixed = _attention(
        q, fox_k, fox_v, diff_k, diff_v, kn2, cend,
        lam_args + (diff_norm_g[0].astype(jnp.float32).reshape(1, DIFF_V_DIM),))

    out = _out_ffn(
        x2d, mixed, mixed, w_o, ffn_norm_g[0].reshape(1, d),
        w_gate[0].astype(bf16), w_up[0].astype(bf16), w_down[0].astype(bf16),
        final_norm_g.reshape(1, d))
    return out.reshape(b, s, d)
```

```python
import functools
import math

import numpy as np
import jax
import jax.numpy as jnp
from jax import lax
from jax.experimental import pallas as pl
from jax.experimental.pallas import tpu as pltpu

D_MODEL = 1024
HEAD_DIM = 64
FOX_HEADS = 8
FOX_WIDTH = FOX_HEADS * HEAD_DIM
DIFF_HEADS = 4
DIFF_V_DIM = 2 * HEAD_DIM
DIFF_WIDTH = DIFF_HEADS * DIFF_V_DIM
EPS = 1e-6
NEG_INF = -1e30
LOG2E = math.log2(math.e)
Q_SCALE = LOG2E / math.sqrt(HEAD_DIM)
UNDERFLOW_LOG2 = 127.0
NORM_SLACK = 1.01
LAMBDA_INIT = 0.8 - 0.6 * math.exp(-0.3 * 0)

LANES = 128
SUBLANES = 8
GATE_PAD = LANES
VMEM_LIMIT = 56 * 1024 * 1024

TM_IN = 512
TQ = 512
TK = 512
TILES_PER_STEP = 4
FIRST_TILES_FOX = (3, 3, 4, 4)
FIRST_TILES_DIFF = (2, 4, 4, 4)
Q_TILES_PER_STEP = 4
TM_FFN = 512

_NT = (((1,), (1,)), ((), ()))


def _rms(x, g):
    return x * lax.rsqrt(jnp.mean(x * x, axis=-1, keepdims=True) + EPS) * g


def _split3(x):
    hi = x.astype(jnp.bfloat16)
    r1 = x - hi.astype(jnp.float32)
    mid = r1.astype(jnp.bfloat16)
    lo = (r1 - mid.astype(jnp.float32)).astype(jnp.bfloat16)
    return hi, mid, lo


def _reduce_rows(x, op):
    while x.shape[0] > 8 and x.shape[0] % 2 == 0:
        half = x.shape[0] // 2
        x = op(x[:half], x[half:])
    reduce = jnp.max if op is jnp.maximum else jnp.sum
    return reduce(x, axis=0, keepdims=True)


def _fox_aug_placement():
    p = np.zeros((LANES, FOX_WIDTH), np.float32)
    for part in range(3):
        for head in range(FOX_HEADS):
            lane = 3 * (head % 2) + part
            p[part * FOX_HEADS + head, (head // 2) * LANES + lane] = 1.0
    return p


def _in_proj_kernel(x_ref, g_ref, wfk_ref, wdk_ref, wfq_ref, wdq_ref, wfv_ref, wdv_ref,
                    wgate_ref, b_ref, place_ref, hsum_ref,
                    q_ref, fk_ref, fv_ref, dk_ref, dv_ref, cend_ref, kn2_ref,
                    tri_ref, carry_ref):
    step = pl.program_id(0)
    tm = x_ref.shape[0]
    f32, bf16 = jnp.float32, jnp.bfloat16

    @pl.when(step == 0)
    def _():
        r = lax.broadcasted_iota(jnp.int32, (tm, tm), 0)
        c = lax.broadcasted_iota(jnp.int32, (tm, tm), 1)
        tri_ref[...] = jnp.where(r >= c, 1.0, 0.0).astype(bf16)
        carry_ref[...] = jnp.zeros_like(carry_ref)
        kn2_ref[...] = jnp.zeros_like(kn2_ref)

    lane = lax.broadcasted_iota(jnp.int32, (1, LANES), 1)
    lo_half = lane < HEAD_DIM
    head_lanes = lane < FOX_HEADS
    w5 = FOX_WIDTH

    def pack3(x):
        hi, mid, lo = (p.astype(f32) for p in _split3(jnp.where(head_lanes, x, 0.0)))
        return (hi + pltpu.roll(mid, FOX_HEADS, axis=1)
                + pltpu.roll(lo, 2 * FOX_HEADS, axis=1)).astype(bf16)

    h = _rms(x_ref[...], g_ref[...]).astype(bf16)

    z = jnp.dot(h, wgate_ref[...], preferred_element_type=f32) + b_ref[...]
    logf = LOG2E * (jnp.minimum(z, 0.0) - jnp.log1p(jnp.exp(-jnp.abs(z))))
    project = lambda w_ref: jnp.dot(h, w_ref[...], preferred_element_type=f32)
    fk = project(wfk_ref).astype(bf16)
    dk = project(wdk_ref).astype(bf16)
    cs = jnp.dot(tri_ref[...], pack3(logf), preferred_element_type=f32)
    c = (cs + pltpu.roll(cs, LANES - FOX_HEADS, axis=1)
         + pltpu.roll(cs, LANES - 2 * FOX_HEADS, axis=1) + carry_ref[0:1, :])
    last = c[tm - 1:tm, :]
    carry_ref[...] = jnp.broadcast_to(last, carry_ref.shape)
    cend_ref[pl.ds(step, 1), :] = last

    q_ref[:, :w5] = (project(wfq_ref) * Q_SCALE).astype(bf16)
    q_ref[:, w5:] = (project(wdq_ref) * Q_SCALE).astype(bf16)
    caug = jnp.dot(pack3(-c), place_ref[...], preferred_element_type=f32).astype(bf16)
    fv = project(wfv_ref).astype(bf16)
    n2 = jnp.dot(jnp.concatenate([fk * fk, dk * dk], axis=1), hsum_ref[...],
                 preferred_element_type=f32)
    kn2_ref[...] = jnp.maximum(kn2_ref[...],
                               jnp.broadcast_to(jnp.max(n2, axis=0, keepdims=True), kn2_ref.shape))
    dv_ref[...] = project(wdv_ref).astype(bf16)

    pos = step * tm + lax.broadcasted_iota(jnp.int32, (tm, LANES), 0)
    pos_lanes = jnp.where(lane < 3, pos & ~(LANES - 1),
                          jnp.where(lane < 6, pos & (LANES - 1), 0)).astype(f32).astype(bf16)
    for g in range(FOX_WIDTH // LANES):
        cols = slice(g * LANES, (g + 1) * LANES)
        key = slice(2 * g * LANES, (2 * g + 1) * LANES)
        bias = slice((2 * g + 1) * LANES, (2 * g + 2) * LANES)
        fk_ref[:, key] = fk[:, cols]
        fk_ref[:, bias] = caug[:, cols]
        dk_ref[:, key] = dk[:, cols]
        dk_ref[:, bias] = pos_lanes
        fv_ref[:, key] = jnp.where(lo_half, fv[:, cols], (lane == HEAD_DIM).astype(bf16))
        fv_ref[:, bias] = jnp.where(lo_half, (lane == 0).astype(bf16), fv[:, cols])


def _in_proj(x, g, weights, b, place, hsum):
    s = x.shape[0]
    n_tiles = s // TM_IN
    const = lambda a: pl.BlockSpec(a.shape, lambda i: (0, 0))
    row_block = lambda width: pl.BlockSpec((TM_IN, width), lambda i: (i, 0))
    bf16 = jnp.bfloat16
    return pl.pallas_call(
        _in_proj_kernel,
        grid=(n_tiles,),
        in_specs=([row_block(D_MODEL), const(g)] + [const(w) for w in weights]
                  + [const(b), const(place), const(hsum)]),
        out_specs=[
            row_block(2 * FOX_WIDTH),
            row_block(2 * FOX_WIDTH),
            row_block(2 * FOX_WIDTH),
            row_block(2 * DIFF_WIDTH),
            row_block(DIFF_WIDTH),
            pl.BlockSpec((n_tiles, LANES), lambda i: (0, 0)),
            pl.BlockSpec((SUBLANES, LANES), lambda i: (0, 0)),
        ],
        out_shape=[
            jax.ShapeDtypeStruct((s, 2 * FOX_WIDTH), bf16),
            jax.ShapeDtypeStruct((s, 2 * FOX_WIDTH), bf16),
            jax.ShapeDtypeStruct((s, 2 * FOX_WIDTH), bf16),
            jax.ShapeDtypeStruct((s, 2 * DIFF_WIDTH), bf16),
            jax.ShapeDtypeStruct((s, DIFF_WIDTH), bf16),
            jax.ShapeDtypeStruct((n_tiles, LANES), jnp.float32),
            jax.ShapeDtypeStruct((SUBLANES, LANES), jnp.float32),
        ],
        scratch_shapes=[
            pltpu.VMEM((TM_IN, TM_IN), bf16),
            pltpu.VMEM((SUBLANES, LANES), jnp.float32),
        ],
        compiler_params=pltpu.CompilerParams(
            dimension_semantics=("arbitrary",), vmem_limit_bytes=VMEM_LIMIT),
        name="in_proj",
    )(x, g, *weights, b, place, hsum)


def _attn_kernel(*refs, fox):
    def one_tile(t, carry):
        rows = pl.ds(pl.multiple_of(t * TQ, TQ), TQ)
        _attn_query_tile(pl.program_id(1) * Q_TILES_PER_STEP + t, rows, refs, fox)
        return carry

    lax.fori_loop(0, Q_TILES_PER_STEP, one_tile, 0)


def _attn_query_tile(i, q_rows, refs, fox):
    if fox:
        q_ref, kaug_ref, vaug_ref, kn2_ref, cend_ref, o_ref, m_ref, acc_ref = refs
    else:
        (q_ref, kaug_ref, v_ref, kn2_ref, lq1_ref, lk1_ref, lq2_ref, lk2_ref, gn_ref,
         o_ref, m_ref, acc_ref, l_ref) = refs
    grp = pl.program_id(0)
    tq, tk = TQ, TK
    n_tiles = kaug_ref.shape[0] // tk
    f32, bf16 = jnp.float32, jnp.bfloat16

    lane = lax.broadcasted_iota(jnp.int32, (1, LANES), 1)
    lo_half = lane < HEAD_DIM

    if not fox:
        slope = LOG2E * jnp.exp2(-2.0 * jnp.full((1, 1), grp + 1, jnp.int32).astype(f32))
        sl = [p.astype(f32) for p in _split3(slope)]
        inv_slope = 1.0 / slope

    def half_norms_sq(x):
        l_in = lax.broadcasted_iota(jnp.int32, (LANES, LANES), 0)
        l_out = lax.broadcasted_iota(jnp.int32, (LANES, LANES), 1)
        half_sum = ((l_in // HEAD_DIM) == l_out).astype(bf16)
        return jnp.dot(x * x, half_sum, preferred_element_type=f32)

    q = q_ref[q_rows, :]
    zero = jnp.zeros_like(q)
    if fox:
        q_bias = [jnp.broadcast_to(((lane >= 3 * h) & (lane < 3 * h + 3)).astype(bf16), q.shape)
                  for h in range(2)]
    else:
        parts = jnp.where((lane == 0) | (lane == 3), sl[0],
                          jnp.where((lane == 1) | (lane == 4), sl[1],
                                    jnp.where((lane == 2) | (lane == 5), sl[2], 0.0)))
        q_bias = [jnp.broadcast_to(parts.astype(bf16), q.shape)] * 2
    q2 = jnp.concatenate(
        [jnp.concatenate([jnp.where(lo_half, q, zero), q_bias[0]], axis=1),
         jnp.concatenate([jnp.where(lo_half, zero, q), q_bias[1]], axis=1)], axis=0)

    m_ref[...] = jnp.full_like(m_ref, NEG_INF)
    acc_ref[...] = jnp.zeros_like(acc_ref)
    if not fox:
        l_ref[...] = jnp.zeros_like(l_ref)

    def tile(j, n_sub, masked):
        width = n_sub * tk
        rows = pl.ds(pl.multiple_of(j * tk, tk), width)
        s = lax.dot_general(q2, kaug_ref[rows, :], _NT, preferred_element_type=f32)
        if masked == "diagonal_last":
            r = lax.broadcasted_iota(jnp.int32, (2 * tq, tk), 0) & (tq - 1)
            c = lax.broadcasted_iota(jnp.int32, (2 * tq, tk), 1)
            diag = jnp.where(c <= r, s[:, width - tk:], NEG_INF)
            s = diag if n_sub == 1 else jnp.concatenate([s[:, :width - tk], diag], axis=1)
        elif masked == "by_position":
            r = lax.broadcasted_iota(jnp.int32, (2 * tq, width), 0) & (tq - 1)
            c = lax.broadcasted_iota(jnp.int32, (2 * tq, width), 1)
            s = jnp.where(c + (j - i) * tk <= r, s, NEG_INF)
        m_prev = m_ref[...]
        m_next = jnp.maximum(m_prev, jnp.max(s, axis=1, keepdims=True))
        alpha = jnp.exp2(m_prev - m_next)
        p = jnp.exp2(s - jnp.tile(m_next, (1, width // LANES)))
        pb = p.astype(bf16)
        if fox:
            pv = jnp.concatenate(
                [jnp.dot(pb[:tq], vaug_ref[rows, :LANES], preferred_element_type=f32),
                 jnp.dot(pb[tq:], vaug_ref[rows, LANES:], preferred_element_type=f32)], axis=0)
        else:
            v = v_ref[rows, :]
            pv = jnp.concatenate([jnp.dot(pb[:tq], v, preferred_element_type=f32),
                                  jnp.dot(pb[tq:], v, preferred_element_type=f32)], axis=0)
            part = p[:, :LANES]
            for blk in range(1, width // LANES):
                part = part + p[:, blk * LANES:(blk + 1) * LANES]
            l_ref[...] = alpha * l_ref[...] + part
        acc_ref[...] = acc_ref[...] * alpha + pv
        m_ref[...] = m_next

    def first_step(first, masked):
        top = jnp.maximum(i - (first - 1), 0)
        qn = half_norms_sq(q)
        first_norm_lane = (0 if fox else FOX_HEADS) + 2 * grp
        kmax = [jnp.sqrt(jnp.sum(jnp.where(lane == first_norm_lane + h, kn2_ref[0:1, :], 0.0),
                                 axis=1, keepdims=True)) * NORM_SLACK for h in range(2)]
        bound = [jnp.sqrt(qn[:, h:h + 1]) * kmax[h] for h in range(2)]
        tile(top, first, masked)
        m_col = [m_ref[h * tq:(h + 1) * tq, 0:1] for h in range(2)]
        if fox:
            head_lane = lax.broadcasted_iota(jnp.int32, cend_ref.shape, 1) - 2 * grp
            cend = [jnp.sum(jnp.where(head_lane == h, cend_ref[...], 0.0), axis=1, keepdims=True)
                    for h in range(2)]
            below_top = lax.broadcasted_iota(jnp.int32, (n_tiles, 1), 0) < top
            dead = below_top
            for h in range(2):
                gap = _reduce_rows(bound[h] - m_col[h], jnp.maximum)
                dead = dead & (-cend[h] < -UNDERFLOW_LOG2 - gap)
            return _reduce_rows(dead.astype(f32), jnp.add).astype(jnp.int32)[0, 0], top
        gap = _reduce_rows(jnp.maximum(bound[0] - m_col[0], bound[1] - m_col[1]), jnp.maximum)
        x = (-UNDERFLOW_LOG2 - gap) * inv_slope
        j_lo = jnp.clip(jnp.floor(x * (1.0 / tk)), 0.0, 1.0 * n_tiles)
        return jnp.minimum(j_lo.astype(jnp.int32)[0, 0], top), top

    per_group = FIRST_TILES_FOX if fox else FIRST_TILES_DIFF
    sizes = sorted(set(per_group))
    which = sum(jnp.where(grp == g, sizes.index(size), 0) for g, size in enumerate(per_group))
    branches = []
    for size in sizes:
        branches += [functools.partial(first_step, size, "diagonal_last"),
                     functools.partial(first_step, size, "by_position")]
    early = i < sum(jnp.where(which == n, size - 1, 0) for n, size in enumerate(sizes))
    j_lo, top = lax.switch(2 * which + early.astype(jnp.int32), branches)

    n_live = top - j_lo
    one = n_live & 1
    two = n_live & 2

    @pl.when(one == 1)
    def _():
        tile(top - 1, 1, masked=False)

    @pl.when(two == 2)
    def _():
        tile(top - one - 2, 2, masked=False)

    def body(step, carry):
        tile(top - one - two - TILES_PER_STEP * (step + 1), TILES_PER_STEP, masked=False)
        return carry

    lax.fori_loop(0, n_live // TILES_PER_STEP, body, 0)

    acc = [acc_ref[h * tq:(h + 1) * tq, :] for h in range(2)]
    if fox:
        out = jnp.where(lo_half, acc[0] / acc[0][:, HEAD_DIM:HEAD_DIM + 1],
                        acc[1] / acc[1][:, 0:1])
    else:
        lam = (jnp.exp(jnp.sum(lq1_ref[...] * lk1_ref[...], axis=1, keepdims=True))
               - jnp.exp(jnp.sum(lq2_ref[...] * lk2_ref[...], axis=1, keepdims=True))
               + LAMBDA_INIT)
        l = jnp.sum(l_ref[...], axis=1, keepdims=True)
        a = acc[0] / l[:tq] - lam * (acc[1] / l[tq:])
        out = _rms(a, gn_ref[...]) * (1.0 - LAMBDA_INIT)
    o_ref[q_rows, :] = out.astype(o_ref.dtype)


def _attention(q, kaug, v, kn2, extra, *, fox):
    s = q.shape[0]
    groups = FOX_HEADS // 2 if fox else DIFF_HEADS
    q_col0 = 0 if fox else FOX_WIDTH // LANES
    resident = lambda width: pl.BlockSpec((s, width), lambda g, i: (0, g))
    small = lambda a: pl.BlockSpec(a.shape, lambda g, i: (0, 0))
    in_specs = [
        pl.BlockSpec((Q_TILES_PER_STEP * TQ, LANES), lambda g, i: (i, q_col0 + g)),
        resident(2 * LANES),
        resident(2 * LANES if fox else LANES),
        small(kn2),
    ] + [small(a) for a in extra]
    scratch = [
        pltpu.VMEM((2 * TQ, LANES), jnp.float32),
        pltpu.VMEM((2 * TQ, LANES), jnp.float32),
    ]
    if not fox:
        scratch.append(pltpu.VMEM((2 * TQ, LANES), jnp.float32))
    return pl.pallas_call(
        functools.partial(_attn_kernel, fox=fox),
        grid=(groups, s // (Q_TILES_PER_STEP * TQ)),
        in_specs=in_specs,
        out_specs=pl.BlockSpec((Q_TILES_PER_STEP * TQ, LANES), lambda g, i: (i, g)),
        out_shape=jax.ShapeDtypeStruct((s, groups * LANES), jnp.bfloat16),
        scratch_shapes=scratch,
        compiler_params=pltpu.CompilerParams(
            dimension_semantics=("arbitrary", "arbitrary"), vmem_limit_bytes=VMEM_LIMIT),
        name="fox_attn" if fox else "diff_attn",
    )(q, kaug, v, kn2, *extra)


def _out_ffn_kernel(x_ref, fo_ref, do_ref, wo_ref, gf_ref, wg_ref, wu_ref, wd_ref,
                    gl_ref, o_ref):
    x1 = (x_ref[...]
          + jnp.dot(fo_ref[...], wo_ref[:FOX_WIDTH, :], preferred_element_type=jnp.float32)
          + jnp.dot(do_ref[...], wo_ref[FOX_WIDTH:, :], preferred_element_type=jnp.float32))
    h = _rms(x1, gf_ref[...]).astype(jnp.bfloat16)
    g = jnp.dot(h, wg_ref[...], preferred_element_type=jnp.float32)
    u = jnp.dot(h, wu_ref[...], preferred_element_type=jnp.float32)
    a = (g * jax.nn.sigmoid(g) * u).astype(jnp.bfloat16)
    x2 = x1 + jnp.dot(a, wd_ref[...], preferred_element_type=jnp.float32)
    o_ref[...] = _rms(x2, gl_ref[...])


def _out_ffn(x, fo, do, wo, gf, wg, wu, wd, gl):
    s = x.shape[0]
    const = lambda shape: pl.BlockSpec(shape, lambda i: (0, 0), pipeline_mode=pl.Buffered(1))
    return pl.pallas_call(
        _out_ffn_kernel,
        grid=(s // TM_FFN,),
        in_specs=[
            pl.BlockSpec((TM_FFN, D_MODEL), lambda i: (i, 0)),
            pl.BlockSpec((TM_FFN, FOX_WIDTH), lambda i: (i, 0)),
            pl.BlockSpec((TM_FFN, DIFF_WIDTH), lambda i: (i, 0)),
            const(wo.shape), const(gf.shape), const(wg.shape), const(wu.shape),
            const(wd.shape), const(gl.shape),
        ],
        out_specs=pl.BlockSpec((TM_FFN, D_MODEL), lambda i: (i, 0)),
        out_shape=jax.ShapeDtypeStruct((s, D_MODEL), jnp.float32),
        compiler_params=pltpu.CompilerParams(
            dimension_semantics=("arbitrary",), vmem_limit_bytes=VMEM_LIMIT),
        name="out_ffn",
    )(x, fo, do, wo, gf, wg, wu, wd, gl)


def kernel(x, mix_norm_g, w_in, b_forget, lambda_q1, lambda_k1, lambda_q2, lambda_k2,
           diff_norm_g, w_out, ffn_norm_g, w_gate, w_up, w_down, final_norm_g):
    b, s, d = x.shape
    assert b == 1 and d == D_MODEL and w_in.shape[0] == 1
    assert s % TQ == 0 and s % TM_IN == 0 and s % TM_FFN == 0
    assert TM_IN == TK and TQ == TK and TILES_PER_STEP == 4
    assert s >= max(FIRST_TILES_FOX + FIRST_TILES_DIFF) * TK and s % (Q_TILES_PER_STEP * TQ) == 0
    assert s < 2 ** 14 + 1
    bf16 = jnp.bfloat16
    x2d = x.reshape(s, d)

    w = w_in[0]
    col = np.cumsum((0, FOX_WIDTH, FOX_WIDTH, FOX_WIDTH, FOX_HEADS, DIFF_WIDTH, DIFF_WIDTH,
                     DIFF_WIDTH))
    fq, fk, fv, gate, dq, dk, dv = (w[:, col[n]:col[n + 1]] for n in range(7))
    order = jnp.argsort(b_forget[0])
    by_head = lambda t: t.reshape(d, FOX_HEADS, HEAD_DIM)[:, order, :].reshape(d, FOX_WIDTH)
    weights = [t.astype(bf16) for t in (
        by_head(fk), dk, by_head(fq), dq, by_head(fv), dv,
        jnp.pad(gate[:, order], ((0, 0), (0, GATE_PAD - FOX_HEADS))))]
    b_pad = jnp.pad(b_forget[0][order].astype(jnp.float32),
                    (0, GATE_PAD - FOX_HEADS)).reshape(1, GATE_PAD)
    w_o = w_out[0]
    w_o = jnp.concatenate(
        [w_o[:FOX_WIDTH].reshape(FOX_HEADS, HEAD_DIM, d)[order].reshape(FOX_WIDTH, d),
         w_o[FOX_WIDTH:]], axis=0).astype(bf16)
    place = jnp.asarray(_fox_aug_placement(), bf16)
    hsum = jnp.asarray(np.arange(2 * FOX_WIDTH)[:, None] // HEAD_DIM == np.arange(LANES)[None, :],
                       bf16)

    q, fox_k, fox_v, diff_k, diff_v, cend, kn2 = _in_proj(
        x2d, mix_norm_g[0].reshape(1, d), weights, b_pad, place, hsum)

    fox_out = _attention(q, fox_k, fox_v, kn2, (cend,), fox=True)
    lam_args = tuple(a[0].astype(jnp.float32).reshape(1, HEAD_DIM)
                     for a in (lambda_q1, lambda_k1, lambda_q2, lambda_k2))
    diff_out = _attention(
        q, diff_k, diff_v, kn2,
        lam_args + (diff_norm_g[0].astype(jnp.float32).reshape(1, DIFF_V_DIM),), fox=False)

    out = _out_ffn(
        x2d, fox_out, diff_out, w_o, ffn_norm_g[0].reshape(1, d),
        w_gate[0].astype(bf16), w_up[0].astype(bf16), w_down[0].astype(bf16),
        final_norm_g.reshape(1, d))
    return out.reshape(b, s, d)
```
